```python
import math
import jax, jax.numpy as jnp
from jax import lax
import numpy as np

D_MODEL = 1024
BATCH = 2
SEQ = 8192
DEPTH = 1
DEC_BATCH = 128
DEC_SEQ = 1
PAST_LEN = 2048
PAGE_SIZE = 128

HG_HEADS = 4
HG_KEY = 128
HG_VAL = 128
HG_WIDTH = HG_HEADS * HG_KEY
HG_OUT = HG_HEADS * HG_VAL
HG_CHUNK = 64
DA_HEADS = 4
DA_HEAD = 64
DA_VAL = 2 * DA_HEAD
DA_QK = DA_HEADS * 2 * DA_HEAD
DA_OUT = DA_HEADS * DA_VAL
MIX_WIDTH = HG_OUT + DA_OUT
Q_BLOCK = 128
ALIBI_SLOPES = tuple(2.0 ** (-8.0 * (h + 1) / DA_HEADS) for h in range(DA_HEADS))
IN_SPLITS = (HG_WIDTH, HG_WIDTH, HG_OUT, HG_OUT, DA_QK, DA_QK, DA_OUT)
IN_WIDTH = sum(IN_SPLITS)
N_EXPERTS = 32
TOP_K = 4
D_FF = D_MODEL
SWIGLU_ALPHA = 1.702
SWIGLU_LIMIT = 7.0
MOE_BLOCK = 128
DEEPNORM_ALPHA = (2 * DEPTH) ** 0.25
DEEPNORM_BETA = (8 * DEPTH) ** -0.25
LN_EPS = 1e-5
RMS_EPS = 1e-6

kernel_name = "hymba_hgrn2_diffattn_moe_step"


def layer_norm(x, g, b):
    xf = x.astype(jnp.float32)
    mu = jnp.mean(xf, -1, keepdims=True)
    var = jnp.mean(jnp.square(xf - mu), -1, keepdims=True)
    return ((xf - mu) * lax.rsqrt(var + LN_EPS) * g + b).astype(x.dtype)


def rms_norm(x, g):
    xf = x.astype(jnp.float32)
    return (xf * lax.rsqrt(jnp.mean(xf * xf, -1, keepdims=True) + RMS_EPS) * g).astype(x.dtype)


def hgrn2_recurrence(q, k, v, log_f, S0):
    B, T, H, dk = q.shape
    dv = v.shape[-1]
    C = math.gcd(T, HG_CHUNK)
    n = T // C

    def to_chunks(a):
        return a.reshape(B, n, C, H, a.shape[-1]).transpose(1, 0, 3, 2, 4)

    tri = jnp.tril(jnp.ones((C, C), bool))

    def step(S, inp):
        qc, kc, vc, gc = inp
        b = jnp.cumsum(gc, axis=2)
        o = jnp.einsum('bhtd,bhdv->bhtv', qc * jnp.exp(b), S)
        decay = jnp.exp(jnp.where(tri[:, :, None], b[:, :, :, None, :] - b[:, :, None, :, :], -jnp.inf))
        scores = jnp.einsum('bhtd,bhsd,bhtsd->bhts', qc, kc, decay)
        o = o + jnp.einsum('bhts,bhsv->bhtv', scores, vc)
        b_last = b[:, :, -1:, :]
        S = jnp.exp(b_last[:, :, 0, :])[..., None] * S + jnp.einsum('bhsd,bhsv->bhdv', kc * jnp.exp(b_last - b), vc)
        return S, o

    S, o = lax.scan(step, S0, (to_chunks(q), to_chunks(k), to_chunks(v), to_chunks(log_f)))
    return o.transpose(1, 0, 3, 2, 4).reshape(B, T, H, dv), S


def hgrn2_mixer(zq, zf, zi, zg, lb, norm_g, S0):
    B, T, _ = zq.shape
    shp_k = (B, T, HG_HEADS, HG_KEY)
    zf = zf.astype(jnp.float32)
    q = jax.nn.silu(zq.astype(jnp.float32)).reshape(shp_k)
    log_f = jnp.logaddexp(jnp.log(lb), jnp.log1p(-lb) + jax.nn.log_sigmoid(zf)).reshape(shp_k)
    k = ((1.0 - lb) * jax.nn.sigmoid(-zf)).reshape(shp_k)
    v = zi.astype(jnp.float32).reshape(B, T, HG_HEADS, HG_VAL)
    o, S = hgrn2_recurrence(q, k, v, log_f, S0.astype(jnp.float32))
    o = rms_norm(o, norm_g) * jax.nn.silu(zg.astype(jnp.float32)).reshape(B, T, HG_HEADS, HG_VAL)
    return o.reshape(B, T, HG_OUT).astype(zq.dtype), S.astype(S0.dtype)


def diff_attention(q, k, v, q_pos, k_pos, lam):
    s = jnp.einsum('bqhmd,bkhmd->bhmqk', q, k).astype(jnp.float32) * (DA_HEAD ** -0.5)
    dist = (q_pos[:, None] - k_pos[None, :]).astype(jnp.float32)
    slopes = jnp.asarray(ALIBI_SLOPES, jnp.float32)
    s = jnp.where(dist >= 0, s - slopes[:, None, None, None] * dist, -jnp.inf)
    p = jax.nn.softmax(s, axis=-1)
    w = p[:, :, 0] - lam * p[:, :, 1]
    return jnp.einsum('bhqk,bkhd->bqhd', w.astype(v.dtype), v)


def diff_attention_blocked(q, k, v, q_pos, k_pos, lam):
    B, T = q.shape[:2]
    qb = math.gcd(T, Q_BLOCK)
    nb = T // qb
    q_blocks = q.reshape(B, nb, qb, DA_HEADS, 2, DA_HEAD).swapaxes(0, 1)
    pos_blocks = q_pos.reshape(nb, qb)
    out = lax.map(lambda a: diff_attention(a[0], k, v, a[1], k_pos, lam), (q_blocks, pos_blocks))
    return out.swapaxes(0, 1).reshape(B, T, DA_HEADS, DA_VAL)


def mixing_sublayer(h, p, lam, lam_init, lb, S0, k_past, v_past):
    B, T, _ = h.shape
    z = h @ p['w_in']
    offs = [int(o) for o in np.cumsum(IN_SPLITS)[:-1]]
    zq, zf, zi, zg, dq, dk, dv = jnp.split(z, offs, axis=-1)
    hg_out, S = hgrn2_mixer(zq, zf, zi, zg, lb, p['hg_norm_g'], S0)
    q = dq.reshape(B, T, DA_HEADS, 2, DA_HEAD)
    k_new = dk.reshape(B, T, DA_HEADS, DA_VAL)
    v_new = dv.reshape(B, T, DA_HEADS, DA_VAL)
    if k_past is None:
        P, k_all, v_all = 0, k_new, v_new
    else:
        P = k_past.shape[1]
        k_all = jnp.concatenate([k_past, k_new], axis=1)
        v_all = jnp.concatenate([v_past, v_new], axis=1)
    q_pos = P + jnp.arange(T)
    k_pos = jnp.arange(P + T)
    da = diff_attention_blocked(q, k_all.reshape(B, P + T, DA_HEADS, 2, DA_HEAD), v_all, q_pos, k_pos, lam)
    da = rms_norm(da, p['da_subln_g']) * (1.0 - lam_init)
    mix = jnp.concatenate([hg_out, da.reshape(B, T, DA_OUT).astype(h.dtype)], axis=-1) @ p['w_out']
    return mix, S, k_new, v_new


def clamped_swiglu(u):
    glu, lin = jnp.split(u, 2, axis=-1)
    glu = jnp.minimum(glu, SWIGLU_LIMIT)
    lin = jnp.clip(lin, -SWIGLU_LIMIT, SWIGLU_LIMIT)
    return glu * jax.nn.sigmoid(SWIGLU_ALPHA * glu) * (lin + 1.0)


def moe_ffn(h, w_router, b_router, w_up, b_up, w_down, b_down):
    N, D = h.shape
    logits = (h @ w_router).astype(jnp.float32) + b_router
    top_val, top_idx = lax.top_k(logits, TOP_K)
    top_w = jax.nn.softmax(top_val, axis=-1)
    A = N * TOP_K
    e_flat = top_idx.reshape(A)
    tok_flat = jnp.repeat(jnp.arange(N, dtype=jnp.int32), TOP_K)
    w_flat = top_w.reshape(A)
    order = jnp.argsort(e_flat)
    e_sorted = e_flat[order]
    counts = jnp.bincount(e_flat, length=N_EXPERTS)
    padded = (counts + MOE_BLOCK - 1) // MOE_BLOCK * MOE_BLOCK
    pad_end = jnp.cumsum(padded)
    pad_start = pad_end - padded
    start = jnp.cumsum(counts) - counts
    dest = pad_start[e_sorted] + jnp.arange(A) - start[e_sorted]
    n_blocks = -(-A // MOE_BLOCK) + N_EXPERTS
    R = n_blocks * MOE_BLOCK
    row_tok = jnp.full((R,), N, jnp.int32).at[dest].set(tok_flat[order])
    row_w = jnp.zeros((R,), jnp.float32).at[dest].set(w_flat[order])
    blk_exp = jnp.minimum(jnp.searchsorted(pad_end, jnp.arange(n_blocks) * MOE_BLOCK, side='right'), N_EXPERTS - 1)
    h_pad = jnp.concatenate([h, jnp.zeros((1, D), h.dtype)], axis=0)
    xb = h_pad[row_tok].reshape(n_blocks, MOE_BLOCK, D)

    def expert_block(args):
        xe, e = args
        u = xe @ w_up[e] + b_up[e]
        return clamped_swiglu(u) @ w_down[e] + b_down[e]

    yb = lax.map(expert_block, (xb, blk_exp)).reshape(R, D)
    y = yb.astype(jnp.float32) * row_w[:, None]
    out = jnp.zeros((N + 1, D), jnp.float32).at[row_tok].add(y)[:N]
    return out.astype(h.dtype)


def decoder_layer(x, c, S0, k_past, v_past, lam, lam_init, lb, p):
    B, T, D = x.shape
    mod = jax.nn.silu(c) @ p['w_ada'] + p['b_ada']
    sh1, sc1, g1, sh2, sc2, g2 = jnp.split(mod[:, None, :], 6, axis=-1)
    h = x * (1.0 + sc1) + sh1
    mix, S, k_new, v_new = mixing_sublayer(h, p, lam, lam_init, lb, S0, k_past, v_past)
    x = layer_norm(DEEPNORM_ALPHA * x + (1.0 + g1) * mix, p['ln1_g'], p['ln1_b'])
    h = x * (1.0 + sc2) + sh2
    ff = moe_ffn(h.reshape(B * T, D), p['w_router'], p['b_router'], p['w_up'], p['b_up'],
                 p['w_down'], p['b_down']).reshape(B, T, D)
    x = layer_norm(DEEPNORM_ALPHA * x + (1.0 + g2) * ff, p['ln2_g'], p['ln2_b'])
    return x, S, k_new, v_new


def setup_inputs(seed: int = 0) -> dict:
    key = jax.random.key(seed)
    ks = jax.random.split(key, 32)
    f32 = jnp.float32
    n_pages = PAST_LEN // PAGE_SIZE
    n_pool = (DEC_BATCH * n_pages * 5) // 4

    def nrm(k, shape, scale):
        return jax.random.normal(k, shape, f32) * scale

    col_scale = jnp.concatenate([jnp.full((n,), s, f32) for n, s in zip(
        IN_SPLITS, (1.0, 1.0, DEEPNORM_BETA, 1.0, 1.0, 1.0, DEEPNORM_BETA))])
    page_table = jax.random.permutation(ks[7], n_pool)[:DEC_BATCH * n_pages].reshape(DEC_BATCH, n_pages).astype(jnp.int32)
    return {
        'x_prompt': nrm(ks[0], (BATCH, SEQ, D_MODEL), 1.0),
        'x_sample': nrm(ks[1], (DEC_BATCH, DEC_SEQ, D_MODEL), 1.0),
        'c_prompt': nrm(ks[2], (BATCH, D_MODEL), 1.0),
        'c_sample': nrm(ks[3], (DEC_BATCH, D_MODEL), 1.0),
        'cache_k': nrm(ks[4], (DEPTH, n_pool, PAGE_SIZE, DA_HEADS, DA_VAL), 1.0),
        'cache_v': nrm(ks[5], (DEPTH, n_pool, PAGE_SIZE, DA_HEADS, DA_VAL), DEEPNORM_BETA),
        'state_hgrn': nrm(ks[6], (DEPTH, DEC_BATCH, HG_HEADS, HG_KEY, HG_VAL), 0.3),
        'page_table': page_table,
        'ln_in_g': 1.0 + nrm(ks[8], (D_MODEL,), 0.02),
        'ln_in_b': nrm(ks[9], (D_MODEL,), 0.02),
        'w_ada': nrm(ks[10], (DEPTH, D_MODEL, 6 * D_MODEL), 0.1 * D_MODEL ** -0.5),
        'b_ada': nrm(ks[11], (DEPTH, 6 * D_MODEL), 0.01),
        'w_in': nrm(ks[12], (DEPTH, D_MODEL, IN_WIDTH), D_MODEL ** -0.5) * col_scale,
        'hg_lb': nrm(ks[13], (DEPTH + 1, HG_WIDTH), 1.0),
        'hg_norm_g': 1.0 + nrm(ks[14], (DEPTH, HG_VAL), 0.02),
        'da_lq1': nrm(ks[15], (DEPTH, DA_HEAD), 0.1),
        'da_lk1': nrm(ks[16], (DEPTH, DA_HEAD), 0.1),
        'da_lq2': nrm(ks[17], (DEPTH, DA_HEAD), 0.1),
        'da_lk2': nrm(ks[18], (DEPTH, DA_HEAD), 0.1),
        'da_subln_g': 1.0 + nrm(ks[19], (DEPTH, DA_VAL), 0.02),
        'w_out': nrm(ks[20], (DEPTH, MIX_WIDTH, D_MODEL), DEEPNORM_BETA * MIX_WIDTH ** -0.5),
        'ln1_g': 1.0 + nrm(ks[21], (DEPTH, D_MODEL), 0.02),
        'ln1_b': nrm(ks[22], (DEPTH, D_MODEL), 0.02),
        'w_router': nrm(ks[23], (DEPTH, D_MODEL, N_EXPERTS), D_MODEL ** -0.5),
        'b_router': nrm(ks[24], (DEPTH, N_EXPERTS), 0.01),
        'w_up': nrm(ks[25], (DEPTH, N_EXPERTS, D_MODEL, 2 * D_FF), DEEPNORM_BETA * D_MODEL ** -0.5),
        'b_up': nrm(ks[26], (DEPTH, N_EXPERTS, 2 * D_FF), 0.01),
        'w_down': nrm(ks[27], (DEPTH, N_EXPERTS, D_FF, D_MODEL), DEEPNORM_BETA * D_FF ** -0.5),
        'b_down': nrm(ks[28], (DEPTH, N_EXPERTS, D_MODEL), 0.01),
        'ln2_g': 1.0 + nrm(ks[29], (DEPTH, D_MODEL), 0.02),
        'ln2_b': nrm(ks[30], (DEPTH, D_MODEL), 0.02),
    }


def reference(x_prompt, x_sample, c_prompt, c_sample, cache_k, cache_v, state_hgrn, page_table,
              ln_in_g, ln_in_b, w_ada, b_ada, w_in, hg_lb, hg_norm_g, da_lq1, da_lk1, da_lq2, da_lk2,
              da_subln_g, w_out, ln1_g, ln1_b, w_router, b_router, w_up, b_up, w_down, b_down,
              ln2_g, ln2_b):
    n_seq, n_pages = page_table.shape
    xp = layer_norm(x_prompt, ln_in_g, ln_in_b)
    xs = layer_norm(x_sample, ln_in_g, ln_in_b)
    lb_all = jnp.cumsum(jax.nn.softmax(hg_lb.astype(jnp.float32), axis=0), axis=0)
    kp_l, vp_l, sp_l, ks_l, vs_l, ss_l = [], [], [], [], [], []
    for l in range(DEPTH):
        p = {'w_ada': w_ada[l], 'b_ada': b_ada[l], 'w_in': w_in[l], 'hg_norm_g': hg_norm_g[l],
             'da_subln_g': da_subln_g[l], 'w_out': w_out[l], 'ln1_g': ln1_g[l], 'ln1_b': ln1_b[l],
             'w_router': w_router[l], 'b_router': b_router[l], 'w_up': w_up[l], 'b_up': b_up[l],
             'w_down': w_down[l], 'b_down': b_down[l], 'ln2_g': ln2_g[l], 'ln2_b': ln2_b[l]}
        lam_init = 0.8 - 0.6 * math.exp(-0.3 * l)
        lam = (jnp.exp(jnp.sum(da_lq1[l].astype(jnp.float32) * da_lk1[l].astype(jnp.float32)))
               - jnp.exp(jnp.sum(da_lq2[l].astype(jnp.float32) * da_lk2[l].astype(jnp.float32))) + lam_init)
        lb = lb_all[l]
        zero_state = jnp.zeros((xp.shape[0], HG_HEADS, HG_KEY, HG_VAL), x_prompt.dtype)
        xp, sp, kp, vp = decoder_layer(xp, c_prompt, zero_state, None, None, lam, lam_init, lb, p)
        k_past = cache_k[l][page_table].reshape(n_seq, n_pages * PAGE_SIZE, DA_HEADS, DA_VAL)
        v_past = cache_v[l][page_table].reshape(n_seq, n_pages * PAGE_SIZE, DA_HEADS, DA_VAL)
        xs, ss, ksn, vsn = decoder_layer(xs, c_sample, state_hgrn[l], k_past, v_past, lam, lam_init, lb, p)
        kp_l.append(kp); vp_l.append(vp); sp_l.append(sp)
        ks_l.append(ksn); vs_l.append(vsn); ss_l.append(ss)
    k_prompt = jnp.stack(kp_l)
    v_prompt = jnp.stack(vp_l)
    hgrn_prompt = jnp.stack(sp_l)
    k_sample = jnp.stack(ks_l)
    v_sample = jnp.stack(vs_l)
    hgrn_sample = jnp.stack(ss_l)
    return (xp, xs, k_prompt, v_prompt, hgrn_prompt, k_sample, v_sample, hgrn_sample)
```

```python
import functools
import math

import numpy as np
import jax
import jax.numpy as jnp
from jax import lax
from jax.experimental import pallas as pl
from jax.experimental.pallas import tpu as pltpu

F32, BF16, I32, U32 = jnp.float32, jnp.bfloat16, jnp.int32, jnp.uint32

D_MODEL = 1024
HEADS = 4
HEAD_W = 128
DA_HEAD = 64
GROUP_W = HEADS * HEAD_W
HG_COLS = 4 * GROUP_W
IN_WIDTH = HG_COLS + 3 * GROUP_W
N_EXPERTS = 32
TOP_K = 4
D_FF = 1024
SWIGLU_ALPHA = 1.702
SWIGLU_LIMIT = 7.0
DEEPNORM_ALPHA = 2.0 ** 0.25
LN_EPS = 1e-5
RMS_EPS = 1e-6
LAM_INIT = 0.8 - 0.6 * math.exp(-0.3 * 0)
ALIBI_SLOPES = tuple(2.0 ** (-8.0 * (h + 1) / HEADS) for h in range(HEADS))
PAGE_SIZE = 128
EXP_CLAMP = 80.0

VMEM_LIMIT = 56 * 1024 * 1024
MOE_BLOCK = 256
HG_CHUNK = 128
ATTN_BLOCK = 256
ROW_BLOCK = 256
TOK_BLOCK = 128


def _cparams(sem):
    return pltpu.CompilerParams(dimension_semantics=sem, vmem_limit_bytes=VMEM_LIMIT)


def _sigmoid(x):
    return 1.0 / (1.0 + jnp.exp(-x))


def _layer_norm(x, g, b):
    mu = jnp.mean(x, -1, keepdims=True)
    xc = x - mu
    var = jnp.mean(xc * xc, -1, keepdims=True)
    return xc * lax.rsqrt(var + LN_EPS) * g + b


def _dot(a, b):
    return jnp.dot(a, b, preferred_element_type=F32)


def _dot_nt(a, b):
    return lax.dot_general(a, b, (((1,), (1,)), ((), ())), preferred_element_type=F32)


def _dot_tn(a, b):
    return lax.dot_general(a, b, (((0,), (0,)), ((), ())), preferred_element_type=F32)


def _pack_bf16_pairs(x):
    w = x.shape[-1] // 2
    lo = pltpu.bitcast(x[:, :w].astype(BF16).astype(F32), U32)
    hi = pltpu.bitcast(x[:, w:].astype(BF16).astype(F32), U32)
    return (lo >> 16) | hi


def _unpack_bf16_pairs(w):
    lo = pltpu.bitcast(w << 16, F32)
    hi = pltpu.bitcast(w & jnp.uint32(0xFFFF0000), F32)
    return lo, hi


def _ada_kernel(c_ref, w_ref, b_ref, o_ref):
    c = c_ref[...]
    a = (c * _sigmoid(c)).astype(BF16)
    o_ref[...] = _dot(a, w_ref[...].astype(BF16)) + b_ref[...]


def _ada(c, w_ada, b_ada):
    rows, d = c.shape
    n = w_ada.shape[1]
    bn = 1536
    return pl.pallas_call(
        _ada_kernel,
        grid=(n // bn,),
        in_specs=[pl.BlockSpec((rows, d), lambda j: (0, 0)),
                  pl.BlockSpec((d, bn), lambda j: (0, j)),
                  pl.BlockSpec((1, bn), lambda j: (0, j))],
        out_specs=pl.BlockSpec((rows, bn), lambda j: (0, j)),
        out_shape=jax.ShapeDtypeStruct((rows, n), F32),
        compiler_params=_cparams(("arbitrary",)),
        name="ada",
    )(c, w_ada, b_ada.reshape(1, n))


def _inproj_kernel(x_ref, g_ref, b_ref, sc_ref, sh_ref, w_ref, zhg_ref, q_ref, kf_ref, vf_ref, kb_ref, vb_ref):
    x0 = _layer_norm(x_ref[...], g_ref[...], b_ref[...])
    h = (x0 * (1.0 + sc_ref[...]) + sh_ref[...]).astype(BF16)
    zhg_ref[...] = _dot(h, w_ref[:, 0:HG_COLS])
    c0 = HG_COLS
    q_ref[...] = (_dot(h, w_ref[:, c0:c0 + GROUP_W]) * (DA_HEAD ** -0.5)).astype(BF16)
    k = _dot(h, w_ref[:, c0 + GROUP_W:c0 + 2 * GROUP_W])
    kf_ref[...] = k
    kb_ref[...] = k.astype(BF16)
    v = _dot(h, w_ref[:, c0 + 2 * GROUP_W:c0 + 3 * GROUP_W])
    vf_ref[...] = v
    vb_ref[...] = v.astype(BF16)


def _mod_spec(mod, bm):
    if mod.shape[1] == 1:
        return pl.BlockSpec((None, 1, mod.shape[2]), lambda g, i: (g, 0, 0))
    return pl.BlockSpec((None, bm, mod.shape[2]), lambda g, i: (g, i, 0))


def _inproj(x, sc, sh, ln_g, ln_b, w_in_bf, bm):
    G, R, D = x.shape
    row = lambda w: pl.BlockSpec((None, bm, w), lambda g, i: (g, i, 0))
    full = lambda a: pl.BlockSpec(a.shape, lambda g, i: (0,) * a.ndim)
    outs = [(HG_COLS, F32), (GROUP_W, BF16), (GROUP_W, F32), (GROUP_W, F32), (GROUP_W, BF16), (GROUP_W, BF16)]
    return pl.pallas_call(
        _inproj_kernel,
        grid=(G, R // bm),
        in_specs=[row(D), full(ln_g), full(ln_b), _mod_spec(sc, bm), _mod_spec(sh, bm), full(w_in_bf)],
        out_specs=[row(w) for w, _ in outs],
        out_shape=[jax.ShapeDtypeStruct((G, R, w), dt) for w, dt in outs],
        compiler_params=_cparams(("arbitrary", "arbitrary")),
        name="inproj",
    )(x, ln_g, ln_b, sc, sh, w_in_bf)


def _hgrn_gates(zq, zf, lb):
    q = zq * _sigmoid(zq)
    f = lb + (1.0 - lb) * _sigmoid(zf)
    k = (1.0 - lb) * _sigmoid(-zf)
    return q, jnp.log(f), k


def _hgrn_kernel(z_ref, lb_ref, ng_ref, lvl_ref, tri_ref, o_ref, sfin_ref, st_ref, *, C):
    t = pl.program_id(1)

    @pl.when(t == 0)
    def _():
        st_ref[...] = jnp.zeros_like(st_ref)

    lvl = lvl_ref[...]
    tri = tri_ref[...]
    n_levels = int(math.log2(C)) - 3
    ng = ng_ref[...]
    for h in range(HEADS):
        cs = slice(h * HEAD_W, (h + 1) * HEAD_W)
        zq = z_ref[:, h * HEAD_W:(h + 1) * HEAD_W]
        zf = z_ref[:, GROUP_W + h * HEAD_W:GROUP_W + (h + 1) * HEAD_W]
        v = z_ref[:, 2 * GROUP_W + h * HEAD_W:2 * GROUP_W + (h + 1) * HEAD_W].astype(BF16)
        zg = z_ref[:, 3 * GROUP_W + h * HEAD_W:3 * GROUP_W + (h + 1) * HEAD_W]
        q, g, k = _hgrn_gates(zq, zf, lb_ref[:, cs])
        g1 = g.astype(BF16)
        r1 = g - g1.astype(F32)
        g2 = r1.astype(BF16)
        g3 = (r1 - g2.astype(F32)).astype(BF16)
        b = _dot(tri, g1) + _dot(tri, g2) + _dot(tri, g3)

        b8 = b.reshape(C // 8, 8, HEAD_W)
        bmid = jnp.broadcast_to(b8[:, 3:4, :], b8.shape).reshape(C, HEAD_W)
        e = jnp.clip(b - bmid, -EXP_CLAMP, EXP_CLAMP)
        a = jnp.where(lvl == 0, _dot_nt((q * jnp.exp(e)).astype(BF16), (k * jnp.exp(-e)).astype(BF16)), 0.0)
        for li in range(1, n_levels + 1):
            m = 4 << li
            bb = b.reshape(C // (2 * m), 2 * m, HEAD_W)
            d = b - jnp.broadcast_to(bb[:, m - 1:m, :], bb.shape).reshape(C, HEAD_W)
            qs = (q * jnp.exp(jnp.minimum(d, 0.0))).astype(BF16)
            ks = (k * jnp.exp(jnp.minimum(-d, 0.0))).astype(BF16)
            a = jnp.where(lvl == li, _dot_nt(qs, ks), a)

        st = st_ref[h]
        o = _dot(a.astype(BF16), v) + _dot_nt((q * jnp.exp(b)).astype(BF16), st.astype(BF16))
        b_last = b[C - 1:C, :]
        kd = (k * jnp.exp(b_last - b)).astype(BF16)
        st_ref[h] = st * jnp.exp(b_last) + _dot_tn(v, kd)

        ms = jnp.mean(o * o, -1, keepdims=True)
        o_ref[:, cs] = (o * lax.rsqrt(ms + RMS_EPS) * ng * (zg * _sigmoid(zg))).astype(BF16)

    @pl.when(t == pl.num_programs(1) - 1)
    def _():
        for h in range(HEADS):
            sfin_ref[h] = st_ref[h].T


def _hgrn_level_table(C):
    t = np.arange(C)[:, None]
    s = np.arange(C)[None, :]
    x = t ^ s
    lvl = np.zeros((C, C), np.int32)
    m = 8
    while m < C:
        lvl += (x >= m).astype(np.int32)
        m *= 2
    return np.where(s <= t, lvl, -1).astype(np.int32)


def _hgrn_prompt(zhg, lb, norm_g):
    B, T, _ = zhg.shape
    C = HG_CHUNK
    lvl = jnp.asarray(_hgrn_level_table(C))
    tri = jnp.asarray(np.tril(np.ones((C, C), np.float32)), BF16)
    full = lambda a: pl.BlockSpec(a.shape, lambda b, t: (0,) * a.ndim)
    return pl.pallas_call(
        functools.partial(_hgrn_kernel, C=C),
        grid=(B, T // C),
        in_specs=[pl.BlockSpec((None, C, HG_COLS), lambda b, t: (b, t, 0)), full(lb), full(norm_g), full(lvl), full(tri)],
        out_specs=[pl.BlockSpec((None, C, GROUP_W), lambda b, t: (b, t, 0)),
                   pl.BlockSpec((None, HEADS, HEAD_W, HEAD_W), lambda b, t: (b, 0, 0, 0))],
        out_shape=[jax.ShapeDtypeStruct((B, T, GROUP_W), BF16),
                   jax.ShapeDtypeStruct((B, HEADS, HEAD_W, HEAD_W), F32)],
        scratch_shapes=[pltpu.VMEM((HEADS, HEAD_W, HEAD_W), F32)],
        compiler_params=_cparams(("arbitrary", "arbitrary")),
        name="hgrn_prompt",
    )(zhg, lb, norm_g, lvl, tri)


def _hgrn_step_kernel(z_ref, s_ref, lb_ref, ng_ref, o_ref, so_ref, *, G):
    ng = ng_ref[...]
    for h in range(HEADS):
        cs = slice(h * HEAD_W, (h + 1) * HEAD_W)
        zq = z_ref[:, h * HEAD_W:(h + 1) * HEAD_W]
        zf = z_ref[:, GROUP_W + h * HEAD_W:GROUP_W + (h + 1) * HEAD_W]
        v = z_ref[:, 2 * GROUP_W + h * HEAD_W:2 * GROUP_W + (h + 1) * HEAD_W]
        zg = z_ref[:, 3 * GROUP_W + h * HEAD_W:3 * GROUP_W + (h + 1) * HEAD_W]
        lb = lb_ref[:, cs]
        q = zq * _sigmoid(zq)
        f = lb + (1.0 - lb) * _sigmoid(zf)
        k = (1.0 - lb) * _sigmoid(-zf)
        qT, fT, kT = q.T, f.T, k.T
        rows = []
        for j in range(G):
            s_new = fT[:, j:j + 1] * s_ref[j, h] + kT[:, j:j + 1] * v[j:j + 1, :]
            so_ref[j, h] = s_new
            rows.append(jnp.sum(s_new * qT[:, j:j + 1], axis=0, keepdims=True))
        o = jnp.concatenate(rows, axis=0)
        ms = jnp.mean(o * o, -1, keepdims=True)
        o_ref[:, cs] = (o * lax.rsqrt(ms + RMS_EPS) * ng * (zg * _sigmoid(zg))).astype(BF16)


def _hgrn_sample(zhg, state, lb, norm_g):
    N = zhg.shape[0]
    G = 8
    full = lambda a: pl.BlockSpec(a.shape, lambda i: (0,) * a.ndim)
    st_spec = pl.BlockSpec((G, HEADS, HEAD_W, HEAD_W), lambda i: (i, 0, 0, 0))
    return pl.pallas_call(
        functools.partial(_hgrn_step_kernel, G=G),
        grid=(N // G,),
        in_specs=[pl.BlockSpec((G, HG_COLS), lambda i: (i, 0)), st_spec, full(lb), full(norm_g)],
        out_specs=[pl.BlockSpec((G, GROUP_W), lambda i: (i, 0)), st_spec],
        out_shape=[jax.ShapeDtypeStruct((N, GROUP_W), BF16), jax.ShapeDtypeStruct(state.shape, F32)],
        compiler_params=_cparams(("arbitrary",)),
        name="hgrn_sample",
    )(zhg, state, lb, norm_g)


def _attn_kernel(q_ref, k_ref, v_ref, slope_ref, lam_ref, g_ref, o_ref, m_ref, l_ref, acc_ref, *, blk):
    qi = pl.program_id(2)
    q_start = qi * blk
    q = q_ref[...]
    slope = slope_ref[...]
    m_ref[...] = jnp.full_like(m_ref, -jnp.inf)
    l_ref[...] = jnp.zeros_like(l_ref)
    acc_ref[...] = jnp.zeros_like(acc_ref)

    def block(kj, masked):
        k_start = pl.multiple_of(kj * blk, blk)
        kb = k_ref[pl.ds(k_start, blk), :]
        vb = v_ref[pl.ds(k_start, blk), :]
        kpos = k_start + lax.broadcasted_iota(I32, (1, blk), 1)
        bias = slope * (kpos - q_start).astype(F32)
        if masked:
            ok = kpos <= q_start + lax.broadcasted_iota(I32, (blk, 1), 0)
        for mi in range(2):
            ds = slice(mi * DA_HEAD, (mi + 1) * DA_HEAD)
            s = _dot_nt(q[:, ds], kb[:, ds]) + bias
            if masked:
                s = jnp.where(ok, s, -jnp.inf)
            m_prev = m_ref[mi]
            m_new = jnp.maximum(m_prev, jnp.max(s, -1, keepdims=True))
            p = jnp.exp(s - m_new)
            alpha = jnp.exp(m_prev - m_new)
            l_ref[mi] = alpha * l_ref[mi] + jnp.sum(p, -1, keepdims=True)
            acc_ref[mi] = alpha * acc_ref[mi] + _dot(p.astype(BF16), vb)
            m_ref[mi] = m_new

    block(qi, True)

    def body(kj, c):
        block(kj, False)
        return c

    lax.fori_loop(0, qi, body, 0)

    o = acc_ref[0] / l_ref[0] - lam_ref[...] * (acc_ref[1] / l_ref[1])
    ms = jnp.mean(o * o, -1, keepdims=True)
    o_ref[...] = (o * lax.rsqrt(ms + RMS_EPS) * g_ref[...] * (1.0 - LAM_INIT)).astype(BF16)


def _attn_prompt(q, k, v, lam_row, subln_g):
    B, T, _ = q.shape
    blk = min(ATTN_BLOCK, T)
    slopes = jnp.asarray(np.repeat(np.asarray(ALIBI_SLOPES, np.float32)[:, None, None], blk, axis=2))
    full = lambda a: pl.BlockSpec(a.shape, lambda b, h, i: (0,) * a.ndim)
    seq = pl.BlockSpec((None, T, HEAD_W), lambda b, h, i: (b, 0, h))
    return pl.pallas_call(
        functools.partial(_attn_kernel, blk=blk),
        grid=(B, HEADS, T // blk),
        in_specs=[pl.BlockSpec((None, blk, HEAD_W), lambda b, h, i: (b, i, h)), seq, seq,
                  pl.BlockSpec((None, 1, blk), lambda b, h, i: (h, 0, 0)), full(lam_row), full(subln_g)],
        out_specs=pl.BlockSpec((None, blk, HEAD_W), lambda b, h, i: (b, i, h)),
        out_shape=jax.ShapeDtypeStruct((B, T, GROUP_W), BF16),
        scratch_shapes=[pltpu.VMEM((2, blk, 1), F32), pltpu.VMEM((2, blk, 1), F32), pltpu.VMEM((2, blk, HEAD_W), F32)],
        compiler_params=_cparams(("arbitrary", "arbitrary", "arbitrary")),
        name="attn_prompt",
    )(q, k, v, slopes, lam_row, subln_g)


def _attn_decode_kernel(pt_ref, q_ref, kn_ref, vn_ref, bias_ref, lam_ref, g_ref, *rest, n_pages):
    k_refs = rest[:n_pages]
    v_refs = rest[n_pages:2 * n_pages]
    o_ref = rest[2 * n_pages]
    del pt_ref
    r8 = lax.broadcasted_iota(I32, (8, GROUP_W), 0)
    c8 = lax.broadcasted_iota(I32, (8, GROUP_W), 1)
    qmat = jnp.where(c8 // DA_HEAD == r8, jnp.broadcast_to(q_ref[...].astype(F32), (8, GROUP_W)), 0.0)
    qmat_bf = qmat.astype(BF16)
    s_parts = [_dot_nt(qmat_bf, k_refs[p][...].astype(BF16)) for p in range(n_pages)]
    s = jnp.concatenate(s_parts, axis=1) + bias_ref[...]
    kn = kn_ref[...].astype(BF16).astype(F32)
    s_self = jnp.sum(qmat_bf.astype(F32) * kn, axis=-1, keepdims=True)
    m = jnp.maximum(jnp.max(s, -1, keepdims=True), s_self)
    p = jnp.exp(s - m)
    p_self = jnp.exp(s_self - m)
    inv_l = 1.0 / (jnp.sum(p, -1, keepdims=True) + p_self)
    coef = jnp.where(lax.broadcasted_iota(I32, (8, 1), 0) % 2 == 0, 1.0, -lam_ref[:, 0:1]) * inv_l
    w = (p * coef).astype(BF16)
    acc = jnp.zeros((8, GROUP_W), F32)
    for pg in range(n_pages):
        acc = acc + _dot(w[:, pg * PAGE_SIZE:(pg + 1) * PAGE_SIZE], v_refs[pg][...].astype(BF16))
    acc = acc + (p_self * coef) * vn_ref[...]
    o = jnp.sum(jnp.where(c8 // HEAD_W == r8 // 2, acc, 0.0), axis=0, keepdims=True)
    outs = []
    for h in range(HEADS):
        oh = o[:, h * HEAD_W:(h + 1) * HEAD_W]
        ms = jnp.mean(oh * oh, -1, keepdims=True)
        outs.append(oh * lax.rsqrt(ms + RMS_EPS) * g_ref[...] * (1.0 - LAM_INIT))
    o_ref[...] = jnp.concatenate(outs, axis=1).astype(BF16)


def _attn_sample(q, k_new, v_new, cache_k, cache_v, page_table, lam_row, subln_g):
    N, n_pages = page_table.shape
    past = n_pages * PAGE_SIZE
    ck = cache_k.reshape(cache_k.shape[0], PAGE_SIZE, GROUP_W)
    cv = cache_v.reshape(cache_v.shape[0], PAGE_SIZE, GROUP_W)
    kpos = np.arange(past, dtype=np.float32)[None, :]
    slope_rows = np.repeat(np.asarray(ALIBI_SLOPES, np.float32), 2)[:, None]
    bias = jnp.asarray(-slope_rows * (past - kpos))
    row = pl.BlockSpec((None, 1, GROUP_W), lambda i, pt: (i, 0, 0))
    full = lambda a: pl.BlockSpec(a.shape, lambda i, pt: (0,) * a.ndim)

    def page_spec(p):
        return pl.BlockSpec((None, PAGE_SIZE, GROUP_W), lambda i, pt: (pt[i, p], 0, 0))

    grid_spec = pltpu.PrefetchScalarGridSpec(
        num_scalar_prefetch=1,
        grid=(N,),
        in_specs=[row, row, row, full(bias), full(lam_row), full(subln_g)]
                 + [page_spec(p) for p in range(n_pages)] + [page_spec(p) for p in range(n_pages)],
        out_specs=row,
    )
    r3 = lambda a: a.reshape(N, 1, GROUP_W)
    out = pl.pallas_call(
        functools.partial(_attn_decode_kernel, n_pages=n_pages),
        grid_spec=grid_spec,
        out_shape=jax.ShapeDtypeStruct((N, 1, GROUP_W), BF16),
        compiler_params=_cparams(("arbitrary",)),
        name="attn_sample",
    )(page_table, r3(q), r3(k_new), r3(v_new), bias, lam_row, subln_g, *([ck] * n_pages), *([cv] * n_pages))
    return out.reshape(N, GROUP_W)


def _outproj_kernel(hg_ref, da_ref, x_ref, lg_ref, lb_ref, g1_ref, sc2_ref, sh2_ref, w_ref, l1g_ref, l1b_ref,
                    wrh_ref, wrl_ref, br_ref, u_ref, cin_ref,
                    x1_ref, h2_ref, idx_ref, tw_ref, rank_ref, cout_ref, run_ref):
    @pl.when((pl.program_id(0) == 0) & (pl.program_id(1) == 0))
    def _():
        run_ref[...] = cin_ref[...]

    x0 = _layer_norm(x_ref[...], lg_ref[...], lb_ref[...])
    mix = _dot(hg_ref[...], w_ref[0:GROUP_W, :]) + _dot(da_ref[...], w_ref[GROUP_W:2 * GROUP_W, :])
    x1 = _layer_norm(DEEPNORM_ALPHA * x0 + (1.0 + g1_ref[...]) * mix, l1g_ref[...], l1b_ref[...])
    x1_ref[...] = x1
    h2 = x1 * (1.0 + sc2_ref[...]) + sh2_ref[...]
    h2_ref[...] = _pack_bf16_pairs(h2)

    hi = h2.astype(BF16)
    lo = (h2 - hi.astype(F32)).astype(BF16)
    wrh = wrh_ref[...]
    logits = _dot_nt(wrh, hi) + _dot_nt(wrh, lo) + _dot_nt(wrl_ref[...], hi) + br_ref[...]

    n_e, bm = logits.shape
    rows = lax.broadcasted_iota(I32, (n_e, bm), 0).astype(F32)
    vals, sels = [], []
    work = logits
    for kk in range(TOP_K):
        mx = jnp.max(work, axis=0, keepdims=True)
        ix = jnp.min(jnp.where(work == mx, rows, float(n_e)), axis=0, keepdims=True)
        sel = rows == ix
        idx_ref[kk:kk + 1, :] = ix.astype(I32)
        vals.append(mx)
        sels.append(sel)
        work = jnp.where(sel, -jnp.inf, work)
    es = [jnp.exp(vv - vals[0]) for vv in vals]
    inv = 1.0 / (es[0] + es[1] + es[2] + es[3])
    for kk in range(TOP_K):
        tw_ref[kk:kk + 1, :] = es[kk] * inv

    base = run_ref[...]
    for kk in range(TOP_K):
        oh = jnp.where(sels[kk], 1.0, 0.0)
        before = base + _dot(oh.astype(BF16), u_ref[...])
        rank_ref[kk:kk + 1, :] = jnp.sum(jnp.where(sels[kk], before, 0.0), axis=0, keepdims=True).astype(I32)
        base = base + jnp.sum(oh, axis=1, keepdims=True)
    run_ref[...] = base
    cout_ref[...] = base


def _outproj(hg, da, x, mods, consts, counts_in, bm):
    G, R, D = x.shape
    g1, sc2, sh2 = mods
    ln_g, ln_b, w_out_bf, l1g, l1b, wrh, wrl, br = consts
    nb = R // bm
    n_tok = G * R
    u = jnp.asarray(np.triu(np.ones((bm, bm), np.float32), 1), BF16)
    row = lambda w: pl.BlockSpec((None, bm, w), lambda g, i: (g, i, 0))
    full = lambda a: pl.BlockSpec(a.shape, lambda g, i: (0,) * a.ndim)
    tok_lanes = pl.BlockSpec((TOP_K, bm), lambda g, i: (0, g * nb + i))
    return pl.pallas_call(
        _outproj_kernel,
        grid=(G, nb),
        in_specs=[row(GROUP_W), row(GROUP_W), row(D), full(ln_g), full(ln_b),
                  _mod_spec(g1, bm), _mod_spec(sc2, bm), _mod_spec(sh2, bm),
                  full(w_out_bf), full(l1g), full(l1b), full(wrh), full(wrl), full(br), full(u), full(counts_in)],
        out_specs=[row(D), pl.BlockSpec((bm, D // 2), lambda g, i: (g * nb + i, 0)),
                   tok_lanes, tok_lanes, tok_lanes, full(counts_in)],
        out_shape=[jax.ShapeDtypeStruct((G, R, D), F32),
                   jax.ShapeDtypeStruct((n_tok, D // 2), U32),
                   jax.ShapeDtypeStruct((TOP_K, n_tok), I32),
                   jax.ShapeDtypeStruct((TOP_K, n_tok), F32),
                   jax.ShapeDtypeStruct((TOP_K, n_tok), I32),
                   jax.ShapeDtypeStruct(counts_in.shape, F32)],
        scratch_shapes=[pltpu.VMEM(counts_in.shape, F32)],
        compiler_params=_cparams(("arbitrary", "arbitrary")),
        name="outproj",
    )(hg, da, x, ln_g, ln_b, g1, sc2, sh2, w_out_bf, l1g, l1b, wrh, wrl, br, u, counts_in)


def _dispatch_kernel(dest_ref, h2_hbm, xb_in_hbm, xb_hbm, sem, *, bt):
    del xb_in_hbm
    tok0 = pl.program_id(0) * bt

    def body(t, c):
        for kk in range(TOP_K):
            pltpu.make_async_copy(h2_hbm.at[tok0 + t], xb_hbm.at[dest_ref[kk, t]], sem).start()
        return c

    lax.fori_loop(0, bt, body, 0)
    pltpu.make_async_copy(h2_hbm.at[pl.ds(0, TOP_K * bt)], xb_hbm.at[pl.ds(0, TOP_K * bt)], sem).wait()


def _dispatch(dest, h2, xb):
    n_tok = h2.shape[0]
    bt = TOK_BLOCK
    anyspec = pl.BlockSpec(memory_space=pl.ANY)
    return pl.pallas_call(
        functools.partial(_dispatch_kernel, bt=bt),
        grid=(n_tok // bt,),
        in_specs=[pl.BlockSpec((TOP_K, bt), lambda i: (0, i), memory_space=pltpu.SMEM), anyspec, anyspec],
        out_specs=anyspec,
        out_shape=jax.ShapeDtypeStruct(xb.shape, U32),
        scratch_shapes=[pltpu.SemaphoreType.DMA(())],
        input_output_aliases={2: 0},
        compiler_params=_cparams(("arbitrary",)),
        name="dispatch",
    )(dest, h2, xb)


def _experts_kernel(be_ref, nu_ref, x_ref, wu_ref, bu_ref, wd_ref, bd_ref, y_ref):
    del be_ref

    @pl.when(pl.program_id(0) < nu_ref[0])
    def _():
        xl, xh = _unpack_bf16_pairs(x_ref[...])
        half = D_MODEL // 2
        u = _dot(xl.astype(BF16), wu_ref[0:half, :]) + _dot(xh.astype(BF16), wu_ref[half:, :]) + bu_ref[...]
        glu = jnp.minimum(u[:, :D_FF], SWIGLU_LIMIT)
        lin = jnp.clip(u[:, D_FF:], -SWIGLU_LIMIT, SWIGLU_LIMIT)
        act = glu * _sigmoid(SWIGLU_ALPHA * glu) * (lin + 1.0)
        y = _dot(act.astype(BF16), wd_ref[...]) + bd_ref[...]
        y_ref[...] = _pack_bf16_pairs(y)

    @pl.when(pl.program_id(0) >= nu_ref[0])
    def _():
        y_ref[...] = jnp.zeros_like(y_ref)


def _experts(blk_exp, n_used, xb, w_up_bf, b_up, w_down_bf, b_down):
    n_rows, w = xb.shape
    bm = MOE_BLOCK
    nb = n_rows // bm
    grid_spec = pltpu.PrefetchScalarGridSpec(
        num_scalar_prefetch=2,
        grid=(nb,),
        in_specs=[pl.BlockSpec((bm, w), lambda j, be, nu: (jnp.minimum(j, nu[0] - 1), 0)),
                  pl.BlockSpec((None, D_MODEL, 2 * D_FF), lambda j, be, nu: (be[j], 0, 0)),
                  pl.BlockSpec((None, 1, 2 * D_FF), lambda j, be, nu: (be[j], 0, 0)),
                  pl.BlockSpec((None, D_FF, D_MODEL), lambda j, be, nu: (be[j], 0, 0)),
                  pl.BlockSpec((None, 1, D_MODEL), lambda j, be, nu: (be[j], 0, 0))],
        out_specs=pl.BlockSpec((bm, w), lambda j, be, nu: (j, 0)),
    )
    return pl.pallas_call(
        _experts_kernel,
        grid_spec=grid_spec,
        out_shape=jax.ShapeDtypeStruct((n_rows, w), U32),
        compiler_params=_cparams(("arbitrary",)),
        name="experts",
    )(blk_exp, n_used, xb, w_up_bf, b_up.reshape(N_EXPERTS, 1, 2 * D_FF), w_down_bf, b_down.reshape(N_EXPERTS, 1, D_MODEL))


def _combine_kernel(dcur_ref, dnext_ref, x1_ref, tw_ref, g2_ref, lg_ref, lb_ref, yb_hbm, o_ref, buf, sems, *, bm):
    i = pl.program_id(0) * pl.num_programs(1) + pl.program_id(1)
    n = pl.num_programs(0) * pl.num_programs(1)
    slot = i % 2

    def issue(dref, s):
        def body(t, c):
            for kk in range(TOP_K):
                pltpu.make_async_copy(yb_hbm.at[dref[kk, t]], buf.at[s, kk, t], sems.at[s]).start()
            return c
        lax.fori_loop(0, bm, body, 0)

    @pl.when(i == 0)
    def _():
        issue(dcur_ref, 0)

    @pl.when(i + 1 < n)
    def _():
        issue(dnext_ref, 1 - slot)

    for kk in range(TOP_K):
        pltpu.make_async_copy(yb_hbm.at[pl.ds(0, bm)], buf.at[slot, kk], sems.at[slot]).wait()

    tw = tw_ref[...]
    ff_lo = jnp.zeros((bm, D_MODEL // 2), F32)
    ff_hi = jnp.zeros((bm, D_MODEL // 2), F32)
    for kk in range(TOP_K):
        lo, hi = _unpack_bf16_pairs(buf[slot, kk])
        ff_lo = ff_lo + tw[:, kk:kk + 1] * lo
        ff_hi = ff_hi + tw[:, kk:kk + 1] * hi
    ff = jnp.concatenate([ff_lo, ff_hi], axis=1)
    o_ref[...] = _layer_norm(DEEPNORM_ALPHA * x1_ref[...] + (1.0 + g2_ref[...]) * ff, lg_ref[...], lb_ref[...])


def _combine(dest, tw_rows, x1, g2, ln_g, ln_b, yb, bm):
    G, R, D = x1.shape
    nb = R // bm
    n_blk = G * nb
    row = lambda w: pl.BlockSpec((None, bm, w), lambda g, i: (g, i, 0))
    full = lambda a: pl.BlockSpec(a.shape, lambda g, i: (0,) * a.ndim)
    cur = pl.BlockSpec((TOP_K, bm), lambda g, i: (0, g * nb + i), memory_space=pltpu.SMEM)
    nxt = pl.BlockSpec((TOP_K, bm), lambda g, i: (0, jnp.minimum(g * nb + i + 1, n_blk - 1)),
                       memory_space=pltpu.SMEM)
    return pl.pallas_call(
        functools.partial(_combine_kernel, bm=bm),
        grid=(G, nb),
        in_specs=[cur, nxt, row(D), pl.BlockSpec((bm, TOP_K), lambda g, i: (g * nb + i, 0)),
                  _mod_spec(g2, bm), full(ln_g), full(ln_b), pl.BlockSpec(memory_space=pl.ANY)],
        out_specs=row(D),
        out_shape=jax.ShapeDtypeStruct((G, R, D), F32),
        scratch_shapes=[pltpu.VMEM((2, TOP_K, bm, D // 2), U32), pltpu.SemaphoreType.DMA((2,))],
        compiler_params=_cparams(("arbitrary", "arbitrary")),
        name="combine",
    )(dest, dest, x1, tw_rows, g2, ln_g, ln_b, yb)


def kernel(x_prompt, x_sample, c_prompt, c_sample, cache_k, cache_v, state_hgrn, page_table, ln_in_g, ln_in_b, w_ada, b_ada, w_in, hg_lb, hg_norm_g, da_lq1, da_lk1, da_lq2, da_lk2, da_subln_g, w_out, ln1_g, ln1_b, w_router, b_router, w_up, b_up, w_down, b_down, ln2_g, ln2_b):
    assert w_in.shape[0] == 1, "single-layer trunk"
    B, T, D = x_prompt.shape
    NS = x_sample.shape[0]
    n_prompt = B * T
    n_tot = n_prompt + NS
    r1 = lambda a: a.reshape(1, -1)

    lb = r1(jax.nn.softmax(hg_lb.astype(F32), axis=0)[0])
    lam = (jnp.exp(jnp.sum(da_lq1[0].astype(F32) * da_lk1[0].astype(F32)))
           - jnp.exp(jnp.sum(da_lq2[0].astype(F32) * da_lk2[0].astype(F32))) + LAM_INIT)
    lam_row = jnp.full((1, HEAD_W), lam, F32)
    ln_g, ln_b = r1(ln_in_g), r1(ln_in_b)
    w_in_bf = w_in[0].astype(BF16)
    w_out_bf = w_out[0].astype(BF16)
    w_up_bf = w_up[0].astype(BF16)
    w_down_bf = w_down[0].astype(BF16)
    wr_t = w_router[0].T
    wrh = wr_t.astype(BF16)
    wrl = (wr_t - wrh.astype(F32)).astype(BF16)
    br = b_router[0].reshape(N_EXPERTS, 1)
    norm_g, subln_g = r1(hg_norm_g[0]), r1(da_subln_g[0])

    n_c = B + NS
    c_all = jnp.concatenate([c_prompt, c_sample, jnp.zeros((-n_c % 8, D), F32)], axis=0)
    mod = _ada(c_all, w_ada[0], b_ada[0])
    mod_p = [mod[:B, j * D:(j + 1) * D].reshape(B, 1, D) for j in range(6)]
    mod_s = [mod[B:n_c, j * D:(j + 1) * D].reshape(1, NS, D) for j in range(6)]

    zhg_p, q_p, kf_p, vf_p, kb_p, vb_p = _inproj(x_prompt, mod_p[1], mod_p[0], ln_g, ln_b, w_in_bf, ROW_BLOCK)
    hg_p, s_p = _hgrn_prompt(zhg_p, lb, norm_g)
    da_p = _attn_prompt(q_p, kb_p, vb_p, lam_row, subln_g)

    xs = x_sample.reshape(1, NS, D)
    zhg_s, q_s, kf_s, vf_s, _, _ = _inproj(xs, mod_s[1], mod_s[0], ln_g, ln_b, w_in_bf, NS)
    hg_s, s_s = _hgrn_sample(zhg_s[0], state_hgrn[0], lb, norm_g)
    da_s = _attn_sample(q_s[0], kf_s[0], vf_s[0], cache_k[0], cache_v[0], page_table, lam_row, subln_g)

    consts = (ln_g, ln_b, w_out_bf, r1(ln1_g[0]), r1(ln1_b[0]), wrh, wrl, br)
    counts0 = jnp.zeros((N_EXPERTS, 1), F32)
    x1_p, h2_p, idx_p, tw_p, rank_p, counts1 = _outproj(
        hg_p, da_p, x_prompt, (mod_p[2], mod_p[4], mod_p[3]), consts, counts0, ROW_BLOCK)
    x1_s, h2_s, idx_s, tw_s, rank_s, counts = _outproj(
        hg_s.reshape(1, NS, GROUP_W), da_s.reshape(1, NS, GROUP_W), xs, (mod_s[2], mod_s[4], mod_s[3]), consts,
        counts1, NS)

    cnt = counts[:, 0].astype(I32)
    padded = (cnt + MOE_BLOCK - 1) // MOE_BLOCK * MOE_BLOCK
    pad_end = jnp.cumsum(padded)
    pad_start = pad_end - padded
    dest_p = jnp.take(pad_start, idx_p) + rank_p
    dest_s = jnp.take(pad_start, idx_s) + rank_s
    n_blocks = -(-n_tot * TOP_K // MOE_BLOCK) + N_EXPERTS
    blk_exp = jnp.minimum(jnp.searchsorted(pad_end, jnp.arange(n_blocks, dtype=I32) * MOE_BLOCK, side='right'),
                          N_EXPERTS - 1).astype(I32)
    n_used = (pad_end[-1:] // MOE_BLOCK).astype(I32)

    xb = jnp.zeros((n_blocks * MOE_BLOCK, D // 2), U32)
    xb = _dispatch(dest_p, h2_p, xb)
    xb = _dispatch(dest_s, h2_s, xb)
    yb = _experts(blk_exp, n_used, xb, w_up_bf, b_up[0], w_down_bf, b_down[0])

    l2g, l2b = r1(ln2_g[0]), r1(ln2_b[0])
    y_p = _combine(dest_p, tw_p.T, x1_p, mod_p[5], l2g, l2b, yb, TOK_BLOCK)
    y_s = _combine(dest_s, tw_s.T, x1_s, mod_s[5], l2g, l2b, yb, TOK_BLOCK)

    shp = lambda a, n, t: a.reshape(1, n, t, HEADS, HEAD_W)
    return (y_p, y_s.reshape(NS, 1, D),
            shp(kf_p, B, T), shp(vf_p, B, T), s_p[None],
            shp(kf_s, NS, 1), shp(vf_s, NS, 1), s_s[None])
```

```python
import functools
import math

import numpy as np
import jax
import jax.numpy as jnp
from jax import lax
from jax.experimental import pallas as pl
from jax.experimental.pallas import tpu as pltpu

F32, BF16, I32, U32 = jnp.float32, jnp.bfloat16, jnp.int32, jnp.uint32

D_MODEL = 1024
HEADS = 4
HEAD_W = 128
DA_HEAD = 64
GROUP_W = HEADS * HEAD_W
HG_COLS = 4 * GROUP_W
IN_WIDTH = HG_COLS + 3 * GROUP_W
N_EXPERTS = 32
TOP_K = 4
D_FF = 1024
SWIGLU_ALPHA = 1.702
SWIGLU_LIMIT = 7.0
DEEPNORM_ALPHA = 2.0 ** 0.25
LN_EPS = 1e-5
RMS_EPS = 1e-6
LAM_INIT = 0.8 - 0.6 * math.exp(-0.3 * 0)
LOG2E = math.log2(math.e)
ALIBI_SLOPES = tuple(2.0 ** (-8.0 * (h + 1) / HEADS) for h in range(HEADS))
PAGE_SIZE = 128
EXP_CLAMP = 80.0

VMEM_LIMIT = 56 * 1024 * 1024
MOE_BLOCK = 256
HG_CHUNK = 128
ATTN_BLOCK = 512
ROW_BLOCK = 256
TOK_BLOCK = 128


def _cparams(sem):
    return pltpu.CompilerParams(dimension_semantics=sem, vmem_limit_bytes=VMEM_LIMIT)


def _sigmoid(x):
    return 1.0 / (1.0 + jnp.exp(-x))


def _layer_norm(x, g, b):
    mu = jnp.mean(x, -1, keepdims=True)
    xc = x - mu
    var = jnp.mean(xc * xc, -1, keepdims=True)
    return xc * lax.rsqrt(var + LN_EPS) * g + b


def _dot(a, b):
    return jnp.dot(a, b, preferred_element_type=F32)


def _dot_nt(a, b):
    return lax.dot_general(a, b, (((1,), (1,)), ((), ())), preferred_element_type=F32)


def _dot_tn(a, b):
    return lax.dot_general(a, b, (((0,), (0,)), ((), ())), preferred_element_type=F32)


def _ada_kernel(c_ref, w_ref, b_ref, o_ref):
    c = c_ref[...]
    a = (c * _sigmoid(c)).astype(BF16)
    o_ref[...] = _dot(a, w_ref[...].astype(BF16)) + b_ref[...]


def _ada(c, w_ada, b_ada):
    rows, d = c.shape
    n = w_ada.shape[1]
    bn = 1536
    return pl.pallas_call(
        _ada_kernel,
        grid=(n // bn,),
        in_specs=[pl.BlockSpec((rows, d), lambda j: (0, 0)),
                  pl.BlockSpec((d, bn), lambda j: (0, j)),
                  pl.BlockSpec((1, bn), lambda j: (0, j))],
        out_specs=pl.BlockSpec((rows, bn), lambda j: (0, j)),
        out_shape=jax.ShapeDtypeStruct((rows, n), F32),
        compiler_params=_cparams(("arbitrary",)),
        name="ada",
    )(c, w_ada, b_ada.reshape(1, n))


def _inproj_kernel(x_ref, g_ref, b_ref, sc_ref, sh_ref, w_ref, wqt_ref, wvt_ref,
                   zhg_ref, q_ref, kf_ref, vf_ref, kb_ref, qt_ref, vt_ref):
    x0 = _layer_norm(x_ref[...], g_ref[...], b_ref[...])
    h = (x0 * (1.0 + sc_ref[...]) + sh_ref[...]).astype(BF16)
    zhg_ref[...] = _dot(h, w_ref[:, 0:HG_COLS])
    c0 = HG_COLS
    q_ref[...] = (_dot(h, w_ref[:, c0:c0 + GROUP_W]) * (DA_HEAD ** -0.5)).astype(BF16)
    k = _dot(h, w_ref[:, c0 + GROUP_W:c0 + 2 * GROUP_W])
    kb_ref[...] = k.astype(BF16)
    v = _dot(h, w_ref[:, c0 + 2 * GROUP_W:c0 + 3 * GROUP_W])
    for hd in range(HEADS):
        kf_ref[:, hd, :] = k[:, hd * HEAD_W:(hd + 1) * HEAD_W]
        vf_ref[:, hd, :] = v[:, hd * HEAD_W:(hd + 1) * HEAD_W]
    qt_ref[...] = (_dot_nt(wqt_ref[...], h) * (DA_HEAD ** -0.5 * LOG2E)).astype(BF16)
    vt_ref[...] = _dot_nt(wvt_ref[...], h).astype(BF16)


def _mod_spec(mod, bm):
    if mod.shape[1] == 1:
        return pl.BlockSpec((None, 1, mod.shape[2]), lambda g, i: (g, 0, 0))
    return pl.BlockSpec((None, bm, mod.shape[2]), lambda g, i: (g, i, 0))


def _inproj(x, sc, sh, ln_g, ln_b, w_in_bf, wqt, wvt, bm):
    G, R, D = x.shape
    nb = R // bm
    row = lambda w: pl.BlockSpec((None, bm, w), lambda g, i: (g, i, 0))
    full = lambda a: pl.BlockSpec(a.shape, lambda g, i: (0,) * a.ndim)
    heads = pl.BlockSpec((None, bm, HEADS, HEAD_W), lambda g, i: (g, i, 0, 0))
    heads_shape = jax.ShapeDtypeStruct((G, R, HEADS, HEAD_W), F32)
    tr = pl.BlockSpec((None, None, GROUP_W, bm), lambda g, i: (g, i, 0, 0))
    tr_shape = jax.ShapeDtypeStruct((G, nb, GROUP_W, bm), BF16)
    sds = lambda w, dt: jax.ShapeDtypeStruct((G, R, w), dt)
    return pl.pallas_call(
        _inproj_kernel,
        grid=(G, nb),
        in_specs=[row(D), full(ln_g), full(ln_b), _mod_spec(sc, bm), _mod_spec(sh, bm), full(w_in_bf), full(wqt), full(wvt)],
        out_specs=[row(HG_COLS), row(GROUP_W), heads, heads, row(GROUP_W), tr, tr],
        out_shape=[sds(HG_COLS, F32), sds(GROUP_W, BF16), heads_shape, heads_shape, sds(GROUP_W, BF16), tr_shape, tr_shape],
        compiler_params=_cparams(("arbitrary", "arbitrary")),
        name="inproj",
    )(x, ln_g, ln_b, sc, sh, w_in_bf, wqt, wvt)


def _hgrn_gates(zq, zf, lb):
    q = zq * _sigmoid(zq)
    f = lb + (1.0 - lb) * _sigmoid(zf)
    k = (1.0 - lb) * _sigmoid(-zf)
    return q, jnp.log(f), k


def _hgrn_kernel(z_ref, lb_ref, ng_ref, lvl_ref, tri_ref, o_ref, sfin_ref, st_ref, *, C):
    t = pl.program_id(1)

    @pl.when(t == 0)
    def _():
        st_ref[...] = jnp.zeros_like(st_ref)

    lvl = lvl_ref[...]
    tri = tri_ref[...]
    n_levels = int(math.log2(C)) - 3
    ng = ng_ref[...]
    for h in range(HEADS):
        cs = slice(h * HEAD_W, (h + 1) * HEAD_W)
        zq = z_ref[:, h * HEAD_W:(h + 1) * HEAD_W]
        zf = z_ref[:, GROUP_W + h * HEAD_W:GROUP_W + (h + 1) * HEAD_W]
        v = z_ref[:, 2 * GROUP_W + h * HEAD_W:2 * GROUP_W + (h + 1) * HEAD_W].astype(BF16)
        zg = z_ref[:, 3 * GROUP_W + h * HEAD_W:3 * GROUP_W + (h + 1) * HEAD_W]
        q, g, k = _hgrn_gates(zq, zf, lb_ref[:, cs])
        g1 = g.astype(BF16)
        r1 = g - g1.astype(F32)
        g2 = r1.astype(BF16)
        g3 = (r1 - g2.astype(F32)).astype(BF16)
        b = _dot(tri, g1) + _dot(tri, g2) + _dot(tri, g3)

        b8 = b.reshape(C // 8, 8, HEAD_W)
        bmid = jnp.broadcast_to(b8[:, 3:4, :], b8.shape).reshape(C, HEAD_W)
        e = jnp.clip(b - bmid, -EXP_CLAMP, EXP_CLAMP)
        a = jnp.where(lvl == 0, _dot_nt((q * jnp.exp(e)).astype(BF16), (k * jnp.exp(-e)).astype(BF16)), 0.0)
        for li in range(1, n_levels + 1):
            m = 4 << li
            bb = b.reshape(C // (2 * m), 2 * m, HEAD_W)
            d = b - jnp.broadcast_to(bb[:, m - 1:m, :], bb.shape).reshape(C, HEAD_W)
            qs = (q * jnp.exp(jnp.minimum(d, 0.0))).astype(BF16)
            ks = (k * jnp.exp(jnp.minimum(-d, 0.0))).astype(BF16)
            a = jnp.where(lvl == li, _dot_nt(qs, ks), a)

        st = st_ref[h]
        o = _dot(a.astype(BF16), v) + _dot_nt((q * jnp.exp(b)).astype(BF16), st.astype(BF16))
        b_last = b[C - 1:C, :]
        kd = (k * jnp.exp(b_last - b)).astype(BF16)
        st_ref[h] = st * jnp.exp(b_last) + _dot_tn(v, kd)

        ms = jnp.mean(o * o, -1, keepdims=True)
        o_ref[:, cs] = (o * lax.rsqrt(ms + RMS_EPS) * ng * (zg * _sigmoid(zg))).astype(BF16)

    @pl.when(t == pl.num_programs(1) - 1)
    def _():
        for h in range(HEADS):
            sfin_ref[h] = st_ref[h].T


def _hgrn_level_table(C):
    t = np.arange(C)[:, None]
    s = np.arange(C)[None, :]
    x = t ^ s
    lvl = np.zeros((C, C), np.int32)
    m = 8
    while m < C:
        lvl += (x >= m).astype(np.int32)
        m *= 2
    return np.where(s <= t, lvl, -1).astype(np.int32)


def _hgrn_prompt(zhg, lb, norm_g):
    B, T, _ = zhg.shape
    C = HG_CHUNK
    lvl = jnp.asarray(_hgrn_level_table(C))
    tri = jnp.asarray(np.tril(np.ones((C, C), np.float32)), BF16)
    full = lambda a: pl.BlockSpec(a.shape, lambda b, t: (0,) * a.ndim)
    return pl.pallas_call(
        functools.partial(_hgrn_kernel, C=C),
        grid=(B, T // C),
        in_specs=[pl.BlockSpec((None, C, HG_COLS), lambda b, t: (b, t, 0)), full(lb), full(norm_g), full(lvl), full(tri)],
        out_specs=[pl.BlockSpec((None, C, GROUP_W), lambda b, t: (b, t, 0)),
                   pl.BlockSpec((None, HEADS, HEAD_W, HEAD_W), lambda b, t: (b, 0, 0, 0))],
        out_shape=[jax.ShapeDtypeStruct((B, T, GROUP_W), BF16),
                   jax.ShapeDtypeStruct((B, HEADS, HEAD_W, HEAD_W), F32)],
        scratch_shapes=[pltpu.VMEM((HEADS, HEAD_W, HEAD_W), F32)],
        compiler_params=_cparams(("arbitrary", "arbitrary")),
        name="hgrn_prompt",
    )(zhg, lb, norm_g, lvl, tri)


def _hgrn_step_kernel(z_ref, s_ref, lb_ref, ng_ref, o_ref, so_ref, *, G):
    ng = ng_ref[...]
    for h in range(HEADS):
        cs = slice(h * HEAD_W, (h + 1) * HEAD_W)
        zq = z_ref[:, h * HEAD_W:(h + 1) * HEAD_W]
        zf = z_ref[:, GROUP_W + h * HEAD_W:GROUP_W + (h + 1) * HEAD_W]
        v = z_ref[:, 2 * GROUP_W + h * HEAD_W:2 * GROUP_W + (h + 1) * HEAD_W]
        zg = z_ref[:, 3 * GROUP_W + h * HEAD_W:3 * GROUP_W + (h + 1) * HEAD_W]
        lb = lb_ref[:, cs]
        q = zq * _sigmoid(zq)
        f = lb + (1.0 - lb) * _sigmoid(zf)
        k = (1.0 - lb) * _sigmoid(-zf)
        qT, fT, kT = q.T, f.T, k.T
        rows = []
        for j in range(G):
            s_new = fT[:, j:j + 1] * s_ref[j, h] + kT[:, j:j + 1] * v[j:j + 1, :]
            so_ref[j, h] = s_new
            rows.append(jnp.sum(s_new * qT[:, j:j + 1], axis=0, keepdims=True))
        o = jnp.concatenate(rows, axis=0)
        ms = jnp.mean(o * o, -1, keepdims=True)
        o_ref[:, cs] = (o * lax.rsqrt(ms + RMS_EPS) * ng * (zg * _sigmoid(zg))).astype(BF16)


def _hgrn_sample(zhg, state, lb, norm_g):
    N = zhg.shape[0]
    G = 8
    full = lambda a: pl.BlockSpec(a.shape, lambda i: (0,) * a.ndim)
    st_spec = pl.BlockSpec((G, HEADS, HEAD_W, HEAD_W), lambda i: (i, 0, 0, 0))
    return pl.pallas_call(
        functools.partial(_hgrn_step_kernel, G=G),
        grid=(N // G,),
        in_specs=[pl.BlockSpec((G, HG_COLS), lambda i: (i, 0)), st_spec, full(lb), full(norm_g)],
        out_specs=[pl.BlockSpec((G, GROUP_W), lambda i: (i, 0)), st_spec],
        out_shape=[jax.ShapeDtypeStruct((N, GROUP_W), BF16), jax.ShapeDtypeStruct(state.shape, F32)],
        compiler_params=_cparams(("arbitrary",)),
        name="hgrn_sample",
    )(zhg, state, lb, norm_g)


def _attn_kernel(qt_ref, k_ref, vt_ref, sl_ref, feat_ref, lam_ref, gcol_ref, o_ref, m_ref, l_ref, acc_ref, *, blk):
    qi = pl.program_id(2)
    row = lax.broadcasted_iota(I32, (HEAD_W, blk), 0)
    qt = qt_ref[...].astype(F32)
    a_hi = sl_ref[0:1, :]
    a_lo = sl_ref[1:2, :]
    slope2 = sl_ref[2:3, :]
    def slope_rows(r0):
        in_rows = (row >= r0) & (row < r0 + 4)
        return jnp.where(in_rows, jnp.where((row - r0) % 2 == 0, a_hi, a_lo), 0.0)

    q_aug = [jnp.where(row < DA_HEAD, qt, slope_rows(DA_HEAD)).astype(BF16),
             jnp.where(row >= DA_HEAD, qt, slope_rows(0)).astype(BF16)]
    keep = [feat_ref[0] > 0, feat_ref[1] > 0]
    feats = [feat_ref[2], feat_ref[3]]
    m_ref[...] = jnp.full_like(m_ref, -jnp.inf)
    l_ref[...] = jnp.zeros_like(l_ref)
    acc_ref[...] = jnp.zeros_like(acc_ref)

    def block(kj, masked):
        k_start = pl.multiple_of(kj * blk, blk)
        kb = k_ref[pl.ds(k_start, blk), :]
        vt = vt_ref[kj]
        off = slope2 * jnp.full((1, blk), k_start - qi * blk, I32).astype(F32)
        if masked:
            ok = lax.broadcasted_iota(I32, (blk, 1), 0) <= lax.broadcasted_iota(I32, (1, blk), 1)
        for mi in range(2):
            st = _dot(jnp.where(keep[mi], kb, feats[mi]), q_aug[mi])
            if masked:
                st = jnp.where(ok, st, -jnp.inf)
            m_prev = m_ref[mi]
            m_new = jnp.maximum(m_prev, jnp.max(st, axis=0, keepdims=True) + off)
            p = jnp.exp2(st - (m_new - off))
            alpha = jnp.exp2(m_prev - m_new)
            l_ref[mi] = alpha * l_ref[mi] + jnp.sum(p, axis=0, keepdims=True)
            acc_ref[mi] = alpha * acc_ref[mi] + _dot(vt, p.astype(BF16))
            m_ref[mi] = m_new

    block(qi, True)

    def body(kj, c):
        block(kj, False)
        return c

    lax.fori_loop(0, qi, body, 0)

    ot = acc_ref[0] * (1.0 / l_ref[0]) - lam_ref[...] * (acc_ref[1] * (1.0 / l_ref[1]))
    ms = jnp.mean(ot * ot, axis=0, keepdims=True)
    o_ref[...] = (ot * lax.rsqrt(ms + RMS_EPS) * gcol_ref[...]).T.astype(BF16)


def _attn_prompt(qt, k, vt, lam, subln_g):
    B, nblk, _, blk = qt.shape
    T = k.shape[1]
    bf = lambda x: np.asarray(x, np.float32).astype(BF16).astype(np.float64)
    a = np.asarray(ALIBI_SLOPES, np.float64) * LOG2E
    a_hi = bf(a)
    a_lo = bf(a - a_hi)
    sl = np.zeros((HEADS, 8, blk), np.float32)
    sl[:, 0, :], sl[:, 1, :], sl[:, 2, :] = a_hi[:, None], a_lo[:, None], a.astype(np.float32)[:, None]
    lane = np.arange(HEAD_W)[None, :]
    r = np.arange(blk)[:, None]
    assert blk <= 256 * 256

    def pos_feat(l0):
        return (np.where((lane == l0) | (lane == l0 + 1), r // 256 * 256, 0)
                + np.where((lane == l0 + 2) | (lane == l0 + 3), r % 256, 0)).astype(np.float32)

    feat = np.stack([np.broadcast_to(lane < DA_HEAD, (blk, HEAD_W)).astype(np.float32),
                     np.broadcast_to(lane >= DA_HEAD, (blk, HEAD_W)).astype(np.float32),
                     pos_feat(DA_HEAD), pos_feat(0)])
    feat = jnp.asarray(feat, BF16)
    lam_row = jnp.full((1, blk), lam, F32)
    gcol = jnp.broadcast_to((subln_g.reshape(HEAD_W, 1) * (1.0 - LAM_INIT)), (HEAD_W, blk))
    full = lambda x: pl.BlockSpec(x.shape, lambda b, h, i: (0,) * x.ndim)
    return pl.pallas_call(
        functools.partial(_attn_kernel, blk=blk),
        grid=(B, HEADS, nblk),
        in_specs=[pl.BlockSpec((None, None, HEAD_W, blk), lambda b, h, i: (b, i, h, 0)),
                  pl.BlockSpec((None, T, HEAD_W), lambda b, h, i: (b, 0, h)),
                  pl.BlockSpec((None, nblk, HEAD_W, blk), lambda b, h, i: (b, 0, h, 0)),
                  pl.BlockSpec((None, 8, blk), lambda b, h, i: (h, 0, 0)),
                  full(feat), full(lam_row), full(gcol)],
        out_specs=pl.BlockSpec((None, blk, HEAD_W), lambda b, h, i: (b, i, h)),
        out_shape=jax.ShapeDtypeStruct((B, T, GROUP_W), BF16),
        scratch_shapes=[pltpu.VMEM((2, 1, blk), F32), pltpu.VMEM((2, 1, blk), F32), pltpu.VMEM((2, HEAD_W, blk), F32)],
        compiler_params=_cparams(("arbitrary", "arbitrary", "arbitrary")),
        name="attn_prompt",
    )(qt, k, vt, jnp.asarray(sl), feat, lam_row, gcol)


def _attn_decode_kernel(pt_ref, q_ref, kn_ref, vn_ref, bias_ref, lam_ref, g_ref, *rest, n_pages):
    k_refs = rest[:n_pages]
    v_refs = rest[n_pages:2 * n_pages]
    o_ref = rest[2 * n_pages]
    del pt_ref
    r8 = lax.broadcasted_iota(I32, (8, GROUP_W), 0)
    c8 = lax.broadcasted_iota(I32, (8, GROUP_W), 1)
    qmat = jnp.where(c8 // DA_HEAD == r8, jnp.broadcast_to(q_ref[...].astype(F32), (8, GROUP_W)), 0.0)
    qmat_bf = qmat.astype(BF16)
    q_heads = [qmat_bf[:, h * HEAD_W:(h + 1) * HEAD_W] for h in range(HEADS)]
    s_parts = []
    for pg in range(n_pages):
        sp = _dot_nt(q_heads[0], k_refs[pg][:, 0, :].astype(BF16))
        for h in range(1, HEADS):
            sp = sp + _dot_nt(q_heads[h], k_refs[pg][:, h, :].astype(BF16))
        s_parts.append(sp)
    s = jnp.concatenate(s_parts, axis=1) + bias_ref[...]
    kn = kn_ref[...].astype(BF16).astype(F32)
    s_self = jnp.sum(qmat_bf.astype(F32) * kn, axis=-1, keepdims=True)
    m = jnp.maximum(jnp.max(s, -1, keepdims=True), s_self)
    p = jnp.exp(s - m)
    p_self = jnp.exp(s_self - m)
    inv_l = 1.0 / (jnp.sum(p, -1, keepdims=True) + p_self)
    coef = jnp.where(lax.broadcasted_iota(I32, (8, 1), 0) % 2 == 0, 1.0, -lam_ref[:, 0:1]) * inv_l
    w = (p * coef).astype(BF16)
    accs = []
    for h in range(HEADS):
        acc = jnp.zeros((8, HEAD_W), F32)
        for pg in range(n_pages):
            acc = acc + _dot(w[:, pg * PAGE_SIZE:(pg + 1) * PAGE_SIZE], v_refs[pg][:, h, :].astype(BF16))
        accs.append(acc)
    acc = jnp.concatenate(accs, axis=1) + (p_self * coef) * vn_ref[...]
    o = jnp.sum(jnp.where(c8 // HEAD_W == r8 // 2, acc, 0.0), axis=0, keepdims=True)
    outs = []
    for h in range(HEADS):
        oh = o[:, h * HEAD_W:(h + 1) * HEAD_W]
        ms = jnp.mean(oh * oh, -1, keepdims=True)
        outs.append(oh * lax.rsqrt(ms + RMS_EPS) * g_ref[...] * (1.0 - LAM_INIT))
    o_ref[...] = jnp.concatenate(outs, axis=1).astype(BF16)


def _attn_sample(q, k_new, v_new, cache_k, cache_v, page_table, lam_row, subln_g):
    N, n_pages = page_table.shape
    past = n_pages * PAGE_SIZE
    kpos = np.arange(past, dtype=np.float32)[None, :]
    slope_rows = np.repeat(np.asarray(ALIBI_SLOPES, np.float32), 2)[:, None]
    bias = jnp.asarray(-slope_rows * (past - kpos))
    row = pl.BlockSpec((None, 1, GROUP_W), lambda i, pt: (i, 0, 0))
    full = lambda a: pl.BlockSpec(a.shape, lambda i, pt: (0,) * a.ndim)

    def page_spec(p):
        return pl.BlockSpec((None, None, PAGE_SIZE, HEADS, HEAD_W), lambda i, pt: (0, pt[i, p], 0, 0, 0))

    grid_spec = pltpu.PrefetchScalarGridSpec(
        num_scalar_prefetch=1,
        grid=(N,),
        in_specs=[row, row, row, full(bias), full(lam_row), full(subln_g)]
                 + [page_spec(p) for p in range(n_pages)] + [page_spec(p) for p in range(n_pages)],
        out_specs=row,
    )
    r3 = lambda a: a.reshape(N, 1, GROUP_W)
    out = pl.pallas_call(
        functools.partial(_attn_decode_kernel, n_pages=n_pages),
        grid_spec=grid_spec,
        out_shape=jax.ShapeDtypeStruct((N, 1, GROUP_W), BF16),
        compiler_params=_cparams(("arbitrary",)),
        name="attn_sample",
    )(page_table, r3(q), r3(k_new), r3(v_new), bias, lam_row, subln_g, *([cache_k] * n_pages), *([cache_v] * n_pages))
    return out.reshape(N, GROUP_W)


def _outproj_kernel(hg_ref, da_ref, x_ref, lg_ref, lb_ref, g1_ref, sc2_ref, sh2_ref, w_ref, l1g_ref, l1b_ref,
                    wrh_ref, wrl_ref, br_ref, u_ref, cin_ref,
                    x1_ref, h2_ref, idx_ref, tw_ref, rank_ref, cout_ref, run_ref):
    @pl.when((pl.program_id(0) == 0) & (pl.program_id(1) == 0))
    def _():
        run_ref[...] = cin_ref[...]

    x0 = _layer_norm(x_ref[...], lg_ref[...], lb_ref[...])
    mix = _dot(hg_ref[...], w_ref[0:GROUP_W, :]) + _dot(da_ref[...], w_ref[GROUP_W:2 * GROUP_W, :])
    x1 = _layer_norm(DEEPNORM_ALPHA * x0 + (1.0 + g1_ref[...]) * mix, l1g_ref[...], l1b_ref[...])
    x1_ref[...] = x1
    h2 = x1 * (1.0 + sc2_ref[...]) + sh2_ref[...]
    h2_ref[...] = h2

    hi = h2.astype(BF16)
    lo = (h2 - hi.astype(F32)).astype(BF16)
    wrh = wrh_ref[...]
    logits = _dot_nt(wrh, hi) + _dot_nt(wrh, lo) + _dot_nt(wrl_ref[...], hi) + br_ref[...]

    n_e, bm = logits.shape
    rows = lax.broadcasted_iota(I32, (n_e, bm), 0).astype(F32)
    vals, sels = [], []
    work = logits
    for kk in range(TOP_K):
        mx = jnp.max(work, axis=0, keepdims=True)
        ix = jnp.min(jnp.where(work == mx, rows, float(n_e)), axis=0, keepdims=True)
        sel = rows == ix
        idx_ref[kk:kk + 1, :] = ix.astype(I32)
        vals.append(mx)
        sels.append(sel)
        work = jnp.where(sel, -jnp.inf, work)
    es = [jnp.exp(vv - vals[0]) for vv in vals]
    inv = 1.0 / (es[0] + es[1] + es[2] + es[3])
    for kk in range(TOP_K):
        tw_ref[kk:kk + 1, :] = es[kk] * inv

    base = run_ref[...]
    for kk in range(TOP_K):
        oh = jnp.where(sels[kk], 1.0, 0.0)
        before = base + _dot(oh.astype(BF16), u_ref[...])
        rank_ref[kk:kk + 1, :] = jnp.sum(jnp.where(sels[kk], before, 0.0), axis=0, keepdims=True).astype(I32)
        base = base + jnp.sum(oh, axis=1, keepdims=True)
    run_ref[...] = base
    cout_ref[...] = base


def _outproj(hg, da, x, mods, consts, counts_in, bm):
    G, R, D = x.shape
    g1, sc2, sh2 = mods
    ln_g, ln_b, w_out_bf, l1g, l1b, wrh, wrl, br = consts
    nb = R // bm
    n_tok = G * R
    u = jnp.asarray(np.triu(np.ones((bm, bm), np.float32), 1), BF16)
    row = lambda w: pl.BlockSpec((None, bm, w), lambda g, i: (g, i, 0))
    full = lambda a: pl.BlockSpec(a.shape, lambda g, i: (0,) * a.ndim)
    tok_lanes = pl.BlockSpec((TOP_K, bm), lambda g, i: (0, g * nb + i))
    return pl.pallas_call(
        _outproj_kernel,
        grid=(G, nb),
        in_specs=[row(GROUP_W), row(GROUP_W), row(D), full(ln_g), full(ln_b),
                  _mod_spec(g1, bm), _mod_spec(sc2, bm), _mod_spec(sh2, bm),
                  full(w_out_bf), full(l1g), full(l1b), full(wrh), full(wrl), full(br), full(u), full(counts_in)],
        out_specs=[row(D), pl.BlockSpec((bm, D), lambda g, i: (g * nb + i, 0)),
                   tok_lanes, tok_lanes, tok_lanes, full(counts_in)],
        out_shape=[jax.ShapeDtypeStruct((G, R, D), F32),
                   jax.ShapeDtypeStruct((n_tok, D), F32),
                   jax.ShapeDtypeStruct((TOP_K, n_tok), I32),
                   jax.ShapeDtypeStruct((TOP_K, n_tok), F32),
                   jax.ShapeDtypeStruct((TOP_K, n_tok), I32),
                   jax.ShapeDtypeStruct(counts_in.shape, F32)],
        scratch_shapes=[pltpu.VMEM(counts_in.shape, F32)],
        compiler_params=_cparams(("arbitrary", "arbitrary")),
        name="outproj",
    )(hg, da, x, ln_g, ln_b, g1, sc2, sh2, w_out_bf, l1g, l1b, wrh, wrl, br, u, counts_in)


def _dispatch_kernel(dest_ref, h2_ref, xb_in_hbm, xb_hbm, sem, *, bt):
    del xb_in_hbm

    def body(t, c):
        for kk in range(TOP_K):
            pltpu.make_async_copy(h2_ref.at[t], xb_hbm.at[dest_ref[kk, t]], sem).start()
        return c

    lax.fori_loop(0, bt, body, 0)
    for kk in range(TOP_K):
        pltpu.make_async_copy(h2_ref, xb_hbm.at[pl.ds(0, bt)], sem).wait()


def _dispatch(dest, h2, xb):
    n_tok = h2.shape[0]
    bt = TOK_BLOCK
    anyspec = pl.BlockSpec(memory_space=pl.ANY)
    return pl.pallas_call(
        functools.partial(_dispatch_kernel, bt=bt),
        grid=(n_tok // bt,),
        in_specs=[pl.BlockSpec((TOP_K, bt), lambda i: (0, i), memory_space=pltpu.SMEM),
                  pl.BlockSpec((bt, h2.shape[1]), lambda i: (i, 0)), anyspec],
        out_specs=anyspec,
        out_shape=jax.ShapeDtypeStruct(xb.shape, xb.dtype),
        scratch_shapes=[pltpu.SemaphoreType.DMA(())],
        input_output_aliases={2: 0},
        compiler_params=_cparams(("arbitrary",)),
        name="dispatch",
    )(dest, h2, xb)


def _experts_kernel(be_ref, nu_ref, x_ref, wu_ref, bu_ref, wd_ref, bd_ref, y_ref):
    del be_ref

    @pl.when(pl.program_id(0) < nu_ref[0])
    def _():
        u = _dot(x_ref[...].astype(BF16), wu_ref[...]) + bu_ref[...]
        glu = jnp.minimum(u[:, :D_FF], SWIGLU_LIMIT)
        lin = jnp.clip(u[:, D_FF:], -SWIGLU_LIMIT, SWIGLU_LIMIT)
        act = glu * _sigmoid(SWIGLU_ALPHA * glu) * (lin + 1.0)
        y_ref[...] = _dot(act.astype(BF16), wd_ref[...]) + bd_ref[...]

    @pl.when(pl.program_id(0) >= nu_ref[0])
    def _():
        y_ref[...] = jnp.zeros_like(y_ref)


def _experts(blk_exp, n_used, xb, w_up_bf, b_up, w_down_bf, b_down):
    n_rows, w = xb.shape
    bm = MOE_BLOCK
    nb = n_rows // bm
    grid_spec = pltpu.PrefetchScalarGridSpec(
        num_scalar_prefetch=2,
        grid=(nb,),
        in_specs=[pl.BlockSpec((bm, w), lambda j, be, nu: (jnp.minimum(j, nu[0] - 1), 0)),
                  pl.BlockSpec((None, D_MODEL, 2 * D_FF), lambda j, be, nu: (be[j], 0, 0)),
                  pl.BlockSpec((None, 1, 2 * D_FF), lambda j, be, nu: (be[j], 0, 0)),
                  pl.BlockSpec((None, D_FF, D_MODEL), lambda j, be, nu: (be[j], 0, 0)),
                  pl.BlockSpec((None, 1, D_MODEL), lambda j, be, nu: (be[j], 0, 0))],
        out_specs=pl.BlockSpec((bm, w), lambda j, be, nu: (j, 0)),
    )
    return pl.pallas_call(
        _experts_kernel,
        grid_spec=grid_spec,
        out_shape=jax.ShapeDtypeStruct((n_rows, w), F32),
        compiler_params=_cparams(("arbitrary",)),
        name="experts",
    )(blk_exp, n_used, xb, w_up_bf, b_up.reshape(N_EXPERTS, 1, 2 * D_FF), w_down_bf, b_down.reshape(N_EXPERTS, 1, D_MODEL))


def _combine_kernel(dcur_ref, dnext_ref, x1_ref, tw_ref, g2_ref, lg_ref, lb_ref, yb_hbm, o_ref, buf, sems, *, bm):
    i = pl.program_id(0) * pl.num_programs(1) + pl.program_id(1)
    n = pl.num_programs(0) * pl.num_programs(1)
    slot = i % 2

    def issue(dref, s):
        def body(t, c):
            for kk in range(TOP_K):
                pltpu.make_async_copy(yb_hbm.at[dref[kk, t]], buf.at[s, kk, t], sems.at[s]).start()
            return c
        lax.fori_loop(0, bm, body, 0)

    @pl.when(i == 0)
    def _():
        issue(dcur_ref, 0)

    @pl.when(i + 1 < n)
    def _():
        issue(dnext_ref, 1 - slot)

    for kk in range(TOP_K):
        pltpu.make_async_copy(yb_hbm.at[pl.ds(0, bm)], buf.at[slot, kk], sems.at[slot]).wait()

    tw = tw_ref[...]
    ff = tw[:, 0:1] * buf[slot, 0]
    for kk in range(1, TOP_K):
        ff = ff + tw[:, kk:kk + 1] * buf[slot, kk]
    o_ref[...] = _layer_norm(DEEPNORM_ALPHA * x1_ref[...] + (1.0 + g2_ref[...]) * ff, lg_ref[...], lb_ref[...])


def _combine(dest, tw_rows, x1, g2, ln_g, ln_b, yb, bm):
    G, R, D = x1.shape
    nb = R // bm
    n_blk = G * nb
    row = lambda w: pl.BlockSpec((None, bm, w), lambda g, i: (g, i, 0))
    full = lambda a: pl.BlockSpec(a.shape, lambda g, i: (0,) * a.ndim)
    cur = pl.BlockSpec((TOP_K, bm), lambda g, i: (0, g * nb + i), memory_space=pltpu.SMEM)
    nxt = pl.BlockSpec((TOP_K, bm), lambda g, i: (0, jnp.minimum(g * nb + i + 1, n_blk - 1)),
                       memory_space=pltpu.SMEM)
    return pl.pallas_call(
        functools.partial(_combine_kernel, bm=bm),
        grid=(G, nb),
        in_specs=[cur, nxt, row(D), pl.BlockSpec((bm, TOP_K), lambda g, i: (g * nb + i, 0)),
                  _mod_spec(g2, bm), full(ln_g), full(ln_b), pl.BlockSpec(memory_space=pl.ANY)],
        out_specs=row(D),
        out_shape=jax.ShapeDtypeStruct((G, R, D), F32),
        scratch_shapes=[pltpu.VMEM((2, TOP_K, bm, D), F32), pltpu.SemaphoreType.DMA((2,))],
        compiler_params=_cparams(("arbitrary", "arbitrary")),
        name="combine",
    )(dest, dest, x1, tw_rows, g2, ln_g, ln_b, yb)


def kernel(x_prompt, x_sample, c_prompt, c_sample, cache_k, cache_v, state_hgrn, page_table, ln_in_g, ln_in_b, w_ada, b_ada, w_in, hg_lb, hg_norm_g, da_lq1, da_lk1, da_lq2, da_lk2, da_subln_g, w_out, ln1_g, ln1_b, w_router, b_router, w_up, b_up, w_down, b_down, ln2_g, ln2_b):
    assert w_in.shape[0] == 1, "single-layer trunk"
    B, T, D = x_prompt.shape
    NS = x_sample.shape[0]
    n_prompt = B * T
    n_tot = n_prompt + NS
    r1 = lambda a: a.reshape(1, -1)

    lb = r1(jax.nn.softmax(hg_lb.astype(F32), axis=0)[0])
    lam = (jnp.exp(jnp.sum(da_lq1[0].astype(F32) * da_lk1[0].astype(F32)))
           - jnp.exp(jnp.sum(da_lq2[0].astype(F32) * da_lk2[0].astype(F32))) + LAM_INIT)
    lam_row = jnp.full((1, HEAD_W), lam, F32)
    ln_g, ln_b = r1(ln_in_g), r1(ln_in_b)
    w_in_bf = w_in[0].astype(BF16)
    wqt = w_in_bf[:, HG_COLS:HG_COLS + GROUP_W].T
    wvt = w_in_bf[:, HG_COLS + 2 * GROUP_W:].T
    w_out_bf = w_out[0].astype(BF16)
    w_up_bf = w_up[0].astype(BF16)
    w_down_bf = w_down[0].astype(BF16)
    wr_t = w_router[0].T
    wrh = wr_t.astype(BF16)
    wrl = (wr_t - wrh.astype(F32)).astype(BF16)
    br = b_router[0].reshape(N_EXPERTS, 1)
    norm_g, subln_g = r1(hg_norm_g[0]), r1(da_subln_g[0])

    n_c = B + NS
    c_all = jnp.concatenate([c_prompt, c_sample, jnp.zeros((-n_c % 8, D), F32)], axis=0)
    mod = _ada(c_all, w_ada[0], b_ada[0])
    mod_p = [mod[:B, j * D:(j + 1) * D].reshape(B, 1, D) for j in range(6)]
    mod_s = [mod[B:n_c, j * D:(j + 1) * D].reshape(1, NS, D) for j in range(6)]

    zhg_p, _, kf_p, vf_p, kb_p, qt_p, vt_p = _inproj(x_prompt, mod_p[1], mod_p[0], ln_g, ln_b, w_in_bf, wqt, wvt,
                                                     ATTN_BLOCK)
    hg_p, s_p = _hgrn_prompt(zhg_p, lb, norm_g)
    da_p = _attn_prompt(qt_p, kb_p, vt_p, lam, subln_g)

    xs = x_sample.reshape(1, NS, D)
    zhg_s, q_s, kf_s, vf_s, _, _, _ = _inproj(xs, mod_s[1], mod_s[0], ln_g, ln_b, w_in_bf, wqt, wvt, NS)
    hg_s, s_s = _hgrn_sample(zhg_s[0], state_hgrn[0], lb, norm_g)
    da_s = _attn_sample(q_s[0], kf_s.reshape(NS, GROUP_W), vf_s.reshape(NS, GROUP_W), cache_k, cache_v, page_table,
                        lam_row, subln_g)

    consts = (ln_g, ln_b, w_out_bf, r1(ln1_g[0]), r1(ln1_b[0]), wrh, wrl, br)
    counts0 = jnp.zeros((N_EXPERTS, 1), F32)
    x1_p, h2_p, idx_p, tw_p, rank_p, counts1 = _outproj(
        hg_p, da_p, x_prompt, (mod_p[2], mod_p[4], mod_p[3]), consts, counts0, ROW_BLOCK)
    x1_s, h2_s, idx_s, tw_s, rank_s, counts = _outproj(
        hg_s.reshape(1, NS, GROUP_W), da_s.reshape(1, NS, GROUP_W), xs, (mod_s[2], mod_s[4], mod_s[3]), consts,
        counts1, NS)

    cnt = counts[:, 0].astype(I32)
    padded = (cnt + MOE_BLOCK - 1) // MOE_BLOCK * MOE_BLOCK
    pad_end = jnp.cumsum(padded)
    pad_start = pad_end - padded
    e_ids = jnp.arange(N_EXPERTS, dtype=I32)

    def slot_of(idx, rank):
        return jnp.sum(jnp.where(idx[..., None] == e_ids, pad_start, 0), axis=-1) + rank

    dest_p = slot_of(idx_p, rank_p)
    dest_s = slot_of(idx_s, rank_s)
    n_blocks = -(-n_tot * TOP_K // MOE_BLOCK) + N_EXPERTS
    blk_row0 = jnp.arange(n_blocks, dtype=I32) * MOE_BLOCK
    blk_exp = jnp.minimum(jnp.sum((pad_end[None, :] <= blk_row0[:, None]).astype(I32), axis=1), N_EXPERTS - 1)
    n_used = (pad_end[-1:] // MOE_BLOCK).astype(I32)

    xb = jnp.zeros((n_blocks * MOE_BLOCK, D), F32)
    xb = _dispatch(dest_p, h2_p, xb)
    xb = _dispatch(dest_s, h2_s, xb)
    yb = _experts(blk_exp, n_used, xb, w_up_bf, b_up[0], w_down_bf, b_down[0])

    l2g, l2b = r1(ln2_g[0]), r1(ln2_b[0])
    y_p = _combine(dest_p, tw_p.T, x1_p, mod_p[5], l2g, l2b, yb, TOK_BLOCK)
    y_s = _combine(dest_s, tw_s.T, x1_s, mod_s[5], l2g, l2b, yb, TOK_BLOCK)

    smp = lambda a: a.reshape(1, NS, 1, HEADS, HEAD_W)
    return (y_p, y_s.reshape(NS, 1, D), kf_p[None], vf_p[None], s_p[None], smp(kf_s), smp(vf_s), s_s[None])
```

```python
import functools
import math

import numpy as np
import jax
import jax.numpy as jnp
from jax import lax
from jax.experimental import pallas as pl
from jax.experimental.pallas import tpu as pltpu

F32, BF16, I32, U32 = jnp.float32, jnp.bfloat16, jnp.int32, jnp.uint32

D_MODEL = 1024
HEADS = 4
HEAD_W = 128
DA_HEAD = 64
GROUP_W = HEADS * HEAD_W
HG_COLS = 4 * GROUP_W
IN_WIDTH = HG_COLS + 3 * GROUP_W
N_EXPERTS = 32
TOP_K = 4
D_FF = 1024
SWIGLU_ALPHA = 1.702
SWIGLU_LIMIT = 7.0
DEEPNORM_ALPHA = 2.0 ** 0.25
LN_EPS = 1e-5
RMS_EPS = 1e-6
LAM_INIT = 0.8 - 0.6 * math.exp(-0.3 * 0)
LOG2E = math.log2(math.e)
ALIBI_SLOPES = tuple(2.0 ** (-8.0 * (h + 1) / HEADS) for h in range(HEADS))
PAGE_SIZE = 128
EXP_CLAMP = 80.0

VMEM_LIMIT = 56 * 1024 * 1024
MOE_BLOCK = 256
HG_CHUNK = 128
ATTN_BLOCK = 1024
INPROJ_BLOCK = 512
ROW_BLOCK = 256
TOK_BLOCK = 128


def _cparams(sem):
    return pltpu.CompilerParams(dimension_semantics=sem, vmem_limit_bytes=VMEM_LIMIT)


def _sigmoid(x):
    return 1.0 / (1.0 + jnp.exp(-x))


def _layer_norm(x, g, b):
    mu = jnp.mean(x, -1, keepdims=True)
    xc = x - mu
    var = jnp.mean(xc * xc, -1, keepdims=True)
    return xc * lax.rsqrt(var + LN_EPS) * g + b


def _dot(a, b):
    return jnp.dot(a, b, preferred_element_type=F32)


def _dot_nt(a, b):
    return lax.dot_general(a, b, (((1,), (1,)), ((), ())), preferred_element_type=F32)


def _dot_tn(a, b):
    return lax.dot_general(a, b, (((0,), (0,)), ((), ())), preferred_element_type=F32)


def _ada_kernel(c_ref, w_ref, b_ref, o_ref):
    c = c_ref[...]
    a = (c * _sigmoid(c)).astype(BF16)
    o_ref[...] = _dot(a, w_ref[...].astype(BF16)) + b_ref[...]


def _ada(c, w_ada, b_ada):
    rows, d = c.shape
    n = w_ada.shape[1]
    bn = 1536
    return pl.pallas_call(
        _ada_kernel,
        grid=(n // bn,),
        in_specs=[pl.BlockSpec((rows, d), lambda j: (0, 0)),
                  pl.BlockSpec((d, bn), lambda j: (0, j)),
                  pl.BlockSpec((1, bn), lambda j: (0, j))],
        out_specs=pl.BlockSpec((rows, bn), lambda j: (0, j)),
        out_shape=jax.ShapeDtypeStruct((rows, n), F32),
        compiler_params=_cparams(("arbitrary",)),
        name="ada",
    )(c, w_ada, b_ada.reshape(1, n))


def _inproj_kernel(x_ref, g_ref, b_ref, sc_ref, sh_ref, w_ref, wqt_ref, wvt_ref,
                   zhg_ref, q_ref, kf_ref, vf_ref, kb_ref, qt_ref, vt_ref):
    x0 = _layer_norm(x_ref[...], g_ref[...], b_ref[...])
    h = (x0 * (1.0 + sc_ref[...]) + sh_ref[...]).astype(BF16)
    zhg_ref[...] = _dot(h, w_ref[:, 0:HG_COLS])
    c0 = HG_COLS
    q_ref[...] = (_dot(h, w_ref[:, c0:c0 + GROUP_W]) * (DA_HEAD ** -0.5)).astype(BF16)
    k = _dot(h, w_ref[:, c0 + GROUP_W:c0 + 2 * GROUP_W])
    kb_ref[...] = k.astype(BF16)
    v = _dot(h, w_ref[:, c0 + 2 * GROUP_W:c0 + 3 * GROUP_W])
    for hd in range(HEADS):
        kf_ref[:, hd, :] = k[:, hd * HEAD_W:(hd + 1) * HEAD_W]
        vf_ref[:, hd, :] = v[:, hd * HEAD_W:(hd + 1) * HEAD_W]
    qt_ref[...] = (_dot_nt(wqt_ref[...], h) * (DA_HEAD ** -0.5 * LOG2E)).astype(BF16)
    vt_ref[...] = _dot_nt(wvt_ref[...], h).astype(BF16)


def _mod_spec(mod, bm):
    if mod.shape[1] == 1:
        return pl.BlockSpec((None, 1, mod.shape[2]), lambda g, i: (g, 0, 0))
    return pl.BlockSpec((None, bm, mod.shape[2]), lambda g, i: (g, i, 0))


def _inproj(x, sc, sh, ln_g, ln_b, w_in_bf, wqt, wvt, bm, tblk):
    G, R, D = x.shape
    nb = R // bm
    per = tblk // bm
    row = lambda w: pl.BlockSpec((None, bm, w), lambda g, i: (g, i, 0))
    full = lambda a: pl.BlockSpec(a.shape, lambda g, i: (0,) * a.ndim)
    heads = pl.BlockSpec((None, bm, HEADS, HEAD_W), lambda g, i: (g, i, 0, 0))
    heads_shape = jax.ShapeDtypeStruct((G, R, HEADS, HEAD_W), F32)
    tr = pl.BlockSpec((None, None, GROUP_W, bm), lambda g, i: (g, i // per, 0, i % per))
    tr_shape = jax.ShapeDtypeStruct((G, R // tblk, GROUP_W, tblk), BF16)
    sds = lambda w, dt: jax.ShapeDtypeStruct((G, R, w), dt)
    return pl.pallas_call(
        _inproj_kernel,
        grid=(G, nb),
        in_specs=[row(D), full(ln_g), full(ln_b), _mod_spec(sc, bm), _mod_spec(sh, bm), full(w_in_bf), full(wqt), full(wvt)],
        out_specs=[row(HG_COLS), row(GROUP_W), heads, heads, row(GROUP_W), tr, tr],
        out_shape=[sds(HG_COLS, F32), sds(GROUP_W, BF16), heads_shape, heads_shape, sds(GROUP_W, BF16), tr_shape, tr_shape],
        compiler_params=_cparams(("arbitrary", "arbitrary")),
        name="inproj",
    )(x, ln_g, ln_b, sc, sh, w_in_bf, wqt, wvt)


def _hgrn_gates(zq, zf, lb):
    q = zq * _sigmoid(zq)
    f = lb + (1.0 - lb) * _sigmoid(zf)
    k = (1.0 - lb) * _sigmoid(-zf)
    return q, jnp.log(f), k


def _hgrn_kernel(z_ref, lb_ref, ng_ref, lvl_ref, tri_ref, o_ref, sfin_ref, st_ref, *, C):
    t = pl.program_id(1)

    @pl.when(t == 0)
    def _():
        st_ref[...] = jnp.zeros_like(st_ref)

    lvl = lvl_ref[...]
    tri = tri_ref[...]
    n_levels = int(math.log2(C)) - 3
    ng = ng_ref[...]
    for h in range(HEADS):
        cs = slice(h * HEAD_W, (h + 1) * HEAD_W)
        zq = z_ref[:, h * HEAD_W:(h + 1) * HEAD_W]
        zf = z_ref[:, GROUP_W + h * HEAD_W:GROUP_W + (h + 1) * HEAD_W]
        v = z_ref[:, 2 * GROUP_W + h * HEAD_W:2 * GROUP_W + (h + 1) * HEAD_W].astype(BF16)
        zg = z_ref[:, 3 * GROUP_W + h * HEAD_W:3 * GROUP_W + (h + 1) * HEAD_W]
        q, g, k = _hgrn_gates(zq, zf, lb_ref[:, cs])
        g1 = g.astype(BF16)
        r1 = g - g1.astype(F32)
        g2 = r1.astype(BF16)
        g3 = (r1 - g2.astype(F32)).astype(BF16)
        b = _dot(tri, g1) + _dot(tri, g2) + _dot(tri, g3)

        b8 = b.reshape(C // 8, 8, HEAD_W)
        bmid = jnp.broadcast_to(b8[:, 3:4, :], b8.shape).reshape(C, HEAD_W)
        e = jnp.clip(b - bmid, -EXP_CLAMP, EXP_CLAMP)
        a = jnp.where(lvl == 0, _dot_nt((q * jnp.exp(e)).astype(BF16), (k * jnp.exp(-e)).astype(BF16)), 0.0)
        for li in range(1, n_levels + 1):
            m = 4 << li
            bb = b.reshape(C // (2 * m), 2 * m, HEAD_W)
            d = b - jnp.broadcast_to(bb[:, m - 1:m, :], bb.shape).reshape(C, HEAD_W)
            qs = (q * jnp.exp(jnp.minimum(d, 0.0))).astype(BF16)
            ks = (k * jnp.exp(jnp.minimum(-d, 0.0))).astype(BF16)
            a = jnp.where(lvl == li, _dot_nt(qs, ks), a)

        st = st_ref[h]
        o = _dot(a.astype(BF16), v) + _dot_nt((q * jnp.exp(b)).astype(BF16), st.astype(BF16))
        b_last = b[C - 1:C, :]
        kd = (k * jnp.exp(b_last - b)).astype(BF16)
        st_ref[h] = st * jnp.exp(b_last) + _dot_tn(v, kd)

        ms = jnp.mean(o * o, -1, keepdims=True)
        o_ref[:, cs] = (o * lax.rsqrt(ms + RMS_EPS) * ng * (zg * _sigmoid(zg))).astype(BF16)

    @pl.when(t == pl.num_programs(1) - 1)
    def _():
        for h in range(HEADS):
            sfin_ref[h] = st_ref[h].T


def _hgrn_level_table(C):
    t = np.arange(C)[:, None]
    s = np.arange(C)[None, :]
    x = t ^ s
    lvl = np.zeros((C, C), np.int32)
    m = 8
    while m < C:
        lvl += (x >= m).astype(np.int32)
        m *= 2
    return np.where(s <= t, lvl, -1).astype(np.int32)


def _hgrn_prompt(zhg, lb, norm_g):
    B, T, _ = zhg.shape
    C = HG_CHUNK
    lvl = jnp.asarray(_hgrn_level_table(C))
    tri = jnp.asarray(np.tril(np.ones((C, C), np.float32)), BF16)
    full = lambda a: pl.BlockSpec(a.shape, lambda b, t: (0,) * a.ndim)
    return pl.pallas_call(
        functools.partial(_hgrn_kernel, C=C),
        grid=(B, T // C),
        in_specs=[pl.BlockSpec((None, C, HG_COLS), lambda b, t: (b, t, 0)), full(lb), full(norm_g), full(lvl), full(tri)],
        out_specs=[pl.BlockSpec((None, C, GROUP_W), lambda b, t: (b, t, 0)),
                   pl.BlockSpec((None, HEADS, HEAD_W, HEAD_W), lambda b, t: (b, 0, 0, 0))],
        out_shape=[jax.ShapeDtypeStruct((B, T, GROUP_W), BF16),
                   jax.ShapeDtypeStruct((B, HEADS, HEAD_W, HEAD_W), F32)],
        scratch_shapes=[pltpu.VMEM((HEADS, HEAD_W, HEAD_W), F32)],
        compiler_params=_cparams(("arbitrary", "arbitrary")),
        name="hgrn_prompt",
    )(zhg, lb, norm_g, lvl, tri)


def _hgrn_step_kernel(z_ref, s_ref, lb_ref, ng_ref, o_ref, so_ref, *, G):
    ng = ng_ref[...]
    for h in range(HEADS):
        cs = slice(h * HEAD_W, (h + 1) * HEAD_W)
        zq = z_ref[:, h * HEAD_W:(h + 1) * HEAD_W]
        zf = z_ref[:, GROUP_W + h * HEAD_W:GROUP_W + (h + 1) * HEAD_W]
        v = z_ref[:, 2 * GROUP_W + h * HEAD_W:2 * GROUP_W + (h + 1) * HEAD_W]
        zg = z_ref[:, 3 * GROUP_W + h * HEAD_W:3 * GROUP_W + (h + 1) * HEAD_W]
        lb = lb_ref[:, cs]
        q = zq * _sigmoid(zq)
        f = lb + (1.0 - lb) * _sigmoid(zf)
        k = (1.0 - lb) * _sigmoid(-zf)
        qT, fT, kT = q.T, f.T, k.T
        rows = []
        for j in range(G):
            s_new = fT[:, j:j + 1] * s_ref[j, h] + kT[:, j:j + 1] * v[j:j + 1, :]
            so_ref[j, h] = s_new
            rows.append(jnp.sum(s_new * qT[:, j:j + 1], axis=0, keepdims=True))
        o = jnp.concatenate(rows, axis=0)
        ms = jnp.mean(o * o, -1, keepdims=True)
        o_ref[:, cs] = (o * lax.rsqrt(ms + RMS_EPS) * ng * (zg * _sigmoid(zg))).astype(BF16)


def _hgrn_sample(zhg, state, lb, norm_g):
    N = zhg.shape[0]
    G = 8
    full = lambda a: pl.BlockSpec(a.shape, lambda i: (0,) * a.ndim)
    st_spec = pl.BlockSpec((G, HEADS, HEAD_W, HEAD_W), lambda i: (i, 0, 0, 0))
    return pl.pallas_call(
        functools.partial(_hgrn_step_kernel, G=G),
        grid=(N // G,),
        in_specs=[pl.BlockSpec((G, HG_COLS), lambda i: (i, 0)), st_spec, full(lb), full(norm_g)],
        out_specs=[pl.BlockSpec((G, GROUP_W), lambda i: (i, 0)), st_spec],
        out_shape=[jax.ShapeDtypeStruct((N, GROUP_W), BF16), jax.ShapeDtypeStruct(state.shape, F32)],
        compiler_params=_cparams(("arbitrary",)),
        name="hgrn_sample",
    )(zhg, state, lb, norm_g)


def _attn_kernel(qt_ref, k_ref, vt_ref, sl_ref, feat_ref, lam_ref, gcol_ref, o_ref, m_ref, l_ref, acc_ref, *, blk):
    qi = pl.program_id(2)
    row = lax.broadcasted_iota(I32, (HEAD_W, blk), 0)
    qt = qt_ref[...].astype(F32)
    a_hi = sl_ref[0:1, :]
    a_lo = sl_ref[1:2, :]
    slope2 = sl_ref[2:3, :]
    def slope_rows(r0):
        in_rows = (row >= r0) & (row < r0 + 4)
        return jnp.where(in_rows, jnp.where((row - r0) % 2 == 0, a_hi, a_lo), 0.0)

    q_aug = [jnp.where(row < DA_HEAD, qt, slope_rows(DA_HEAD)).astype(BF16),
             jnp.where(row >= DA_HEAD, qt, slope_rows(0)).astype(BF16)]
    keep = [feat_ref[0] > 0, feat_ref[1] > 0]
    feats = [feat_ref[2], feat_ref[3]]
    m_ref[...] = jnp.full_like(m_ref, -jnp.inf)
    l_ref[...] = jnp.zeros_like(l_ref)
    acc_ref[...] = jnp.zeros_like(acc_ref)

    def block(kj, masked):
        k_start = pl.multiple_of(kj * blk, blk)
        kb = k_ref[pl.ds(k_start, blk), :]
        vt = vt_ref[kj]
        off = slope2 * jnp.full((1, blk), k_start - qi * blk, I32).astype(F32)
        if masked:
            ok = lax.broadcasted_iota(I32, (blk, 1), 0) <= lax.broadcasted_iota(I32, (1, blk), 1)
        for mi in range(2):
            st = _dot(jnp.where(keep[mi], kb, feats[mi]), q_aug[mi])
            if masked:
                st = jnp.where(ok, st, -jnp.inf)
            m_prev = m_ref[mi]
            m_new = jnp.maximum(m_prev, jnp.max(st, axis=0, keepdims=True) + off)
            p = jnp.exp2(st - (m_new - off))
            alpha = jnp.exp2(m_prev - m_new)
            l_ref[mi] = alpha * l_ref[mi] + jnp.sum(p, axis=0, keepdims=True)
            acc_ref[mi] = alpha * acc_ref[mi] + _dot(vt, p.astype(BF16))
            m_ref[mi] = m_new

    block(qi, True)

    def body(kj, c):
        block(kj, False)
        return c

    lax.fori_loop(0, qi, body, 0)

    ot = acc_ref[0] * (1.0 / l_ref[0]) - lam_ref[...] * (acc_ref[1] * (1.0 / l_ref[1]))
    ms = jnp.mean(ot * ot, axis=0, keepdims=True)
    o_ref[...] = (ot * lax.rsqrt(ms + RMS_EPS) * gcol_ref[...]).T.astype(BF16)


def _attn_prompt(qt, k, vt, lam, subln_g):
    B, nblk, _, blk = qt.shape
    T = k.shape[1]
    bf = lambda x: np.asarray(x, np.float32).astype(BF16).astype(np.float64)
    a = np.asarray(ALIBI_SLOPES, np.float64) * LOG2E
    a_hi = bf(a)
    a_lo = bf(a - a_hi)
    sl = np.zeros((HEADS, 8, blk), np.float32)
    sl[:, 0, :], sl[:, 1, :], sl[:, 2, :] = a_hi[:, None], a_lo[:, None], a.astype(np.float32)[:, None]
    lane = np.arange(HEAD_W)[None, :]
    r = np.arange(blk)[:, None]
    assert blk <= 256 * 256

    def pos_feat(l0):
        return (np.where((lane == l0) | (lane == l0 + 1), r // 256 * 256, 0)
                + np.where((lane == l0 + 2) | (lane == l0 + 3), r % 256, 0)).astype(np.float32)

    feat = np.stack([np.broadcast_to(lane < DA_HEAD, (blk, HEAD_W)).astype(np.float32),
                     np.broadcast_to(lane >= DA_HEAD, (blk, HEAD_W)).astype(np.float32),
                     pos_feat(DA_HEAD), pos_feat(0)])
    feat = jnp.asarray(feat, BF16)
    lam_row = jnp.full((1, blk), lam, F32)
    gcol = jnp.broadcast_to((subln_g.reshape(HEAD_W, 1) * (1.0 - LAM_INIT)), (HEAD_W, blk))
    full = lambda x: pl.BlockSpec(x.shape, lambda b, h, i: (0,) * x.ndim)
    return pl.pallas_call(
        functools.partial(_attn_kernel, blk=blk),
        grid=(B, HEADS, nblk),
        in_specs=[pl.BlockSpec((None, None, HEAD_W, blk), lambda b, h, i: (b, i, h, 0)),
                  pl.BlockSpec((None, T, HEAD_W), lambda b, h, i: (b, 0, h)),
                  pl.BlockSpec((None, nblk, HEAD_W, blk), lambda b, h, i: (b, 0, h, 0)),
                  pl.BlockSpec((None, 8, blk), lambda b, h, i: (h, 0, 0)),
                  full(feat), full(lam_row), full(gcol)],
        out_specs=pl.BlockSpec((None, blk, HEAD_W), lambda b, h, i: (b, i, h)),
        out_shape=jax.ShapeDtypeStruct((B, T, GROUP_W), BF16),
        scratch_shapes=[pltpu.VMEM((2, 1, blk), F32), pltpu.VMEM((2, 1, blk), F32), pltpu.VMEM((2, HEAD_W, blk), F32)],
        compiler_params=_cparams(("arbitrary", "arbitrary", "arbitrary")),
        name="attn_prompt",
    )(qt, k, vt, jnp.asarray(sl), feat, lam_row, gcol)


def _attn_decode_kernel(pt_ref, q_ref, kn_ref, vn_ref, bias_ref, lam_ref, g_ref, ck_hbm, cv_hbm, o_ref,
                        kbuf, vbuf, sems, *, n_pages):
    i = pl.program_id(0)
    slot = i % 2

    def page_copies(seq, s):
        for pg in range(n_pages):
            page = pt_ref[seq, pg]
            for h in range(HEADS):
                dst = (s, h, pl.ds(pg * PAGE_SIZE, PAGE_SIZE))
                yield pltpu.make_async_copy(ck_hbm.at[0, page, :, h, :], kbuf.at[dst], sems.at[0, s])
                yield pltpu.make_async_copy(cv_hbm.at[0, page, :, h, :], vbuf.at[dst], sems.at[1, s])

    @pl.when(i == 0)
    def _():
        for cp in page_copies(0, 0):
            cp.start()

    @pl.when(i + 1 < pl.num_programs(0))
    def _():
        for cp in page_copies(i + 1, 1 - slot):
            cp.start()

    for cp in page_copies(i, slot):
        cp.wait()

    r8 = lax.broadcasted_iota(I32, (8, GROUP_W), 0)
    c8 = lax.broadcasted_iota(I32, (8, GROUP_W), 1)
    qmat = jnp.where(c8 // DA_HEAD == r8, jnp.broadcast_to(q_ref[...].astype(F32), (8, GROUP_W)), 0.0)
    qmat_bf = qmat.astype(BF16)
    s = bias_ref[...]
    for h in range(HEADS):
        s = s + _dot_nt(qmat_bf[:, h * HEAD_W:(h + 1) * HEAD_W], kbuf[slot, h].astype(BF16))
    kn = kn_ref[...].astype(BF16).astype(F32)
    s_self = jnp.sum(qmat_bf.astype(F32) * kn, axis=-1, keepdims=True)
    m = jnp.maximum(jnp.max(s, -1, keepdims=True), s_self)
    p = jnp.exp(s - m)
    p_self = jnp.exp(s_self - m)
    inv_l = 1.0 / (jnp.sum(p, -1, keepdims=True) + p_self)
    coef = jnp.where(lax.broadcasted_iota(I32, (8, 1), 0) % 2 == 0, 1.0, -lam_ref[:, 0:1]) * inv_l
    w = (p * coef).astype(BF16)
    accs = [_dot(w, vbuf[slot, h].astype(BF16)) for h in range(HEADS)]
    acc = jnp.concatenate(accs, axis=1) + (p_self * coef) * vn_ref[...]
    o = jnp.sum(jnp.where(c8 // HEAD_W == r8 // 2, acc, 0.0), axis=0, keepdims=True)
    outs = []
    for h in range(HEADS):
        oh = o[:, h * HEAD_W:(h + 1) * HEAD_W]
        ms = jnp.mean(oh * oh, -1, keepdims=True)
        outs.append(oh * lax.rsqrt(ms + RMS_EPS) * g_ref[...] * (1.0 - LAM_INIT))
    o_ref[...] = jnp.concatenate(outs, axis=1).astype(BF16)


def _attn_sample(q, k_new, v_new, cache_k, cache_v, page_table, lam_row, subln_g):
    N, n_pages = page_table.shape
    past = n_pages * PAGE_SIZE
    kpos = np.arange(past, dtype=np.float32)[None, :]
    slope_rows = np.repeat(np.asarray(ALIBI_SLOPES, np.float32), 2)[:, None]
    bias = jnp.asarray(-slope_rows * (past - kpos))
    row = pl.BlockSpec((None, 1, GROUP_W), lambda i, pt: (i, 0, 0))
    full = lambda a: pl.BlockSpec(a.shape, lambda i, pt: (0,) * a.ndim)
    anyspec = pl.BlockSpec(memory_space=pl.ANY)
    grid_spec = pltpu.PrefetchScalarGridSpec(
        num_scalar_prefetch=1,
        grid=(N,),
        in_specs=[row, row, row, full(bias), full(lam_row), full(subln_g), anyspec, anyspec],
        out_specs=row,
        scratch_shapes=[pltpu.VMEM((2, HEADS, past, HEAD_W), F32), pltpu.VMEM((2, HEADS, past, HEAD_W), F32),
                        pltpu.SemaphoreType.DMA((2, 2))],
    )
    r3 = lambda a: a.reshape(N, 1, GROUP_W)
    out = pl.pallas_call(
        functools.partial(_attn_decode_kernel, n_pages=n_pages),
        grid_spec=grid_spec,
        out_shape=jax.ShapeDtypeStruct((N, 1, GROUP_W), BF16),
        compiler_params=_cparams(("arbitrary",)),
        name="attn_sample",
    )(page_table, r3(q), r3(k_new), r3(v_new), bias, lam_row, subln_g, cache_k, cache_v)
    return out.reshape(N, GROUP_W)


def _outproj_kernel(hg_ref, da_ref, x_ref, lg_ref, lb_ref, g1_ref, sc2_ref, sh2_ref, w_ref, l1g_ref, l1b_ref,
                    wrh_ref, wrl_ref, br_ref, u_ref, cin_ref,
                    x1_ref, h2_ref, idx_ref, tw_ref, rank_ref, cout_ref, run_ref):
    @pl.when((pl.program_id(0) == 0) & (pl.program_id(1) == 0))
    def _():
        run_ref[...] = cin_ref[...]

    x0 = _layer_norm(x_ref[...], lg_ref[...], lb_ref[...])
    mix = _dot(hg_ref[...], w_ref[0:GROUP_W, :]) + _dot(da_ref[...], w_ref[GROUP_W:2 * GROUP_W, :])
    x1 = _layer_norm(DEEPNORM_ALPHA * x0 + (1.0 + g1_ref[...]) * mix, l1g_ref[...], l1b_ref[...])
    x1_ref[...] = x1
    h2 = x1 * (1.0 + sc2_ref[...]) + sh2_ref[...]
    h2_ref[...] = h2

    hi = h2.astype(BF16)
    lo = (h2 - hi.astype(F32)).astype(BF16)
    wrh = wrh_ref[...]
    logits = _dot_nt(wrh, hi) + _dot_nt(wrh, lo) + _dot_nt(wrl_ref[...], hi) + br_ref[...]

    n_e, bm = logits.shape
    rows = lax.broadcasted_iota(I32, (n_e, bm), 0).astype(F32)
    vals, sels = [], []
    work = logits
    for kk in range(TOP_K):
        mx = jnp.max(work, axis=0, keepdims=True)
        ix = jnp.min(jnp.where(work == mx, rows, float(n_e)), axis=0, keepdims=True)
        sel = rows == ix
        idx_ref[kk:kk + 1, :] = ix.astype(I32)
        vals.append(mx)
        sels.append(sel)
        work = jnp.where(sel, -jnp.inf, work)
    es = [jnp.exp(vv - vals[0]) for vv in vals]
    inv = 1.0 / (es[0] + es[1] + es[2] + es[3])
    for kk in range(TOP_K):
        tw_ref[kk:kk + 1, :] = es[kk] * inv

    base = run_ref[...]
    for kk in range(TOP_K):
        oh = jnp.where(sels[kk], 1.0, 0.0)
        before = base + _dot(oh.astype(BF16), u_ref[...])
        rank_ref[kk:kk + 1, :] = jnp.sum(jnp.where(sels[kk], before, 0.0), axis=0, keepdims=True).astype(I32)
        base = base + jnp.sum(oh, axis=1, keepdims=True)
    run_ref[...] = base
    cout_ref[...] = base


def _outproj(hg, da, x, mods, consts, counts_in, bm):
    G, R, D = x.shape
    g1, sc2, sh2 = mods
    ln_g, ln_b, w_out_bf, l1g, l1b, wrh, wrl, br = consts
    nb = R // bm
    n_tok = G * R
    u = jnp.asarray(np.triu(np.ones((bm, bm), np.float32), 1), BF16)
    row = lambda w: pl.BlockSpec((None, bm, w), lambda g, i: (g, i, 0))
    full = lambda a: pl.BlockSpec(a.shape, lambda g, i: (0,) * a.ndim)
    tok_lanes = pl.BlockSpec((TOP_K, bm), lambda g, i: (0, g * nb + i))
    return pl.pallas_call(
        _outproj_kernel,
        grid=(G, nb),
        in_specs=[row(GROUP_W), row(GROUP_W), row(D), full(ln_g), full(ln_b),
                  _mod_spec(g1, bm), _mod_spec(sc2, bm), _mod_spec(sh2, bm),
                  full(w_out_bf), full(l1g), full(l1b), full(wrh), full(wrl), full(br), full(u), full(counts_in)],
        out_specs=[row(D), pl.BlockSpec((bm, D), lambda g, i: (g * nb + i, 0)),
                   tok_lanes, tok_lanes, tok_lanes, full(counts_in)],
        out_shape=[jax.ShapeDtypeStruct((G, R, D), F32),
                   jax.ShapeDtypeStruct((n_tok, D), F32),
                   jax.ShapeDtypeStruct((TOP_K, n_tok), I32),
                   jax.ShapeDtypeStruct((TOP_K, n_tok), F32),
                   jax.ShapeDtypeStruct((TOP_K, n_tok), I32),
                   jax.ShapeDtypeStruct(counts_in.shape, F32)],
        scratch_shapes=[pltpu.VMEM(counts_in.shape, F32)],
        compiler_params=_cparams(("arbitrary", "arbitrary")),
        name="outproj",
    )(hg, da, x, ln_g, ln_b, g1, sc2, sh2, w_out_bf, l1g, l1b, wrh, wrl, br, u, counts_in)


def _dispatch_kernel(dest_ref, h2_ref, xb_in_hbm, xb_hbm, sem, *, bt):
    del xb_in_hbm

    def body(t, c):
        for kk in range(TOP_K):
            pltpu.make_async_copy(h2_ref.at[t], xb_hbm.at[dest_ref[kk, t]], sem).start()
        return c

    lax.fori_loop(0, bt, body, 0)
    for kk in range(TOP_K):
        pltpu.make_async_copy(h2_ref, xb_hbm.at[pl.ds(0, bt)], sem).wait()


def _dispatch(dest, h2, xb):
    n_tok = h2.shape[0]
    bt = TOK_BLOCK
    anyspec = pl.BlockSpec(memory_space=pl.ANY)
    return pl.pallas_call(
        functools.partial(_dispatch_kernel, bt=bt),
        grid=(n_tok // bt,),
        in_specs=[pl.BlockSpec((TOP_K, bt), lambda i: (0, i), memory_space=pltpu.SMEM),
                  pl.BlockSpec((bt, h2.shape[1]), lambda i: (i, 0)), anyspec],
        out_specs=anyspec,
        out_shape=jax.ShapeDtypeStruct(xb.shape, xb.dtype),
        scratch_shapes=[pltpu.SemaphoreType.DMA(())],
        input_output_aliases={2: 0},
        compiler_params=_cparams(("arbitrary",)),
        name="dispatch",
    )(dest, h2, xb)


def _experts_kernel(be_ref, nu_ref, x_ref, wu_ref, bu_ref, wd_ref, bd_ref, y_ref):
    del be_ref

    @pl.when(pl.program_id(0) < nu_ref[0])
    def _():
        u = _dot(x_ref[...].astype(BF16), wu_ref[...]) + bu_ref[...]
        glu = jnp.minimum(u[:, :D_FF], SWIGLU_LIMIT)
        lin = jnp.clip(u[:, D_FF:], -SWIGLU_LIMIT, SWIGLU_LIMIT)
        act = glu * _sigmoid(SWIGLU_ALPHA * glu) * (lin + 1.0)
        y_ref[...] = _dot(act.astype(BF16), wd_ref[...]) + bd_ref[...]

    @pl.when(pl.program_id(0) >= nu_ref[0])
    def _():
        y_ref[...] = jnp.zeros_like(y_ref)


def _experts(blk_exp, n_used, xb, w_up_bf, b_up, w_down_bf, b_down):
    n_rows, w = xb.shape
    bm = MOE_BLOCK
    nb = n_rows // bm
    grid_spec = pltpu.PrefetchScalarGridSpec(
        num_scalar_prefetch=2,
        grid=(nb,),
        in_specs=[pl.BlockSpec((bm, w), lambda j, be, nu: (jnp.minimum(j, nu[0] - 1), 0)),
                  pl.BlockSpec((None, D_MODEL, 2 * D_FF), lambda j, be, nu: (be[j], 0, 0)),
                  pl.BlockSpec((None, 1, 2 * D_FF), lambda j, be, nu: (be[j], 0, 0)),
                  pl.BlockSpec((None, D_FF, D_MODEL), lambda j, be, nu: (be[j], 0, 0)),
                  pl.BlockSpec((None, 1, D_MODEL), lambda j, be, nu: (be[j], 0, 0))],
        out_specs=pl.BlockSpec((bm, w), lambda j, be, nu: (j, 0)),
    )
    return pl.pallas_call(
        _experts_kernel,
        grid_spec=grid_spec,
        out_shape=jax.ShapeDtypeStruct((n_rows, w), F32),
        compiler_params=_cparams(("arbitrary",)),
        name="experts",
    )(blk_exp, n_used, xb, w_up_bf, b_up.reshape(N_EXPERTS, 1, 2 * D_FF), w_down_bf, b_down.reshape(N_EXPERTS, 1, D_MODEL))


def _combine_kernel(dcur_ref, dnext_ref, x1_ref, tw_ref, g2_ref, lg_ref, lb_ref, yb_hbm, o_ref, buf, sems, *, bm):
    i = pl.program_id(0) * pl.num_programs(1) + pl.program_id(1)
    n = pl.num_programs(0) * pl.num_programs(1)
    slot = i % 2

    def issue(dref, s):
        def body(t, c):
            for kk in range(TOP_K):
                pltpu.make_async_copy(yb_hbm.at[dref[kk, t]], buf.at[s, kk, t], sems.at[s]).start()
            return c
        lax.fori_loop(0, bm, body, 0)

    @pl.when(i == 0)
    def _():
        issue(dcur_ref, 0)

    @pl.when(i + 1 < n)
    def _():
        issue(dnext_ref, 1 - slot)

    for kk in range(TOP_K):
        pltpu.make_async_copy(yb_hbm.at[pl.ds(0, bm)], buf.at[slot, kk], sems.at[slot]).wait()

    tw = tw_ref[...]
    ff = tw[:, 0:1] * buf[slot, 0]
    for kk in range(1, TOP_K):
        ff = ff + tw[:, kk:kk + 1] * buf[slot, kk]
    o_ref[...] = _layer_norm(DEEPNORM_ALPHA * x1_ref[...] + (1.0 + g2_ref[...]) * ff, lg_ref[...], lb_ref[...])


def _combine(dest, tw_rows, x1, g2, ln_g, ln_b, yb, bm):
    G, R, D = x1.shape
    nb = R // bm
    n_blk = G * nb
    row = lambda w: pl.BlockSpec((None, bm, w), lambda g, i: (g, i, 0))
    full = lambda a: pl.BlockSpec(a.shape, lambda g, i: (0,) * a.ndim)
    cur = pl.BlockSpec((TOP_K, bm), lambda g, i: (0, g * nb + i), memory_space=pltpu.SMEM)
    nxt = pl.BlockSpec((TOP_K, bm), lambda g, i: (0, jnp.minimum(g * nb + i + 1, n_blk - 1)),
                       memory_space=pltpu.SMEM)
    return pl.pallas_call(
        functools.partial(_combine_kernel, bm=bm),
        grid=(G, nb),
        in_specs=[cur, nxt, row(D), pl.BlockSpec((bm, TOP_K), lambda g, i: (g * nb + i, 0)),
                  _mod_spec(g2, bm), full(ln_g), full(ln_b), pl.BlockSpec(memory_space=pl.ANY)],
        out_specs=row(D),
        out_shape=jax.ShapeDtypeStruct((G, R, D), F32),
        scratch_shapes=[pltpu.VMEM((2, TOP_K, bm, D), F32), pltpu.SemaphoreType.DMA((2,))],
        compiler_params=_cparams(("arbitrary", "arbitrary")),
        name="combine",
    )(dest, dest, x1, tw_rows, g2, ln_g, ln_b, yb)


def kernel(x_prompt, x_sample, c_prompt, c_sample, cache_k, cache_v, state_hgrn, page_table, ln_in_g, ln_in_b, w_ada, b_ada, w_in, hg_lb, hg_norm_g, da_lq1, da_lk1, da_lq2, da_lk2, da_subln_g, w_out, ln1_g, ln1_b, w_router, b_router, w_up, b_up, w_down, b_down, ln2_g, ln2_b):
    assert w_in.shape[0] == 1, "single-layer trunk"
    B, T, D = x_prompt.shape
    NS = x_sample.shape[0]
    n_prompt = B * T
    n_tot = n_prompt + NS
    r1 = lambda a: a.reshape(1, -1)

    lb = r1(jax.nn.softmax(hg_lb.astype(F32), axis=0)[0])
    lam = (jnp.exp(jnp.sum(da_lq1[0].astype(F32) * da_lk1[0].astype(F32)))
           - jnp.exp(jnp.sum(da_lq2[0].astype(F32) * da_lk2[0].astype(F32))) + LAM_INIT)
    lam_row = jnp.full((1, HEAD_W), lam, F32)
    ln_g, ln_b = r1(ln_in_g), r1(ln_in_b)
    w_in_bf = w_in[0].astype(BF16)
    wqt = w_in_bf[:, HG_COLS:HG_COLS + GROUP_W].T
    wvt = w_in_bf[:, HG_COLS + 2 * GROUP_W:].T
    w_out_bf = w_out[0].astype(BF16)
    w_up_bf = w_up[0].astype(BF16)
    w_down_bf = w_down[0].astype(BF16)
    wr_t = w_router[0].T
    wrh = wr_t.astype(BF16)
    wrl = (wr_t - wrh.astype(F32)).astype(BF16)
    br = b_router[0].reshape(N_EXPERTS, 1)
    norm_g, subln_g = r1(hg_norm_g[0]), r1(da_subln_g[0])

    n_c = B + NS
    c_all = jnp.concatenate([c_prompt, c_sample, jnp.zeros((-n_c % 8, D), F32)], axis=0)
    mod = _ada(c_all, w_ada[0], b_ada[0])
    mod_p = [mod[:B, j * D:(j + 1) * D].reshape(B, 1, D) for j in range(6)]
    mod_s = [mod[B:n_c, j * D:(j + 1) * D].reshape(1, NS, D) for j in range(6)]

    zhg_p, _, kf_p, vf_p, kb_p, qt_p, vt_p = _inproj(x_prompt, mod_p[1], mod_p[0], ln_g, ln_b, w_in_bf, wqt, wvt,
                                                     INPROJ_BLOCK, ATTN_BLOCK)
    hg_p, s_p = _hgrn_prompt(zhg_p, lb, norm_g)
    da_p = _attn_prompt(qt_p, kb_p, vt_p, lam, subln_g)

    xs = x_sample.reshape(1, NS, D)
    zhg_s, q_s, kf_s, vf_s, _, _, _ = _inproj(xs, mod_s[1], mod_s[0], ln_g, ln_b, w_in_bf, wqt, wvt, NS, NS)
    hg_s, s_s = _hgrn_sample(zhg_s[0], state_hgrn[0], lb, norm_g)
    da_s = _attn_sample(q_s[0], kf_s.reshape(NS, GROUP_W), vf_s.reshape(NS, GROUP_W), cache_k, cache_v, page_table,
                        lam_row, subln_g)

    consts = (ln_g, ln_b, w_out_bf, r1(ln1_g[0]), r1(ln1_b[0]), wrh, wrl, br)
    counts0 = jnp.zeros((N_EXPERTS, 1), F32)
    x1_p, h2_p, idx_p, tw_p, rank_p, counts1 = _outproj(
        hg_p, da_p, x_prompt, (mod_p[2], mod_p[4], mod_p[3]), consts, counts0, ROW_BLOCK)
    x1_s, h2_s, idx_s, tw_s, rank_s, counts = _outproj(
        hg_s.reshape(1, NS, GROUP_W), da_s.reshape(1, NS, GROUP_W), xs, (mod_s[2], mod_s[4], mod_s[3]), consts,
        counts1, NS)

    cnt = counts[:, 0].astype(I32)
    padded = (cnt + MOE_BLOCK - 1) // MOE_BLOCK * MOE_BLOCK
    pad_end = jnp.cumsum(padded)
    pad_start = pad_end - padded
    e_ids = jnp.arange(N_EXPERTS, dtype=I32)

    def slot_of(idx, rank):
        return jnp.sum(jnp.where(idx[..., None] == e_ids, pad_start, 0), axis=-1) + rank

    dest_p = slot_of(idx_p, rank_p)
    dest_s = slot_of(idx_s, rank_s)
    n_blocks = -(-n_tot * TOP_K // MOE_BLOCK) + N_EXPERTS
    blk_row0 = jnp.arange(n_blocks, dtype=I32) * MOE_BLOCK
    blk_exp = jnp.minimum(jnp.sum((pad_end[None, :] <= blk_row0[:, None]).astype(I32), axis=1), N_EXPERTS - 1)
    n_used = (pad_end[-1:] // MOE_BLOCK).astype(I32)

    xb = jnp.zeros((n_blocks * MOE_BLOCK, D), F32)
    xb = _dispatch(dest_p, h2_p, xb)
    xb = _dispatch(dest_s, h2_s, xb)
    yb = _experts(blk_exp, n_used, xb, w_up_bf, b_up[0], w_down_bf, b_down[0])

    l2g, l2b = r1(ln2_g[0]), r1(ln2_b[0])
    y_p = _combine(dest_p, tw_p.T, x1_p, mod_p[5], l2g, l2b, yb, TOK_BLOCK)
    y_s = _combine(dest_s, tw_s.T, x1_s, mod_s[5], l2g, l2b, yb, TOK_BLOCK)

    smp = lambda a: a.reshape(1, NS, 1, HEADS, HEAD_W)
    return (y_p, y_s.reshape(NS, 1, D), kf_p[None], vf_p[None], s_p[None], smp(kf_s), smp(vf_s), s_s[None])
```

```python
import functools
import math

import numpy as np
import jax
import jax.numpy as jnp
from jax import lax
from jax.experimental import pallas as pl
from jax.experimental.pallas import tpu as pltpu

F32, BF16, I32, U32 = jnp.float32, jnp.bfloat16, jnp.int32, jnp.uint32

D_MODEL = 1024
HEADS = 4
HEAD_W = 128
DA_HEAD = 64
GROUP_W = HEADS * HEAD_W
HG_COLS = 4 * GROUP_W
IN_WIDTH = HG_COLS + 3 * GROUP_W
N_EXPERTS = 32
TOP_K = 4
D_FF = 1024
SWIGLU_ALPHA = 1.702
SWIGLU_LIMIT = 7.0
DEEPNORM_ALPHA = 2.0 ** 0.25
LN_EPS = 1e-5
RMS_EPS = 1e-6
LAM_INIT = 0.8 - 0.6 * math.exp(-0.3 * 0)
LOG2E = math.log2(math.e)
ALIBI_SLOPES = tuple(2.0 ** (-8.0 * (h + 1) / HEADS) for h in range(HEADS))
PAGE_SIZE = 128
EXP_CLAMP = 80.0

VMEM_LIMIT = 56 * 1024 * 1024
MOE_BLOCK = 256
HG_CHUNK = 128
ATTN_BLOCK = 1024
INPROJ_BLOCK = 512
ROW_BLOCK = 512
TOK_BLOCK = 128


def _cparams(sem):
    return pltpu.CompilerParams(dimension_semantics=sem, vmem_limit_bytes=VMEM_LIMIT)


def _sigmoid(x):
    return 1.0 / (1.0 + jnp.exp(-x))


def _layer_norm(x, g, b):
    mu = jnp.mean(x, -1, keepdims=True)
    xc = x - mu
    var = jnp.mean(xc * xc, -1, keepdims=True)
    return xc * lax.rsqrt(var + LN_EPS) * g + b


def _dot(a, b):
    return jnp.dot(a, b, preferred_element_type=F32)


def _dot_nt(a, b):
    return lax.dot_general(a, b, (((1,), (1,)), ((), ())), preferred_element_type=F32)


def _dot_tn(a, b):
    return lax.dot_general(a, b, (((0,), (0,)), ((), ())), preferred_element_type=F32)


def _ada_kernel(c_ref, w_ref, b_ref, o_ref):
    c = c_ref[...]
    a = (c * _sigmoid(c)).astype(BF16)
    o_ref[...] = _dot(a, w_ref[...].astype(BF16)) + b_ref[...]


def _ada(c, w_ada, b_ada):
    rows, d = c.shape
    n = w_ada.shape[1]
    bn = 1536
    return pl.pallas_call(
        _ada_kernel,
        grid=(n // bn,),
        in_specs=[pl.BlockSpec((rows, d), lambda j: (0, 0)),
                  pl.BlockSpec((d, bn), lambda j: (0, j)),
                  pl.BlockSpec((1, bn), lambda j: (0, j))],
        out_specs=pl.BlockSpec((rows, bn), lambda j: (0, j)),
        out_shape=jax.ShapeDtypeStruct((rows, n), F32),
        compiler_params=_cparams(("arbitrary",)),
        name="ada",
    )(c, w_ada, b_ada.reshape(1, n))


def _inproj_common(x_ref, g_ref, b_ref, sc_ref, sh_ref, w_ref, zhg_ref, kf_ref, vf_ref):
    x0 = _layer_norm(x_ref[...], g_ref[...], b_ref[...])
    h = (x0 * (1.0 + sc_ref[...]) + sh_ref[...]).astype(BF16)
    zhg_ref[...] = _dot(h, w_ref[:, 0:HG_COLS])
    c0 = HG_COLS + GROUP_W
    k = _dot(h, w_ref[:, c0:c0 + GROUP_W])
    v = _dot(h, w_ref[:, c0 + GROUP_W:c0 + 2 * GROUP_W])
    for hd in range(HEADS):
        kf_ref[:, hd, :] = k[:, hd * HEAD_W:(hd + 1) * HEAD_W]
        vf_ref[:, hd, :] = v[:, hd * HEAD_W:(hd + 1) * HEAD_W]
    return h, k


def _inproj_prompt_kernel(x_ref, g_ref, b_ref, sc_ref, sh_ref, w_ref, wqt_ref, wvt_ref,
                          zhg_ref, kf_ref, vf_ref, kb_ref, qt_ref, vt_ref):
    h, k = _inproj_common(x_ref, g_ref, b_ref, sc_ref, sh_ref, w_ref, zhg_ref, kf_ref, vf_ref)
    kb_ref[...] = k.astype(BF16)
    qt_ref[...] = (_dot_nt(wqt_ref[...], h) * (DA_HEAD ** -0.5 * LOG2E)).astype(BF16)
    vt_ref[...] = _dot_nt(wvt_ref[...], h).astype(BF16)


def _inproj_sample_kernel(x_ref, g_ref, b_ref, sc_ref, sh_ref, w_ref, zhg_ref, kf_ref, vf_ref, q_ref):
    h, _ = _inproj_common(x_ref, g_ref, b_ref, sc_ref, sh_ref, w_ref, zhg_ref, kf_ref, vf_ref)
    q_ref[...] = (_dot(h, w_ref[:, HG_COLS:HG_COLS + GROUP_W]) * (DA_HEAD ** -0.5)).astype(BF16)


def _mod_spec(mod, bm):
    if mod.shape[1] == 1:
        return pl.BlockSpec((None, 1, mod.shape[2]), lambda g, i: (g, 0, 0))
    return pl.BlockSpec((None, bm, mod.shape[2]), lambda g, i: (g, i, 0))


def _inproj(x, sc, sh, ln_g, ln_b, w_in_bf, bm, transposed=None):
    G, R, D = x.shape
    nb = R // bm
    row = lambda w: pl.BlockSpec((None, bm, w), lambda g, i: (g, i, 0))
    full = lambda a: pl.BlockSpec(a.shape, lambda g, i: (0,) * a.ndim)
    heads = pl.BlockSpec((None, bm, HEADS, HEAD_W), lambda g, i: (g, i, 0, 0))
    heads_shape = jax.ShapeDtypeStruct((G, R, HEADS, HEAD_W), F32)
    sds = lambda w, dt: jax.ShapeDtypeStruct((G, R, w), dt)
    args = [x, ln_g, ln_b, sc, sh, w_in_bf]
    in_specs = [row(D), full(ln_g), full(ln_b), _mod_spec(sc, bm), _mod_spec(sh, bm), full(w_in_bf)]
    out_specs = [row(HG_COLS), heads, heads, row(GROUP_W)]
    out_shape = [sds(HG_COLS, F32), heads_shape, heads_shape, sds(GROUP_W, BF16)]
    body = _inproj_sample_kernel
    if transposed is not None:
        wqt, wvt, tblk = transposed
        per = tblk // bm
        tr = pl.BlockSpec((None, None, GROUP_W, bm), lambda g, i: (g, i // per, 0, i % per))
        tr_shape = jax.ShapeDtypeStruct((G, R // tblk, GROUP_W, tblk), BF16)
        args += [wqt, wvt]
        in_specs += [full(wqt), full(wvt)]
        out_specs += [tr, tr]
        out_shape += [tr_shape, tr_shape]
        body = _inproj_prompt_kernel
    return pl.pallas_call(
        body,
        grid=(G, nb),
        in_specs=in_specs,
        out_specs=out_specs,
        out_shape=out_shape,
        compiler_params=_cparams(("arbitrary", "arbitrary")),
        name="inproj",
    )(*args)


def _hgrn_gates(zq, zf, lb):
    q = zq * _sigmoid(zq)
    f = lb + (1.0 - lb) * _sigmoid(zf)
    k = (1.0 - lb) * _sigmoid(-zf)
    return q, jnp.log(f), k


def _hgrn_kernel(z_ref, lb_ref, ng_ref, lvl_ref, tri_ref, o_ref, sfin_ref, st_ref, *, C):
    t = pl.program_id(1)

    @pl.when(t == 0)
    def _():
        st_ref[...] = jnp.zeros_like(st_ref)

    lvl = lvl_ref[...]
    tri = tri_ref[...]
    n_levels = int(math.log2(C)) - 3
    ng = ng_ref[...]
    for h in range(HEADS):
        cs = slice(h * HEAD_W, (h + 1) * HEAD_W)
        zq = z_ref[:, h * HEAD_W:(h + 1) * HEAD_W]
        zf = z_ref[:, GROUP_W + h * HEAD_W:GROUP_W + (h + 1) * HEAD_W]
        v = z_ref[:, 2 * GROUP_W + h * HEAD_W:2 * GROUP_W + (h + 1) * HEAD_W].astype(BF16)
        zg = z_ref[:, 3 * GROUP_W + h * HEAD_W:3 * GROUP_W + (h + 1) * HEAD_W]
        q, g, k = _hgrn_gates(zq, zf, lb_ref[:, cs])
        g1 = g.astype(BF16)
        r1 = g - g1.astype(F32)
        g2 = r1.astype(BF16)
        g3 = (r1 - g2.astype(F32)).astype(BF16)
        b = _dot(tri, g1) + _dot(tri, g2) + _dot(tri, g3)

        b8 = b.reshape(C // 8, 8, HEAD_W)
        bmid = jnp.broadcast_to(b8[:, 3:4, :], b8.shape).reshape(C, HEAD_W)
        e = jnp.clip(b - bmid, -EXP_CLAMP, EXP_CLAMP)
        a = jnp.where(lvl == 0, _dot_nt((q * jnp.exp(e)).astype(BF16), (k * jnp.exp(-e)).astype(BF16)), 0.0)
        for li in range(1, n_levels + 1):
            m = 4 << li
            bb = b.reshape(C // (2 * m), 2 * m, HEAD_W)
            d = b - jnp.broadcast_to(bb[:, m - 1:m, :], bb.shape).reshape(C, HEAD_W)
            qs = (q * jnp.exp(jnp.minimum(d, 0.0))).astype(BF16)
            ks = (k * jnp.exp(jnp.minimum(-d, 0.0))).astype(BF16)
            a = jnp.where(lvl == li, _dot_nt(qs, ks), a)

        st = st_ref[h]
        o = _dot(a.astype(BF16), v) + _dot_nt((q * jnp.exp(b)).astype(BF16), st.astype(BF16))
        b_last = b[C - 1:C, :]
        kd = (k * jnp.exp(b_last - b)).astype(BF16)
        st_ref[h] = st * jnp.exp(b_last) + _dot_tn(v, kd)

        ms = jnp.mean(o * o, -1, keepdims=True)
        o_ref[:, cs] = (o * lax.rsqrt(ms + RMS_EPS) * ng * (zg * _sigmoid(zg))).astype(BF16)

    @pl.when(t == pl.num_programs(1) - 1)
    def _():
        for h in range(HEADS):
            sfin_ref[h] = st_ref[h].T


def _hgrn_level_table(C):
    t = np.arange(C)[:, None]
    s = np.arange(C)[None, :]
    x = t ^ s
    lvl = np.zeros((C, C), np.int32)
    m = 8
    while m < C:
        lvl += (x >= m).astype(np.int32)
        m *= 2
    return np.where(s <= t, lvl, -1).astype(np.int32)


def _hgrn_prompt(zhg, lb, norm_g):
    B, T, _ = zhg.shape
    C = HG_CHUNK
    lvl = jnp.asarray(_hgrn_level_table(C))
    tri = jnp.asarray(np.tril(np.ones((C, C), np.float32)), BF16)
    full = lambda a: pl.BlockSpec(a.shape, lambda b, t: (0,) * a.ndim)
    return pl.pallas_call(
        functools.partial(_hgrn_kernel, C=C),
        grid=(B, T // C),
        in_specs=[pl.BlockSpec((None, C, HG_COLS), lambda b, t: (b, t, 0)), full(lb), full(norm_g), full(lvl), full(tri)],
        out_specs=[pl.BlockSpec((None, C, GROUP_W), lambda b, t: (b, t, 0)),
                   pl.BlockSpec((None, HEADS, HEAD_W, HEAD_W), lambda b, t: (b, 0, 0, 0))],
        out_shape=[jax.ShapeDtypeStruct((B, T, GROUP_W), BF16),
                   jax.ShapeDtypeStruct((B, HEADS, HEAD_W, HEAD_W), F32)],
        scratch_shapes=[pltpu.VMEM((HEADS, HEAD_W, HEAD_W), F32)],
        compiler_params=_cparams(("arbitrary", "arbitrary")),
        name="hgrn_prompt",
    )(zhg, lb, norm_g, lvl, tri)


def _hgrn_step_kernel(z_ref, s_ref, lb_ref, ng_ref, o_ref, so_ref, *, G):
    ng = ng_ref[...]
    for h in range(HEADS):
        cs = slice(h * HEAD_W, (h + 1) * HEAD_W)
        zq = z_ref[:, h * HEAD_W:(h + 1) * HEAD_W]
        zf = z_ref[:, GROUP_W + h * HEAD_W:GROUP_W + (h + 1) * HEAD_W]
        v = z_ref[:, 2 * GROUP_W + h * HEAD_W:2 * GROUP_W + (h + 1) * HEAD_W]
        zg = z_ref[:, 3 * GROUP_W + h * HEAD_W:3 * GROUP_W + (h + 1) * HEAD_W]
        lb = lb_ref[:, cs]
        q = zq * _sigmoid(zq)
        f = lb + (1.0 - lb) * _sigmoid(zf)
        k = (1.0 - lb) * _sigmoid(-zf)
        qT, fT, kT = q.T, f.T, k.T
        rows = []
        for j in range(G):
            s_new = fT[:, j:j + 1] * s_ref[j, h] + kT[:, j:j + 1] * v[j:j + 1, :]
            so_ref[j, h] = s_new
            rows.append(jnp.sum(s_new * qT[:, j:j + 1], axis=0, keepdims=True))
        o = jnp.concatenate(rows, axis=0)
        ms = jnp.mean(o * o, -1, keepdims=True)
        o_ref[:, cs] = (o * lax.rsqrt(ms + RMS_EPS) * ng * (zg * _sigmoid(zg))).astype(BF16)


def _hgrn_sample(zhg, state, lb, norm_g):
    N = zhg.shape[0]
    G = 8
    full = lambda a: pl.BlockSpec(a.shape, lambda i: (0,) * a.ndim)
    st_spec = pl.BlockSpec((G, HEADS, HEAD_W, HEAD_W), lambda i: (i, 0, 0, 0))
    return pl.pallas_call(
        functools.partial(_hgrn_step_kernel, G=G),
        grid=(N // G,),
        in_specs=[pl.BlockSpec((G, HG_COLS), lambda i: (i, 0)), st_spec, full(lb), full(norm_g)],
        out_specs=[pl.BlockSpec((G, GROUP_W), lambda i: (i, 0)), st_spec],
        out_shape=[jax.ShapeDtypeStruct((N, GROUP_W), BF16), jax.ShapeDtypeStruct(state.shape, F32)],
        compiler_params=_cparams(("arbitrary",)),
        name="hgrn_sample",
    )(zhg, state, lb, norm_g)


def _attn_kernel(qt_ref, k_ref, vt_ref, sl_ref, feat_ref, lam_ref, gcol_ref, o_ref, m_ref, l_ref, acc_ref, *, blk):
    qi = pl.program_id(2)
    row = lax.broadcasted_iota(I32, (HEAD_W, blk), 0)
    qt = qt_ref[...].astype(F32)
    a_hi = sl_ref[0:1, :]
    a_lo = sl_ref[1:2, :]
    slope2 = sl_ref[2:3, :]
    def slope_rows(r0):
        in_rows = (row >= r0) & (row < r0 + 4)
        return jnp.where(in_rows, jnp.where((row - r0) % 2 == 0, a_hi, a_lo), 0.0)

    q_aug = [jnp.where(row < DA_HEAD, qt, slope_rows(DA_HEAD)).astype(BF16),
             jnp.where(row >= DA_HEAD, qt, slope_rows(0)).astype(BF16)]
    keep = [feat_ref[0] > 0, feat_ref[1] > 0]
    feats = [feat_ref[2], feat_ref[3]]
    m_ref[...] = jnp.full_like(m_ref, -jnp.inf)
    l_ref[...] = jnp.zeros_like(l_ref)
    acc_ref[...] = jnp.zeros_like(acc_ref)

    def block(kj, masked):
        k_start = pl.multiple_of(kj * blk, blk)
        kb = k_ref[pl.ds(k_start, blk), :]
        vt = vt_ref[kj]
        off = slope2 * jnp.full((1, blk), k_start - qi * blk, I32).astype(F32)
        if masked:
            ok = lax.broadcasted_iota(I32, (blk, 1), 0) <= lax.broadcasted_iota(I32, (1, blk), 1)
        for mi in range(2):
            st = _dot(jnp.where(keep[mi], kb, feats[mi]), q_aug[mi])
            if masked:
                st = jnp.where(ok, st, -jnp.inf)
            m_prev = m_ref[mi]
            m_new = jnp.maximum(m_prev, jnp.max(st, axis=0, keepdims=True) + off)
            p = jnp.exp2(st - (m_new - off))
            alpha = jnp.exp2(m_prev - m_new)
            l_ref[mi] = alpha * l_ref[mi] + jnp.sum(p, axis=0, keepdims=True)
            acc_ref[mi] = alpha * acc_ref[mi] + _dot(vt, p.astype(BF16))
            m_ref[mi] = m_new

    block(qi, True)

    def body(kj, c):
        block(kj, False)
        return c

    lax.fori_loop(0, qi, body, 0)

    ot = acc_ref[0] * (1.0 / l_ref[0]) - lam_ref[...] * (acc_ref[1] * (1.0 / l_ref[1]))
    ms = jnp.mean(ot * ot, axis=0, keepdims=True)
    o_ref[...] = (ot * lax.rsqrt(ms + RMS_EPS) * gcol_ref[...]).T.astype(BF16)


def _attn_prompt(qt, k, vt, lam, subln_g):
    B, nblk, _, blk = qt.shape
    T = k.shape[1]
    bf = lambda x: np.asarray(x, np.float32).astype(BF16).astype(np.float64)
    a = np.asarray(ALIBI_SLOPES, np.float64) * LOG2E
    a_hi = bf(a)
    a_lo = bf(a - a_hi)
    sl = np.zeros((HEADS, 8, blk), np.float32)
    sl[:, 0, :], sl[:, 1, :], sl[:, 2, :] = a_hi[:, None], a_lo[:, None], a.astype(np.float32)[:, None]
    lane = np.arange(HEAD_W)[None, :]
    r = np.arange(blk)[:, None]
    assert blk <= 256 * 256

    def pos_feat(l0):
        return (np.where((lane == l0) | (lane == l0 + 1), r // 256 * 256, 0)
                + np.where((lane == l0 + 2) | (lane == l0 + 3), r % 256, 0)).astype(np.float32)

    feat = np.stack([np.broadcast_to(lane < DA_HEAD, (blk, HEAD_W)).astype(np.float32),
                     np.broadcast_to(lane >= DA_HEAD, (blk, HEAD_W)).astype(np.float32),
                     pos_feat(DA_HEAD), pos_feat(0)])
    feat = jnp.asarray(feat, BF16)
    lam_row = jnp.full((1, blk), lam, F32)
    gcol = jnp.broadcast_to((subln_g.reshape(HEAD_W, 1) * (1.0 - LAM_INIT)), (HEAD_W, blk))
    full = lambda x: pl.BlockSpec(x.shape, lambda b, h, i: (0,) * x.ndim)
    return pl.pallas_call(
        functools.partial(_attn_kernel, blk=blk),
        grid=(B, HEADS, nblk),
        in_specs=[pl.BlockSpec((None, None, HEAD_W, blk), lambda b, h, i: (b, i, h, 0)),
                  pl.BlockSpec((None, T, HEAD_W), lambda b, h, i: (b, 0, h)),
                  pl.BlockSpec((None, nblk, HEAD_W, blk), lambda b, h, i: (b, 0, h, 0)),
                  pl.BlockSpec((None, 8, blk), lambda b, h, i: (h, 0, 0)),
                  full(feat), full(lam_row), full(gcol)],
        out_specs=pl.BlockSpec((None, blk, HEAD_W), lambda b, h, i: (b, i, h)),
        out_shape=jax.ShapeDtypeStruct((B, T, GROUP_W), BF16),
        scratch_shapes=[pltpu.VMEM((2, 1, blk), F32), pltpu.VMEM((2, 1, blk), F32), pltpu.VMEM((2, HEAD_W, blk), F32)],
        compiler_params=_cparams(("arbitrary", "arbitrary", "arbitrary")),
        name="attn_prompt",
    )(qt, k, vt, jnp.asarray(sl), feat, lam_row, gcol)


def _attn_decode_kernel(pt_ref, q_ref, kn_ref, vn_ref, bias_ref, lam_ref, g_ref, ck_hbm, cv_hbm, o_ref,
                        kbuf, vbuf, sems, *, n_pages):
    i = pl.program_id(0)
    slot = i % 2

    def page_copies(seq, s):
        for pg in range(n_pages):
            page = pt_ref[seq, pg]
            for h in range(HEADS):
                dst = (s, h, pl.ds(pg * PAGE_SIZE, PAGE_SIZE))
                yield pltpu.make_async_copy(ck_hbm.at[0, page, :, h, :], kbuf.at[dst], sems.at[0, s])
                yield pltpu.make_async_copy(cv_hbm.at[0, page, :, h, :], vbuf.at[dst], sems.at[1, s])

    @pl.when(i == 0)
    def _():
        for cp in page_copies(0, 0):
            cp.start()

    @pl.when(i + 1 < pl.num_programs(0))
    def _():
        for cp in page_copies(i + 1, 1 - slot):
            cp.start()

    for cp in page_copies(i, slot):
        cp.wait()

    r8 = lax.broadcasted_iota(I32, (8, GROUP_W), 0)
    c8 = lax.broadcasted_iota(I32, (8, GROUP_W), 1)
    qmat = jnp.where(c8 // DA_HEAD == r8, jnp.broadcast_to(q_ref[...].astype(F32), (8, GROUP_W)), 0.0)
    qmat_bf = qmat.astype(BF16)
    s = bias_ref[...]
    for h in range(HEADS):
        s = s + _dot_nt(qmat_bf[:, h * HEAD_W:(h + 1) * HEAD_W], kbuf[slot, h].astype(BF16))
    kn = kn_ref[...].astype(BF16).astype(F32)
    s_self = jnp.sum(qmat_bf.astype(F32) * kn, axis=-1, keepdims=True)
    m = jnp.maximum(jnp.max(s, -1, keepdims=True), s_self)
    p = jnp.exp(s - m)
    p_self = jnp.exp(s_self - m)
    inv_l = 1.0 / (jnp.sum(p, -1, keepdims=True) + p_self)
    coef = jnp.where(lax.broadcasted_iota(I32, (8, 1), 0) % 2 == 0, 1.0, -lam_ref[:, 0:1]) * inv_l
    w = (p * coef).astype(BF16)
    accs = [_dot(w, vbuf[slot, h].astype(BF16)) for h in range(HEADS)]
    acc = jnp.concatenate(accs, axis=1) + (p_self * coef) * vn_ref[...]
    o = jnp.sum(jnp.where(c8 // HEAD_W == r8 // 2, acc, 0.0), axis=0, keepdims=True)
    outs = []
    for h in range(HEADS):
        oh = o[:, h * HEAD_W:(h + 1) * HEAD_W]
        ms = jnp.mean(oh * oh, -1, keepdims=True)
        outs.append(oh * lax.rsqrt(ms + RMS_EPS) * g_ref[...] * (1.0 - LAM_INIT))
    o_ref[...] = jnp.concatenate(outs, axis=1).astype(BF16)


def _attn_sample(q, k_new, v_new, cache_k, cache_v, page_table, lam_row, subln_g):
    N, n_pages = page_table.shape
    past = n_pages * PAGE_SIZE
    kpos = np.arange(past, dtype=np.float32)[None, :]
    slope_rows = np.repeat(np.asarray(ALIBI_SLOPES, np.float32), 2)[:, None]
    bias = jnp.asarray(-slope_rows * (past - kpos))
    row = pl.BlockSpec((None, 1, GROUP_W), lambda i, pt: (i, 0, 0))
    full = lambda a: pl.BlockSpec(a.shape, lambda i, pt: (0,) * a.ndim)
    anyspec = pl.BlockSpec(memory_space=pl.ANY)
    grid_spec = pltpu.PrefetchScalarGridSpec(
        num_scalar_prefetch=1,
        grid=(N,),
        in_specs=[row, row, row, full(bias), full(lam_row), full(subln_g), anyspec, anyspec],
        out_specs=row,
        scratch_shapes=[pltpu.VMEM((2, HEADS, past, HEAD_W), F32), pltpu.VMEM((2, HEADS, past, HEAD_W), F32),
                        pltpu.SemaphoreType.DMA((2, 2))],
    )
    r3 = lambda a: a.reshape(N, 1, GROUP_W)
    out = pl.pallas_call(
        functools.partial(_attn_decode_kernel, n_pages=n_pages),
        grid_spec=grid_spec,
        out_shape=jax.ShapeDtypeStruct((N, 1, GROUP_W), BF16),
        compiler_params=_cparams(("arbitrary",)),
        name="attn_sample",
    )(page_table, r3(q), r3(k_new), r3(v_new), bias, lam_row, subln_g, cache_k, cache_v)
    return out.reshape(N, GROUP_W)


def _outproj_kernel(hg_ref, da_ref, x_ref, lg_ref, lb_ref, g1_ref, sc2_ref, sh2_ref, w_ref, l1g_ref, l1b_ref,
                    wrh_ref, wrl_ref, br_ref, u_ref, cin_ref,
                    x1_ref, h2_ref, idx_ref, tw_ref, rank_ref, cout_ref, run_ref):
    @pl.when((pl.program_id(0) == 0) & (pl.program_id(1) == 0))
    def _():
        run_ref[...] = cin_ref[...]

    x0 = _layer_norm(x_ref[...], lg_ref[...], lb_ref[...])
    mix = _dot(hg_ref[...], w_ref[0:GROUP_W, :]) + _dot(da_ref[...], w_ref[GROUP_W:2 * GROUP_W, :])
    x1 = _layer_norm(DEEPNORM_ALPHA * x0 + (1.0 + g1_ref[...]) * mix, l1g_ref[...], l1b_ref[...])
    x1_ref[...] = x1
    h2 = x1 * (1.0 + sc2_ref[...]) + sh2_ref[...]
    h2_ref[...] = h2

    hi = h2.astype(BF16)
    lo = (h2 - hi.astype(F32)).astype(BF16)
    wrh = wrh_ref[...]
    logits = _dot_nt(wrh, hi) + _dot_nt(wrh, lo) + _dot_nt(wrl_ref[...], hi) + br_ref[...]

    n_e, bm = logits.shape
    rows = lax.broadcasted_iota(I32, (n_e, bm), 0).astype(F32)
    vals, sels = [], []
    work = logits
    for kk in range(TOP_K):
        mx = jnp.max(work, axis=0, keepdims=True)
        ix = jnp.min(jnp.where(work == mx, rows, float(n_e)), axis=0, keepdims=True)
        sel = rows == ix
        idx_ref[kk:kk + 1, :] = ix.astype(I32)
        vals.append(mx)
        sels.append(sel)
        work = jnp.where(sel, -jnp.inf, work)
    es = [jnp.exp(vv - vals[0]) for vv in vals]
    inv = 1.0 / (es[0] + es[1] + es[2] + es[3])
    for kk in range(TOP_K):
        tw_ref[kk:kk + 1, :] = es[kk] * inv

    base = run_ref[...]
    for kk in range(TOP_K):
        oh = jnp.where(sels[kk], 1.0, 0.0)
        before = base + _dot(oh.astype(BF16), u_ref[...])
        rank_ref[kk:kk + 1, :] = jnp.sum(jnp.where(sels[kk], before, 0.0), axis=0, keepdims=True).astype(I32)
        base = base + jnp.sum(oh, axis=1, keepdims=True)
    run_ref[...] = base
    cout_ref[...] = base


def _outproj(hg, da, x, mods, consts, counts_in, bm):
    G, R, D = x.shape
    g1, sc2, sh2 = mods
    ln_g, ln_b, w_out_bf, l1g, l1b, wrh, wrl, br = consts
    nb = R // bm
    n_tok = G * R
    u = jnp.asarray(np.triu(np.ones((bm, bm), np.float32), 1), BF16)
    row = lambda w: pl.BlockSpec((None, bm, w), lambda g, i: (g, i, 0))
    full = lambda a: pl.BlockSpec(a.shape, lambda g, i: (0,) * a.ndim)
    tok_lanes = pl.BlockSpec((TOP_K, bm), lambda g, i: (0, g * nb + i))
    return pl.pallas_call(
        _outproj_kernel,
        grid=(G, nb),
        in_specs=[row(GROUP_W), row(GROUP_W), row(D), full(ln_g), full(ln_b),
                  _mod_spec(g1, bm), _mod_spec(sc2, bm), _mod_spec(sh2, bm),
                  full(w_out_bf), full(l1g), full(l1b), full(wrh), full(wrl), full(br), full(u), full(counts_in)],
        out_specs=[row(D), pl.BlockSpec((bm, D), lambda g, i: (g * nb + i, 0)),
                   tok_lanes, tok_lanes, tok_lanes, full(counts_in)],
        out_shape=[jax.ShapeDtypeStruct((G, R, D), F32),
                   jax.ShapeDtypeStruct((n_tok, D), F32),
                   jax.ShapeDtypeStruct((TOP_K, n_tok), I32),
                   jax.ShapeDtypeStruct((TOP_K, n_tok), F32),
                   jax.ShapeDtypeStruct((TOP_K, n_tok), I32),
                   jax.ShapeDtypeStruct(counts_in.shape, F32)],
        scratch_shapes=[pltpu.VMEM(counts_in.shape, F32)],
        compiler_params=_cparams(("arbitrary", "arbitrary")),
        name="outproj",
    )(hg, da, x, ln_g, ln_b, g1, sc2, sh2, w_out_bf, l1g, l1b, wrh, wrl, br, u, counts_in)


def _dispatch_kernel(dest_ref, h2_ref, xb_in_hbm, xb_hbm, sem, *, bt):
    del xb_in_hbm

    def body(t, c):
        for kk in range(TOP_K):
            pltpu.make_async_copy(h2_ref.at[t], xb_hbm.at[dest_ref[kk, t]], sem).start(priority=kk % 2)
        return c

    lax.fori_loop(0, bt, body, 0)
    for kk in range(TOP_K):
        pltpu.make_async_copy(h2_ref, xb_hbm.at[pl.ds(0, bt)], sem).wait()


def _dispatch(dest, h2, xb):
    n_tok = h2.shape[0]
    bt = TOK_BLOCK
    anyspec = pl.BlockSpec(memory_space=pl.ANY)
    return pl.pallas_call(
        functools.partial(_dispatch_kernel, bt=bt),
        grid=(n_tok // bt,),
        in_specs=[pl.BlockSpec((TOP_K, bt), lambda i: (0, i), memory_space=pltpu.SMEM),
                  pl.BlockSpec((bt, h2.shape[1]), lambda i: (i, 0)), anyspec],
        out_specs=anyspec,
        out_shape=jax.ShapeDtypeStruct(xb.shape, xb.dtype),
        scratch_shapes=[pltpu.SemaphoreType.DMA(())],
        input_output_aliases={2: 0},
        compiler_params=_cparams(("arbitrary",)),
        name="dispatch",
    )(dest, h2, xb)


def _experts_kernel(be_ref, nu_ref, x_ref, wu_ref, bu_ref, wd_ref, bd_ref, y_ref, wu_bf, wd_bf):
    j = pl.program_id(0)
    active = j < nu_ref[0]

    @pl.when(active & ((j == 0) | (be_ref[j] != be_ref[jnp.maximum(j - 1, 0)])))
    def _():
        wu_bf[...] = wu_ref[...].astype(BF16)
        wd_bf[...] = wd_ref[...].astype(BF16)

    @pl.when(active)
    def _():
        u = _dot(x_ref[...].astype(BF16), wu_bf[...]) + bu_ref[...]
        glu = jnp.minimum(u[:, :D_FF], SWIGLU_LIMIT)
        lin = jnp.clip(u[:, D_FF:], -SWIGLU_LIMIT, SWIGLU_LIMIT)
        act = glu * _sigmoid(SWIGLU_ALPHA * glu) * (lin + 1.0)
        y_ref[...] = _dot(act.astype(BF16), wd_bf[...]) + bd_ref[...]

    @pl.when(jnp.logical_not(active))
    def _():
        y_ref[...] = jnp.zeros_like(y_ref)


def _experts(blk_exp, n_used, xb, w_up, b_up, w_down, b_down):
    n_rows, w = xb.shape
    bm = MOE_BLOCK
    nb = n_rows // bm
    grid_spec = pltpu.PrefetchScalarGridSpec(
        num_scalar_prefetch=2,
        grid=(nb,),
        in_specs=[pl.BlockSpec((bm, w), lambda j, be, nu: (jnp.minimum(j, nu[0] - 1), 0)),
                  pl.BlockSpec((None, D_MODEL, 2 * D_FF), lambda j, be, nu: (be[j], 0, 0)),
                  pl.BlockSpec((None, 1, 2 * D_FF), lambda j, be, nu: (be[j], 0, 0)),
                  pl.BlockSpec((None, D_FF, D_MODEL), lambda j, be, nu: (be[j], 0, 0)),
                  pl.BlockSpec((None, 1, D_MODEL), lambda j, be, nu: (be[j], 0, 0))],
        out_specs=pl.BlockSpec((bm, w), lambda j, be, nu: (j, 0)),
        scratch_shapes=[pltpu.VMEM((D_MODEL, 2 * D_FF), BF16), pltpu.VMEM((D_FF, D_MODEL), BF16)],
    )
    return pl.pallas_call(
        _experts_kernel,
        grid_spec=grid_spec,
        out_shape=jax.ShapeDtypeStruct((n_rows, w), F32),
        compiler_params=_cparams(("arbitrary",)),
        name="experts",
    )(blk_exp, n_used, xb, w_up, b_up.reshape(N_EXPERTS, 1, 2 * D_FF), w_down, b_down.reshape(N_EXPERTS, 1, D_MODEL))


def _combine_kernel(dcur_ref, dnext_ref, x1_ref, tw_ref, g2_ref, lg_ref, lb_ref, yb_hbm, o_ref, buf, sems, *, bm):
    i = pl.program_id(0) * pl.num_programs(1) + pl.program_id(1)
    n = pl.num_programs(0) * pl.num_programs(1)
    slot = i % 2

    def issue(dref, s):
        def body(t, c):
            for kk in range(TOP_K):
                pltpu.make_async_copy(yb_hbm.at[dref[kk, t]], buf.at[s, kk, t], sems.at[s]).start(priority=kk % 2)
            return c
        lax.fori_loop(0, bm, body, 0)

    @pl.when(i == 0)
    def _():
        issue(dcur_ref, 0)

    @pl.when(i + 1 < n)
    def _():
        issue(dnext_ref, 1 - slot)

    for kk in range(TOP_K):
        pltpu.make_async_copy(yb_hbm.at[pl.ds(0, bm)], buf.at[slot, kk], sems.at[slot]).wait()

    tw = tw_ref[...]
    ff = tw[:, 0:1] * buf[slot, 0]
    for kk in range(1, TOP_K):
        ff = ff + tw[:, kk:kk + 1] * buf[slot, kk]
    o_ref[...] = _layer_norm(DEEPNORM_ALPHA * x1_ref[...] + (1.0 + g2_ref[...]) * ff, lg_ref[...], lb_ref[...])


def _combine(dest, tw_rows, x1, g2, ln_g, ln_b, yb, bm):
    G, R, D = x1.shape
    nb = R // bm
    n_blk = G * nb
    row = lambda w: pl.BlockSpec((None, bm, w), lambda g, i: (g, i, 0))
    full = lambda a: pl.BlockSpec(a.shape, lambda g, i: (0,) * a.ndim)
    cur = pl.BlockSpec((TOP_K, bm), lambda g, i: (0, g * nb + i), memory_space=pltpu.SMEM)
    nxt = pl.BlockSpec((TOP_K, bm), lambda g, i: (0, jnp.minimum(g * nb + i + 1, n_blk - 1)),
                       memory_space=pltpu.SMEM)
    return pl.pallas_call(
        functools.partial(_combine_kernel, bm=bm),
        grid=(G, nb),
        in_specs=[cur, nxt, row(D), pl.BlockSpec((bm, TOP_K), lambda g, i: (g * nb + i, 0)),
                  _mod_spec(g2, bm), full(ln_g), full(ln_b), pl.BlockSpec(memory_space=pl.ANY)],
        out_specs=row(D),
        out_shape=jax.ShapeDtypeStruct((G, R, D), F32),
        scratch_shapes=[pltpu.VMEM((2, TOP_K, bm, D), F32), pltpu.SemaphoreType.DMA((2,))],
        compiler_params=_cparams(("arbitrary", "arbitrary")),
        name="combine",
    )(dest, dest, x1, tw_rows, g2, ln_g, ln_b, yb)


def kernel(x_prompt, x_sample, c_prompt, c_sample, cache_k, cache_v, state_hgrn, page_table, ln_in_g, ln_in_b, w_ada, b_ada, w_in, hg_lb, hg_norm_g, da_lq1, da_lk1, da_lq2, da_lk2, da_subln_g, w_out, ln1_g, ln1_b, w_router, b_router, w_up, b_up, w_down, b_down, ln2_g, ln2_b):
    assert w_in.shape[0] == 1, "single-layer trunk"
    B, T, D = x_prompt.shape
    NS = x_sample.shape[0]
    n_prompt = B * T
    n_tot = n_prompt + NS
    r1 = lambda a: a.reshape(1, -1)

    lb = r1(jax.nn.softmax(hg_lb.astype(F32), axis=0)[0])
    lam = (jnp.exp(jnp.sum(da_lq1[0].astype(F32) * da_lk1[0].astype(F32)))
           - jnp.exp(jnp.sum(da_lq2[0].astype(F32) * da_lk2[0].astype(F32))) + LAM_INIT)
    lam_row = jnp.full((1, HEAD_W), lam, F32)
    ln_g, ln_b = r1(ln_in_g), r1(ln_in_b)
    w_in_bf = w_in[0].astype(BF16)
    wqt = w_in_bf[:, HG_COLS:HG_COLS + GROUP_W].T
    wvt = w_in_bf[:, HG_COLS + 2 * GROUP_W:].T
    w_out_bf = w_out[0].astype(BF16)
    wr_t = w_router[0].T
    wrh = wr_t.astype(BF16)
    wrl = (wr_t - wrh.astype(F32)).astype(BF16)
    br = b_router[0].reshape(N_EXPERTS, 1)
    norm_g, subln_g = r1(hg_norm_g[0]), r1(da_subln_g[0])

    n_c = B + NS
    c_all = jnp.concatenate([c_prompt, c_sample, jnp.zeros((-n_c % 8, D), F32)], axis=0)
    mod = _ada(c_all, w_ada[0], b_ada[0])
    mod_p = [mod[:B, j * D:(j + 1) * D].reshape(B, 1, D) for j in range(6)]
    mod_s = [mod[B:n_c, j * D:(j + 1) * D].reshape(1, NS, D) for j in range(6)]

    zhg_p, kf_p, vf_p, kb_p, qt_p, vt_p = _inproj(x_prompt, mod_p[1], mod_p[0], ln_g, ln_b, w_in_bf, INPROJ_BLOCK,
                                                  (wqt, wvt, ATTN_BLOCK))
    hg_p, s_p = _hgrn_prompt(zhg_p, lb, norm_g)
    da_p = _attn_prompt(qt_p, kb_p, vt_p, lam, subln_g)

    xs = x_sample.reshape(1, NS, D)
    zhg_s, kf_s, vf_s, q_s = _inproj(xs, mod_s[1], mod_s[0], ln_g, ln_b, w_in_bf, NS)
    hg_s, s_s = _hgrn_sample(zhg_s[0], state_hgrn[0], lb, norm_g)
    da_s = _attn_sample(q_s[0], kf_s.reshape(NS, GROUP_W), vf_s.reshape(NS, GROUP_W), cache_k, cache_v, page_table,
                        lam_row, subln_g)

    consts = (ln_g, ln_b, w_out_bf, r1(ln1_g[0]), r1(ln1_b[0]), wrh, wrl, br)
    counts0 = jnp.zeros((N_EXPERTS, 1), F32)
    x1_p, h2_p, idx_p, tw_p, rank_p, counts1 = _outproj(
        hg_p, da_p, x_prompt, (mod_p[2], mod_p[4], mod_p[3]), consts, counts0, ROW_BLOCK)
    x1_s, h2_s, idx_s, tw_s, rank_s, counts = _outproj(
        hg_s.reshape(1, NS, GROUP_W), da_s.reshape(1, NS, GROUP_W), xs, (mod_s[2], mod_s[4], mod_s[3]), consts,
        counts1, NS)

    cnt = counts[:, 0].astype(I32)
    padded = (cnt + MOE_BLOCK - 1) // MOE_BLOCK * MOE_BLOCK
    pad_end = jnp.cumsum(padded)
    pad_start = pad_end - padded
    e_ids = jnp.arange(N_EXPERTS, dtype=I32)

    def slot_of(idx, rank):
        return jnp.sum(jnp.where(idx[..., None] == e_ids, pad_start, 0), axis=-1) + rank

    dest_p = slot_of(idx_p, rank_p)
    dest_s = slot_of(idx_s, rank_s)
    n_blocks = -(-n_tot * TOP_K // MOE_BLOCK) + N_EXPERTS
    blk_row0 = jnp.arange(n_blocks, dtype=I32) * MOE_BLOCK
    blk_exp = jnp.minimum(jnp.sum((pad_end[None, :] <= blk_row0[:, None]).astype(I32), axis=1), N_EXPERTS - 1)
    n_used = (pad_end[-1:] // MOE_BLOCK).astype(I32)

    xb = jnp.zeros((n_blocks * MOE_BLOCK, D), F32)
    xb = _dispatch(dest_p, h2_p, xb)
    xb = _dispatch(dest_s, h2_s, xb)
    yb = _experts(blk_exp, n_used, xb, w_up[0], b_up[0], w_down[0], b_down[0])

    l2g, l2b = r1(ln2_g[0]), r1(ln2_b[0])
    y_p = _combine(dest_p, tw_p.T, x1_p, mod_p[5], l2g, l2b, yb, TOK_BLOCK)
    y_s = _combine(dest_s, tw_s.T, x1_s, mod_s[5], l2g, l2b, yb, TOK_BLOCK)

    smp = lambda a: a.reshape(1, NS, 1, HEADS, HEAD_W)
    return (y_p, y_s.reshape(NS, 1, D), kf_p[None], vf_p[None], s_p[None], smp(kf_s), smp(vf_s), s_s[None])
```

```python
import functools
import math

import numpy as np
import jax
import jax.numpy as jnp
from jax import lax
from jax.experimental import pallas as pl
from jax.experimental.pallas import tpu as pltpu

F32, BF16, I32, U32 = jnp.float32, jnp.bfloat16, jnp.int32, jnp.uint32

D_MODEL = 1024
HEADS = 4
HEAD_W = 128
DA_HEAD = 64
GROUP_W = HEADS * HEAD_W
HG_COLS = 4 * GROUP_W
IN_WIDTH = HG_COLS + 3 * GROUP_W
N_EXPERTS = 32
TOP_K = 4
D_FF = 1024
SWIGLU_ALPHA = 1.702
SWIGLU_LIMIT = 7.0
DEEPNORM_ALPHA = 2.0 ** 0.25
LN_EPS = 1e-5
RMS_EPS = 1e-6
LAM_INIT = 0.8 - 0.6 * math.exp(-0.3 * 0)
LOG2E = math.log2(math.e)
ALIBI_SLOPES = tuple(2.0 ** (-8.0 * (h + 1) / HEADS) for h in range(HEADS))
PAGE_SIZE = 128
EXP_CLAMP = 80.0

VMEM_LIMIT = 56 * 1024 * 1024
MOE_BLOCK = 256
HG_CHUNK = 128
ATTN_BLOCK = 1024
INPROJ_BLOCK = 512
ROW_BLOCK = 512
TOK_BLOCK = 128


def _cparams(sem):
    return pltpu.CompilerParams(dimension_semantics=sem, vmem_limit_bytes=VMEM_LIMIT)


def _sigmoid(x):
    return 1.0 / (1.0 + jnp.exp(-x))


def _layer_norm(x, g, b):
    mu = jnp.mean(x, -1, keepdims=True)
    xc = x - mu
    var = jnp.mean(xc * xc, -1, keepdims=True)
    return xc * lax.rsqrt(var + LN_EPS) * g + b


def _dot(a, b):
    return jnp.dot(a, b, preferred_element_type=F32)


def _dot_nt(a, b):
    return lax.dot_general(a, b, (((1,), (1,)), ((), ())), preferred_element_type=F32)


def _dot_tn(a, b):
    return lax.dot_general(a, b, (((0,), (0,)), ((), ())), preferred_element_type=F32)


ROW_TILE = (D_MODEL // HEAD_W, HEAD_W)


def _load_row_tiles(ref):
    return jnp.concatenate([ref[:, c, :] for c in range(ROW_TILE[0])], axis=1)


def _ada_kernel(c_ref, w_ref, b_ref, o_ref):
    c = c_ref[...]
    a = (c * _sigmoid(c)).astype(BF16)
    o_ref[...] = _dot(a, w_ref[...].astype(BF16)) + b_ref[...]


def _ada(c, w_ada, b_ada):
    rows, d = c.shape
    n = w_ada.shape[1]
    bn = 1536
    return pl.pallas_call(
        _ada_kernel,
        grid=(n // bn,),
        in_specs=[pl.BlockSpec((rows, d), lambda j: (0, 0)),
                  pl.BlockSpec((d, bn), lambda j: (0, j)),
                  pl.BlockSpec((1, bn), lambda j: (0, j))],
        out_specs=pl.BlockSpec((rows, bn), lambda j: (0, j)),
        out_shape=jax.ShapeDtypeStruct((rows, n), F32),
        compiler_params=_cparams(("arbitrary",)),
        name="ada",
    )(c, w_ada, b_ada.reshape(1, n))


def _inproj_common(x_ref, g_ref, b_ref, sc_ref, sh_ref, w_ref, zhg_ref, kf_ref, vf_ref):
    x0 = _layer_norm(x_ref[...], g_ref[...], b_ref[...])
    h = (x0 * (1.0 + sc_ref[...]) + sh_ref[...]).astype(BF16)
    zhg_ref[...] = _dot(h, w_ref[:, 0:HG_COLS])
    c0 = HG_COLS + GROUP_W
    k = _dot(h, w_ref[:, c0:c0 + GROUP_W])
    v = _dot(h, w_ref[:, c0 + GROUP_W:c0 + 2 * GROUP_W])
    for hd in range(HEADS):
        kf_ref[:, hd, :] = k[:, hd * HEAD_W:(hd + 1) * HEAD_W]
        vf_ref[:, hd, :] = v[:, hd * HEAD_W:(hd + 1) * HEAD_W]
    return h, k


def _inproj_prompt_kernel(x_ref, g_ref, b_ref, sc_ref, sh_ref, w_ref, wqt_ref, wvt_ref,
                          zhg_ref, kf_ref, vf_ref, kb_ref, qt_ref, vt_ref):
    h, k = _inproj_common(x_ref, g_ref, b_ref, sc_ref, sh_ref, w_ref, zhg_ref, kf_ref, vf_ref)
    kb_ref[...] = k.astype(BF16)
    qt_ref[...] = (_dot_nt(wqt_ref[...], h) * (DA_HEAD ** -0.5 * LOG2E)).astype(BF16)
    vt_ref[...] = _dot_nt(wvt_ref[...], h).astype(BF16)


def _inproj_sample_kernel(x_ref, g_ref, b_ref, sc_ref, sh_ref, w_ref, zhg_ref, kf_ref, vf_ref, q_ref):
    h, _ = _inproj_common(x_ref, g_ref, b_ref, sc_ref, sh_ref, w_ref, zhg_ref, kf_ref, vf_ref)
    q_ref[...] = (_dot(h, w_ref[:, HG_COLS:HG_COLS + GROUP_W]) * (DA_HEAD ** -0.5)).astype(BF16)


def _mod_spec(mod, bm):
    if mod.shape[1] == 1:
        return pl.BlockSpec((None, 1, mod.shape[2]), lambda g, i: (g, 0, 0))
    return pl.BlockSpec((None, bm, mod.shape[2]), lambda g, i: (g, i, 0))


def _inproj(x, sc, sh, ln_g, ln_b, w_in_bf, bm, transposed=None):
    G, R, D = x.shape
    nb = R // bm
    row = lambda w: pl.BlockSpec((None, bm, w), lambda g, i: (g, i, 0))
    full = lambda a: pl.BlockSpec(a.shape, lambda g, i: (0,) * a.ndim)
    heads = pl.BlockSpec((None, bm, HEADS, HEAD_W), lambda g, i: (g, i, 0, 0))
    heads_shape = jax.ShapeDtypeStruct((G, R, HEADS, HEAD_W), F32)
    sds = lambda w, dt: jax.ShapeDtypeStruct((G, R, w), dt)
    args = [x, ln_g, ln_b, sc, sh, w_in_bf]
    in_specs = [row(D), full(ln_g), full(ln_b), _mod_spec(sc, bm), _mod_spec(sh, bm), full(w_in_bf)]
    out_specs = [row(HG_COLS), heads, heads, row(GROUP_W)]
    out_shape = [sds(HG_COLS, F32), heads_shape, heads_shape, sds(GROUP_W, BF16)]
    body = _inproj_sample_kernel
    if transposed is not None:
        wqt, wvt, tblk = transposed
        per = tblk // bm
        tr = pl.BlockSpec((None, None, GROUP_W, bm), lambda g, i: (g, i // per, 0, i % per))
        tr_shape = jax.ShapeDtypeStruct((G, R // tblk, GROUP_W, tblk), BF16)
        args += [wqt, wvt]
        in_specs += [full(wqt), full(wvt)]
        out_specs += [tr, tr]
        out_shape += [tr_shape, tr_shape]
        body = _inproj_prompt_kernel
    return pl.pallas_call(
        body,
        grid=(G, nb),
        in_specs=in_specs,
        out_specs=out_specs,
        out_shape=out_shape,
        compiler_params=_cparams(("arbitrary", "arbitrary")),
        name="inproj",
    )(*args)


def _hgrn_gates(zq, zf, lb):
    q = zq * _sigmoid(zq)
    f = lb + (1.0 - lb) * _sigmoid(zf)
    k = (1.0 - lb) * _sigmoid(-zf)
    return q, jnp.log(f), k


def _hgrn_kernel(z_ref, lb_ref, ng_ref, lvl_ref, tri_ref, o_ref, sfin_ref, st_ref, *, C):
    t = pl.program_id(1)

    @pl.when(t == 0)
    def _():
        st_ref[...] = jnp.zeros_like(st_ref)

    lvl = lvl_ref[...]
    tri = tri_ref[...]
    n_levels = int(math.log2(C)) - 3
    ng = ng_ref[...]
    for h in range(HEADS):
        cs = slice(h * HEAD_W, (h + 1) * HEAD_W)
        zq = z_ref[:, h * HEAD_W:(h + 1) * HEAD_W]
        zf = z_ref[:, GROUP_W + h * HEAD_W:GROUP_W + (h + 1) * HEAD_W]
        v = z_ref[:, 2 * GROUP_W + h * HEAD_W:2 * GROUP_W + (h + 1) * HEAD_W].astype(BF16)
        zg = z_ref[:, 3 * GROUP_W + h * HEAD_W:3 * GROUP_W + (h + 1) * HEAD_W]
        q, g, k = _hgrn_gates(zq, zf, lb_ref[:, cs])
        g1 = g.astype(BF16)
        r1 = g - g1.astype(F32)
        g2 = r1.astype(BF16)
        g3 = (r1 - g2.astype(F32)).astype(BF16)
        b = _dot(tri, g1) + _dot(tri, g2) + _dot(tri, g3)

        b8 = b.reshape(C // 8, 8, HEAD_W)
        bmid = jnp.broadcast_to(b8[:, 3:4, :], b8.shape).reshape(C, HEAD_W)
        e = jnp.clip(b - bmid, -EXP_CLAMP, EXP_CLAMP)
        a = jnp.where(lvl == 0, _dot_nt((q * jnp.exp(e)).astype(BF16), (k * jnp.exp(-e)).astype(BF16)), 0.0)
        for li in range(1, n_levels + 1):
            m = 4 << li
            bb = b.reshape(C // (2 * m), 2 * m, HEAD_W)
            d = b - jnp.broadcast_to(bb[:, m - 1:m, :], bb.shape).reshape(C, HEAD_W)
            qs = (q * jnp.exp(jnp.minimum(d, 0.0))).astype(BF16)
            ks = (k * jnp.exp(jnp.minimum(-d, 0.0))).astype(BF16)
            a = jnp.where(lvl == li, _dot_nt(qs, ks), a)

        st = st_ref[h]
        o = _dot(a.astype(BF16), v) + _dot_nt((q * jnp.exp(b)).astype(BF16), st.astype(BF16))
        b_last = b[C - 1:C, :]
        kd = (k * jnp.exp(b_last - b)).astype(BF16)
        st_ref[h] = st * jnp.exp(b_last) + _dot_tn(v, kd)

        ms = jnp.mean(o * o, -1, keepdims=True)
        o_ref[:, cs] = (o * lax.rsqrt(ms + RMS_EPS) * ng * (zg * _sigmoid(zg))).astype(BF16)

    @pl.when(t == pl.num_programs(1) - 1)
    def _():
        for h in range(HEADS):
            sfin_ref[h] = st_ref[h].T


def _hgrn_level_table(C):
    t = np.arange(C)[:, None]
    s = np.arange(C)[None, :]
    x = t ^ s
    lvl = np.zeros((C, C), np.int32)
    m = 8
    while m < C:
        lvl += (x >= m).astype(np.int32)
        m *= 2
    return np.where(s <= t, lvl, -1).astype(np.int32)


def _hgrn_prompt(zhg, lb, norm_g):
    B, T, _ = zhg.shape
    C = HG_CHUNK
    lvl = jnp.asarray(_hgrn_level_table(C))
    tri = jnp.asarray(np.tril(np.ones((C, C), np.float32)), BF16)
    full = lambda a: pl.BlockSpec(a.shape, lambda b, t: (0,) * a.ndim)
    return pl.pallas_call(
        functools.partial(_hgrn_kernel, C=C),
        grid=(B, T // C),
        in_specs=[pl.BlockSpec((None, C, HG_COLS), lambda b, t: (b, t, 0)), full(lb), full(norm_g), full(lvl), full(tri)],
        out_specs=[pl.BlockSpec((None, C, GROUP_W), lambda b, t: (b, t, 0)),
                   pl.BlockSpec((None, HEADS, HEAD_W, HEAD_W), lambda b, t: (b, 0, 0, 0))],
        out_shape=[jax.ShapeDtypeStruct((B, T, GROUP_W), BF16),
                   jax.ShapeDtypeStruct((B, HEADS, HEAD_W, HEAD_W), F32)],
        scratch_shapes=[pltpu.VMEM((HEADS, HEAD_W, HEAD_W), F32)],
        compiler_params=_cparams(("arbitrary", "arbitrary")),
        name="hgrn_prompt",
    )(zhg, lb, norm_g, lvl, tri)


def _hgrn_step_kernel(z_ref, s_ref, lb_ref, ng_ref, o_ref, so_ref, *, G):
    ng = ng_ref[...]
    for h in range(HEADS):
        cs = slice(h * HEAD_W, (h + 1) * HEAD_W)
        zq = z_ref[:, h * HEAD_W:(h + 1) * HEAD_W]
        zf = z_ref[:, GROUP_W + h * HEAD_W:GROUP_W + (h + 1) * HEAD_W]
        v = z_ref[:, 2 * GROUP_W + h * HEAD_W:2 * GROUP_W + (h + 1) * HEAD_W]
        zg = z_ref[:, 3 * GROUP_W + h * HEAD_W:3 * GROUP_W + (h + 1) * HEAD_W]
        lb = lb_ref[:, cs]
        q = zq * _sigmoid(zq)
        f = lb + (1.0 - lb) * _sigmoid(zf)
        k = (1.0 - lb) * _sigmoid(-zf)
        qT, fT, kT = q.T, f.T, k.T
        rows = []
        for j in range(G):
            s_new = fT[:, j:j + 1] * s_ref[j, h] + kT[:, j:j + 1] * v[j:j + 1, :]
            so_ref[j, h] = s_new
            rows.append(jnp.sum(s_new * qT[:, j:j + 1], axis=0, keepdims=True))
        o = jnp.concatenate(rows, axis=0)
        ms = jnp.mean(o * o, -1, keepdims=True)
        o_ref[:, cs] = (o * lax.rsqrt(ms + RMS_EPS) * ng * (zg * _sigmoid(zg))).astype(BF16)


def _hgrn_sample(zhg, state, lb, norm_g):
    N = zhg.shape[0]
    G = 8
    full = lambda a: pl.BlockSpec(a.shape, lambda i: (0,) * a.ndim)
    st_spec = pl.BlockSpec((G, HEADS, HEAD_W, HEAD_W), lambda i: (i, 0, 0, 0))
    return pl.pallas_call(
        functools.partial(_hgrn_step_kernel, G=G),
        grid=(N // G,),
        in_specs=[pl.BlockSpec((G, HG_COLS), lambda i: (i, 0)), st_spec, full(lb), full(norm_g)],
        out_specs=[pl.BlockSpec((G, GROUP_W), lambda i: (i, 0)), st_spec],
        out_shape=[jax.ShapeDtypeStruct((N, GROUP_W), BF16), jax.ShapeDtypeStruct(state.shape, F32)],
        compiler_params=_cparams(("arbitrary",)),
        name="hgrn_sample",
    )(zhg, state, lb, norm_g)


def _attn_kernel(qt_ref, k_ref, vt_ref, sl_ref, feat_ref, lam_ref, gcol_ref, o_ref, m_ref, l_ref, acc_ref, *, blk):
    qi = pl.program_id(2)
    row = lax.broadcasted_iota(I32, (HEAD_W, blk), 0)
    qt = qt_ref[...].astype(F32)
    a_hi = sl_ref[0:1, :]
    a_lo = sl_ref[1:2, :]
    slope2 = sl_ref[2:3, :]
    def slope_rows(r0):
        in_rows = (row >= r0) & (row < r0 + 4)
        return jnp.where(in_rows, jnp.where((row - r0) % 2 == 0, a_hi, a_lo), 0.0)

    q_aug = [jnp.where(row < DA_HEAD, qt, slope_rows(DA_HEAD)).astype(BF16),
             jnp.where(row >= DA_HEAD, qt, slope_rows(0)).astype(BF16)]
    keep = [feat_ref[0] > 0, feat_ref[1] > 0]
    feats = [feat_ref[2], feat_ref[3]]
    m_ref[...] = jnp.full_like(m_ref, -jnp.inf)
    l_ref[...] = jnp.zeros_like(l_ref)
    acc_ref[...] = jnp.zeros_like(acc_ref)

    def block(kj, masked):
        k_start = pl.multiple_of(kj * blk, blk)
        kb = k_ref[pl.ds(k_start, blk), :]
        vt = vt_ref[kj]
        off = slope2 * jnp.full((1, blk), k_start - qi * blk, I32).astype(F32)
        if masked:
            ok = lax.broadcasted_iota(I32, (blk, 1), 0) <= lax.broadcasted_iota(I32, (1, blk), 1)
        for mi in range(2):
            st = _dot(jnp.where(keep[mi], kb, feats[mi]), q_aug[mi])
            if masked:
                st = jnp.where(ok, st, -jnp.inf)
            m_prev = m_ref[mi]
            m_new = jnp.maximum(m_prev, jnp.max(st, axis=0, keepdims=True) + off)
            p = jnp.exp2(st - (m_new - off))
            alpha = jnp.exp2(m_prev - m_new)
            l_ref[mi] = alpha * l_ref[mi] + jnp.sum(p, axis=0, keepdims=True)
            acc_ref[mi] = alpha * acc_ref[mi] + _dot(vt, p.astype(BF16))
            m_ref[mi] = m_new

    block(qi, True)

    def body(kj, c):
        block(kj, False)
        return c

    lax.fori_loop(0, qi, body, 0)

    ot = acc_ref[0] * (1.0 / l_ref[0]) - lam_ref[...] * (acc_ref[1] * (1.0 / l_ref[1]))
    ms = jnp.mean(ot * ot, axis=0, keepdims=True)
    o_ref[...] = (ot * lax.rsqrt(ms + RMS_EPS) * gcol_ref[...]).T.astype(BF16)


def _attn_prompt(qt, k, vt, lam, subln_g):
    B, nblk, _, blk = qt.shape
    T = k.shape[1]
    bf = lambda x: np.asarray(x, np.float32).astype(BF16).astype(np.float64)
    a = np.asarray(ALIBI_SLOPES, np.float64) * LOG2E
    a_hi = bf(a)
    a_lo = bf(a - a_hi)
    sl = np.zeros((HEADS, 8, blk), np.float32)
    sl[:, 0, :], sl[:, 1, :], sl[:, 2, :] = a_hi[:, None], a_lo[:, None], a.astype(np.float32)[:, None]
    lane = np.arange(HEAD_W)[None, :]
    r = np.arange(blk)[:, None]
    assert blk <= 256 * 256

    def pos_feat(l0):
        return (np.where((lane == l0) | (lane == l0 + 1), r // 256 * 256, 0)
                + np.where((lane == l0 + 2) | (lane == l0 + 3), r % 256, 0)).astype(np.float32)

    feat = np.stack([np.broadcast_to(lane < DA_HEAD, (blk, HEAD_W)).astype(np.float32),
                     np.broadcast_to(lane >= DA_HEAD, (blk, HEAD_W)).astype(np.float32),
                     pos_feat(DA_HEAD), pos_feat(0)])
    feat = jnp.asarray(feat, BF16)
    lam_row = jnp.full((1, blk), lam, F32)
    gcol = jnp.broadcast_to((subln_g.reshape(HEAD_W, 1) * (1.0 - LAM_INIT)), (HEAD_W, blk))
    full = lambda x: pl.BlockSpec(x.shape, lambda b, h, i: (0,) * x.ndim)
    return pl.pallas_call(
        functools.partial(_attn_kernel, blk=blk),
        grid=(B, HEADS, nblk),
        in_specs=[pl.BlockSpec((None, None, HEAD_W, blk), lambda b, h, i: (b, i, h, 0)),
                  pl.BlockSpec((None, T, HEAD_W), lambda b, h, i: (b, 0, h)),
                  pl.BlockSpec((None, nblk, HEAD_W, blk), lambda b, h, i: (b, 0, h, 0)),
                  pl.BlockSpec((None, 8, blk), lambda b, h, i: (h, 0, 0)),
                  full(feat), full(lam_row), full(gcol)],
        out_specs=pl.BlockSpec((None, blk, HEAD_W), lambda b, h, i: (b, i, h)),
        out_shape=jax.ShapeDtypeStruct((B, T, GROUP_W), BF16),
        scratch_shapes=[pltpu.VMEM((2, 1, blk), F32), pltpu.VMEM((2, 1, blk), F32), pltpu.VMEM((2, HEAD_W, blk), F32)],
        compiler_params=_cparams(("arbitrary", "arbitrary", "arbitrary")),
        name="attn_prompt",
    )(qt, k, vt, jnp.asarray(sl), feat, lam_row, gcol)


def _attn_decode_kernel(pt_ref, q_ref, kn_ref, vn_ref, bias_ref, lam_ref, g_ref, ck_hbm, cv_hbm, o_ref,
                        kbuf, vbuf, sems, *, n_pages):
    i = pl.program_id(0)
    slot = i % 2

    def page_copies(seq, s):
        for pg in range(n_pages):
            page = pt_ref[seq, pg]
            for h in range(HEADS):
                dst = (s, h, pl.ds(pg * PAGE_SIZE, PAGE_SIZE))
                yield pltpu.make_async_copy(ck_hbm.at[0, page, :, h, :], kbuf.at[dst], sems.at[0, s])
                yield pltpu.make_async_copy(cv_hbm.at[0, page, :, h, :], vbuf.at[dst], sems.at[1, s])

    @pl.when(i == 0)
    def _():
        for cp in page_copies(0, 0):
            cp.start()

    @pl.when(i + 1 < pl.num_programs(0))
    def _():
        for cp in page_copies(i + 1, 1 - slot):
            cp.start()

    for cp in page_copies(i, slot):
        cp.wait()

    r8 = lax.broadcasted_iota(I32, (8, GROUP_W), 0)
    c8 = lax.broadcasted_iota(I32, (8, GROUP_W), 1)
    qmat = jnp.where(c8 // DA_HEAD == r8, jnp.broadcast_to(q_ref[...].astype(F32), (8, GROUP_W)), 0.0)
    qmat_bf = qmat.astype(BF16)
    s = bias_ref[...]
    for h in range(HEADS):
        s = s + _dot_nt(qmat_bf[:, h * HEAD_W:(h + 1) * HEAD_W], kbuf[slot, h].astype(BF16))
    kn = kn_ref[...].astype(BF16).astype(F32)
    s_self = jnp.sum(qmat_bf.astype(F32) * kn, axis=-1, keepdims=True)
    m = jnp.maximum(jnp.max(s, -1, keepdims=True), s_self)
    p = jnp.exp(s - m)
    p_self = jnp.exp(s_self - m)
    inv_l = 1.0 / (jnp.sum(p, -1, keepdims=True) + p_self)
    coef = jnp.where(lax.broadcasted_iota(I32, (8, 1), 0) % 2 == 0, 1.0, -lam_ref[:, 0:1]) * inv_l
    w = (p * coef).astype(BF16)
    accs = [_dot(w, vbuf[slot, h].astype(BF16)) for h in range(HEADS)]
    acc = jnp.concatenate(accs, axis=1) + (p_self * coef) * vn_ref[...]
    o = jnp.sum(jnp.where(c8 // HEAD_W == r8 // 2, acc, 0.0), axis=0, keepdims=True)
    outs = []
    for h in range(HEADS):
        oh = o[:, h * HEAD_W:(h + 1) * HEAD_W]
        ms = jnp.mean(oh * oh, -1, keepdims=True)
        outs.append(oh * lax.rsqrt(ms + RMS_EPS) * g_ref[...] * (1.0 - LAM_INIT))
    o_ref[...] = jnp.concatenate(outs, axis=1).astype(BF16)


def _attn_sample(q, k_new, v_new, cache_k, cache_v, page_table, lam_row, subln_g):
    N, n_pages = page_table.shape
    past = n_pages * PAGE_SIZE
    kpos = np.arange(past, dtype=np.float32)[None, :]
    slope_rows = np.repeat(np.asarray(ALIBI_SLOPES, np.float32), 2)[:, None]
    bias = jnp.asarray(-slope_rows * (past - kpos))
    row = pl.BlockSpec((None, 1, GROUP_W), lambda i, pt: (i, 0, 0))
    full = lambda a: pl.BlockSpec(a.shape, lambda i, pt: (0,) * a.ndim)
    anyspec = pl.BlockSpec(memory_space=pl.ANY)
    grid_spec = pltpu.PrefetchScalarGridSpec(
        num_scalar_prefetch=1,
        grid=(N,),
        in_specs=[row, row, row, full(bias), full(lam_row), full(subln_g), anyspec, anyspec],
        out_specs=row,
        scratch_shapes=[pltpu.VMEM((2, HEADS, past, HEAD_W), F32), pltpu.VMEM((2, HEADS, past, HEAD_W), F32),
                        pltpu.SemaphoreType.DMA((2, 2))],
    )
    r3 = lambda a: a.reshape(N, 1, GROUP_W)
    out = pl.pallas_call(
        functools.partial(_attn_decode_kernel, n_pages=n_pages),
        grid_spec=grid_spec,
        out_shape=jax.ShapeDtypeStruct((N, 1, GROUP_W), BF16),
        compiler_params=_cparams(("arbitrary",)),
        name="attn_sample",
    )(page_table, r3(q), r3(k_new), r3(v_new), bias, lam_row, subln_g, cache_k, cache_v)
    return out.reshape(N, GROUP_W)


def _outproj_kernel(hg_ref, da_ref, x_ref, lg_ref, lb_ref, g1_ref, sc2_ref, sh2_ref, w_ref, l1g_ref, l1b_ref,
                    wrh_ref, wrl_ref, br_ref, u_ref, cin_ref,
                    x1_ref, h2_hbm, idx_ref, tw_ref, rank_ref, cout_ref, run_ref, hbuf, hsem, *, n_steps):
    step = pl.program_id(0) * pl.num_programs(1) + pl.program_id(1)
    slot = step % 2
    bm = x_ref.shape[0]
    h2_copies = lambda st, s: _tile_copies(hbuf.at[s], h2_hbm, st * bm, bm, hsem.at[s], True)

    @pl.when(step == 0)
    def _():
        run_ref[...] = cin_ref[...]

    x0 = _layer_norm(x_ref[...], lg_ref[...], lb_ref[...])
    mix = _dot(hg_ref[...], w_ref[0:GROUP_W, :]) + _dot(da_ref[...], w_ref[GROUP_W:2 * GROUP_W, :])
    x1 = _layer_norm(DEEPNORM_ALPHA * x0 + (1.0 + g1_ref[...]) * mix, l1g_ref[...], l1b_ref[...])
    x1_ref[...] = x1
    h2 = x1 * (1.0 + sc2_ref[...]) + sh2_ref[...]

    @pl.when(step >= 2)
    def _():
        for cp in h2_copies(step - 2, slot):
            cp.wait()

    hbuf[slot] = h2
    for cp in h2_copies(step, slot):
        cp.start()

    @pl.when(step == n_steps - 1)
    def _():
        tail = h2_copies(step, slot)
        if n_steps > 1:
            tail = h2_copies(step - 1, 1 - slot) + tail
        for cp in tail:
            cp.wait()

    hi = h2.astype(BF16)
    lo = (h2 - hi.astype(F32)).astype(BF16)
    wrh = wrh_ref[...]
    logits = _dot_nt(wrh, hi) + _dot_nt(wrh, lo) + _dot_nt(wrl_ref[...], hi) + br_ref[...]

    n_e, bm = logits.shape
    rows = lax.broadcasted_iota(I32, (n_e, bm), 0).astype(F32)
    vals, sels = [], []
    work = logits
    for kk in range(TOP_K):
        mx = jnp.max(work, axis=0, keepdims=True)
        ix = jnp.min(jnp.where(work == mx, rows, float(n_e)), axis=0, keepdims=True)
        sel = rows == ix
        idx_ref[kk:kk + 1, :] = ix.astype(I32)
        vals.append(mx)
        sels.append(sel)
        work = jnp.where(sel, -jnp.inf, work)
    es = [jnp.exp(vv - vals[0]) for vv in vals]
    inv = 1.0 / (es[0] + es[1] + es[2] + es[3])
    for kk in range(TOP_K):
        tw_ref[kk:kk + 1, :] = es[kk] * inv

    base = run_ref[...]
    for kk in range(TOP_K):
        oh = jnp.where(sels[kk], 1.0, 0.0)
        before = base + _dot(oh.astype(BF16), u_ref[...])
        rank_ref[kk:kk + 1, :] = jnp.sum(jnp.where(sels[kk], before, 0.0), axis=0, keepdims=True).astype(I32)
        base = base + jnp.sum(oh, axis=1, keepdims=True)
    run_ref[...] = base
    cout_ref[...] = base


def _outproj(hg, da, x, mods, consts, counts_in, bm):
    G, R, D = x.shape
    g1, sc2, sh2 = mods
    ln_g, ln_b, w_out_bf, l1g, l1b, wrh, wrl, br = consts
    nb = R // bm
    n_tok = G * R
    u = jnp.asarray(np.triu(np.ones((bm, bm), np.float32), 1), BF16)
    row = lambda w: pl.BlockSpec((None, bm, w), lambda g, i: (g, i, 0))
    full = lambda a: pl.BlockSpec(a.shape, lambda g, i: (0,) * a.ndim)
    tok_lanes = pl.BlockSpec((TOP_K, bm), lambda g, i: (0, g * nb + i))
    return pl.pallas_call(
        functools.partial(_outproj_kernel, n_steps=G * nb),
        grid=(G, nb),
        in_specs=[row(GROUP_W), row(GROUP_W), row(D), full(ln_g), full(ln_b),
                  _mod_spec(g1, bm), _mod_spec(sc2, bm), _mod_spec(sh2, bm),
                  full(w_out_bf), full(l1g), full(l1b), full(wrh), full(wrl), full(br), full(u), full(counts_in)],
        out_specs=[row(D), pl.BlockSpec(memory_space=pl.ANY), tok_lanes, tok_lanes, tok_lanes, full(counts_in)],
        out_shape=[jax.ShapeDtypeStruct((G, R, D), F32),
                   jax.ShapeDtypeStruct((n_tok,) + ROW_TILE, F32),
                   jax.ShapeDtypeStruct((TOP_K, n_tok), I32),
                   jax.ShapeDtypeStruct((TOP_K, n_tok), F32),
                   jax.ShapeDtypeStruct((TOP_K, n_tok), I32),
                   jax.ShapeDtypeStruct(counts_in.shape, F32)],
        scratch_shapes=[pltpu.VMEM(counts_in.shape, F32), pltpu.VMEM((2, bm, D), F32), pltpu.SemaphoreType.DMA((2,))],
        compiler_params=_cparams(("arbitrary", "arbitrary")),
        name="outproj",
    )(hg, da, x, ln_g, ln_b, g1, sc2, sh2, w_out_bf, l1g, l1b, wrh, wrl, br, u, counts_in)


def _dispatch_kernel(dest_ref, h2_ref, xb_in_hbm, xb_hbm, sem, *, bt):
    del xb_in_hbm

    def body(t, c):
        for kk in range(TOP_K):
            pltpu.make_async_copy(h2_ref.at[t], xb_hbm.at[dest_ref[kk, t]], sem).start(priority=kk % 2)
        return c

    lax.fori_loop(0, bt, body, 0, unroll=4)
    for kk in range(TOP_K):
        pltpu.make_async_copy(h2_ref, xb_hbm.at[pl.ds(0, bt)], sem).wait()


def _dispatch(dest, h2, xb):
    n_tok = h2.shape[0]
    bt = TOK_BLOCK
    anyspec = pl.BlockSpec(memory_space=pl.ANY)
    return pl.pallas_call(
        functools.partial(_dispatch_kernel, bt=bt),
        grid=(n_tok // bt,),
        in_specs=[pl.BlockSpec((TOP_K, bt), lambda i: (0, i), memory_space=pltpu.SMEM),
                  pl.BlockSpec((bt,) + ROW_TILE, lambda i: (i, 0, 0)), anyspec],
        out_specs=anyspec,
        out_shape=jax.ShapeDtypeStruct(xb.shape, xb.dtype),
        scratch_shapes=[pltpu.SemaphoreType.DMA(())],
        input_output_aliases={2: 0},
        compiler_params=_cparams(("arbitrary",)),
        name="dispatch",
    )(dest, h2, xb)


def _tile_copies(mat_ref, tiles_hbm, row0, n, sem, to_tiles):
    out = []
    for c in range(ROW_TILE[0]):
        m = mat_ref.at[:, pl.ds(c * HEAD_W, HEAD_W)]
        t = tiles_hbm.at[pl.ds(row0, n), c, :]
        out.append(pltpu.make_async_copy(m, t, sem) if to_tiles else pltpu.make_async_copy(t, m, sem))
    return out


def _experts_kernel(be_ref, nu_ref, xb_hbm, rw_ref, wu_ref, bu_ref, wd_ref, bd_ref, yb_hbm,
                    wu_bf, wd_bf, xbuf, ybuf, sems, *, bm):
    j = pl.program_id(0)
    nb = pl.num_programs(0)
    slot = j % 2
    nu = nu_ref[0]
    active = j < nu
    x_copies = lambda blk, s: _tile_copies(xbuf.at[s], xb_hbm, blk * bm, bm, sems.at[0, s], False)
    y_copies = lambda blk, s: _tile_copies(ybuf.at[s], yb_hbm, blk * bm, bm, sems.at[1, s], True)

    @pl.when(j == 0)
    def _():
        for cp in x_copies(0, 0):
            cp.start()

    @pl.when(j + 1 < nu)
    def _():
        for cp in x_copies(j + 1, 1 - slot):
            cp.start()

    @pl.when(j >= 2)
    def _():
        for cp in y_copies(j - 2, slot):
            cp.wait()

    @pl.when(active & ((j == 0) | (be_ref[j] != be_ref[jnp.maximum(j - 1, 0)])))
    def _():
        wu_bf[...] = wu_ref[...].astype(BF16)
        wd_bf[...] = wd_ref[...].astype(BF16)

    @pl.when(active)
    def _():
        for cp in x_copies(j, slot):
            cp.wait()
        u = _dot(xbuf[slot].astype(BF16), wu_bf[...]) + bu_ref[...]
        glu = jnp.minimum(u[:, :D_FF], SWIGLU_LIMIT)
        lin = jnp.clip(u[:, D_FF:], -SWIGLU_LIMIT, SWIGLU_LIMIT)
        act = glu * _sigmoid(SWIGLU_ALPHA * glu) * (lin + 1.0)
        y = _dot(act.astype(BF16), wd_bf[...]) + bd_ref[...]
        rw = rw_ref[...]
        ybuf[slot] = y * jnp.concatenate([rw] * ROW_TILE[0], axis=1)

    @pl.when(jnp.logical_not(active))
    def _():
        ybuf[slot] = jnp.zeros((bm, D_MODEL), F32)

    for cp in y_copies(j, slot):
        cp.start()

    @pl.when(j == nb - 1)
    def _():
        for cp in y_copies(j - 1, 1 - slot) + y_copies(j, slot):
            cp.wait()


def _experts(blk_exp, n_used, xb, row_w, w_up, b_up, w_down, b_down):
    n_rows = xb.shape[0]
    bm = MOE_BLOCK
    nb = n_rows // bm
    used = lambda j, nu: jnp.minimum(j, nu[0] - 1)
    grid_spec = pltpu.PrefetchScalarGridSpec(
        num_scalar_prefetch=2,
        grid=(nb,),
        in_specs=[pl.BlockSpec(memory_space=pl.ANY),
                  pl.BlockSpec((bm, HEAD_W), lambda j, be, nu: (used(j, nu), 0)),
                  pl.BlockSpec((None, D_MODEL, 2 * D_FF), lambda j, be, nu: (be[j], 0, 0)),
                  pl.BlockSpec((None, 1, 2 * D_FF), lambda j, be, nu: (be[j], 0, 0)),
                  pl.BlockSpec((None, D_FF, D_MODEL), lambda j, be, nu: (be[j], 0, 0)),
                  pl.BlockSpec((None, 1, D_MODEL), lambda j, be, nu: (be[j], 0, 0))],
        out_specs=pl.BlockSpec(memory_space=pl.ANY),
        scratch_shapes=[pltpu.VMEM((D_MODEL, 2 * D_FF), BF16), pltpu.VMEM((D_FF, D_MODEL), BF16),
                        pltpu.VMEM((2, bm, D_MODEL), F32), pltpu.VMEM((2, bm, D_MODEL), F32),
                        pltpu.SemaphoreType.DMA((2, 2))],
    )
    return pl.pallas_call(
        functools.partial(_experts_kernel, bm=bm),
        grid_spec=grid_spec,
        out_shape=jax.ShapeDtypeStruct((n_rows,) + ROW_TILE, F32),
        compiler_params=_cparams(("arbitrary",)),
        name="experts",
    )(blk_exp, n_used, xb, row_w, w_up, b_up.reshape(N_EXPERTS, 1, 2 * D_FF), w_down,
      b_down.reshape(N_EXPERTS, 1, D_MODEL))


def _combine_kernel(dcur_ref, dnext_ref, x1_ref, g2_ref, lg_ref, lb_ref, yb_hbm, o_ref, buf, ff_ref, sems, *, bm):
    i = pl.program_id(0) * pl.num_programs(1) + pl.program_id(1)
    n = pl.num_programs(0) * pl.num_programs(1)
    slot = i % 2

    def issue(dref, s):
        def body(t, c):
            for kk in range(TOP_K):
                pltpu.make_async_copy(yb_hbm.at[dref[kk, t]], buf.at[s, kk, t], sems.at[s]).start(priority=kk % 2)
            return c
        lax.fori_loop(0, bm, body, 0, unroll=4)

    @pl.when(i == 0)
    def _():
        issue(dcur_ref, 0)

    @pl.when(i + 1 < n)
    def _():
        issue(dnext_ref, 1 - slot)

    for kk in range(TOP_K):
        pltpu.make_async_copy(yb_hbm.at[pl.ds(0, bm)], buf.at[slot, kk], sems.at[slot]).wait()

    ff_ref[...] = (buf[slot, 0] + buf[slot, 1]) + (buf[slot, 2] + buf[slot, 3])
    ff = _load_row_tiles(ff_ref)
    o_ref[...] = _layer_norm(DEEPNORM_ALPHA * x1_ref[...] + (1.0 + g2_ref[...]) * ff, lg_ref[...], lb_ref[...])


def _combine(dest, x1, g2, ln_g, ln_b, yb, bm):
    G, R, D = x1.shape
    nb = R // bm
    n_blk = G * nb
    row = lambda w: pl.BlockSpec((None, bm, w), lambda g, i: (g, i, 0))
    full = lambda a: pl.BlockSpec(a.shape, lambda g, i: (0,) * a.ndim)
    cur = pl.BlockSpec((TOP_K, bm), lambda g, i: (0, g * nb + i), memory_space=pltpu.SMEM)
    nxt = pl.BlockSpec((TOP_K, bm), lambda g, i: (0, jnp.minimum(g * nb + i + 1, n_blk - 1)),
                       memory_space=pltpu.SMEM)
    return pl.pallas_call(
        functools.partial(_combine_kernel, bm=bm),
        grid=(G, nb),
        in_specs=[cur, nxt, row(D), _mod_spec(g2, bm), full(ln_g), full(ln_b), pl.BlockSpec(memory_space=pl.ANY)],
        out_specs=row(D),
        out_shape=jax.ShapeDtypeStruct((G, R, D), F32),
        scratch_shapes=[pltpu.VMEM((2, TOP_K, bm) + ROW_TILE, F32), pltpu.VMEM((bm,) + ROW_TILE, F32),
                        pltpu.SemaphoreType.DMA((2,))],
        compiler_params=_cparams(("arbitrary", "arbitrary")),
        name="combine",
    )(dest, dest, x1, g2, ln_g, ln_b, yb)


def kernel(x_prompt, x_sample, c_prompt, c_sample, cache_k, cache_v, state_hgrn, page_table, ln_in_g, ln_in_b, w_ada, b_ada, w_in, hg_lb, hg_norm_g, da_lq1, da_lk1, da_lq2, da_lk2, da_subln_g, w_out, ln1_g, ln1_b, w_router, b_router, w_up, b_up, w_down, b_down, ln2_g, ln2_b):
    assert w_in.shape[0] == 1, "single-layer trunk"
    B, T, D = x_prompt.shape
    NS = x_sample.shape[0]
    n_prompt = B * T
    n_tot = n_prompt + NS
    r1 = lambda a: a.reshape(1, -1)

    lb = r1(jax.nn.softmax(hg_lb.astype(F32), axis=0)[0])
    lam = (jnp.exp(jnp.sum(da_lq1[0].astype(F32) * da_lk1[0].astype(F32)))
           - jnp.exp(jnp.sum(da_lq2[0].astype(F32) * da_lk2[0].astype(F32))) + LAM_INIT)
    lam_row = jnp.full((1, HEAD_W), lam, F32)
    ln_g, ln_b = r1(ln_in_g), r1(ln_in_b)
    w_in_bf = w_in[0].astype(BF16)
    wqt = w_in_bf[:, HG_COLS:HG_COLS + GROUP_W].T
    wvt = w_in_bf[:, HG_COLS + 2 * GROUP_W:].T
    w_out_bf = w_out[0].astype(BF16)
    wr_t = w_router[0].T
    wrh = wr_t.astype(BF16)
    wrl = (wr_t - wrh.astype(F32)).astype(BF16)
    br = b_router[0].reshape(N_EXPERTS, 1)
    norm_g, subln_g = r1(hg_norm_g[0]), r1(da_subln_g[0])

    n_c = B + NS
    c_all = jnp.concatenate([c_prompt, c_sample, jnp.zeros((-n_c % 8, D), F32)], axis=0)
    mod = _ada(c_all, w_ada[0], b_ada[0])
    mod_p = [mod[:B, j * D:(j + 1) * D].reshape(B, 1, D) for j in range(6)]
    mod_s = [mod[B:n_c, j * D:(j + 1) * D].reshape(1, NS, D) for j in range(6)]

    zhg_p, kf_p, vf_p, kb_p, qt_p, vt_p = _inproj(x_prompt, mod_p[1], mod_p[0], ln_g, ln_b, w_in_bf, INPROJ_BLOCK,
                                                  (wqt, wvt, ATTN_BLOCK))
    hg_p, s_p = _hgrn_prompt(zhg_p, lb, norm_g)
    da_p = _attn_prompt(qt_p, kb_p, vt_p, lam, subln_g)

    xs = x_sample.reshape(1, NS, D)
    zhg_s, kf_s, vf_s, q_s = _inproj(xs, mod_s[1], mod_s[0], ln_g, ln_b, w_in_bf, NS)
    hg_s, s_s = _hgrn_sample(zhg_s[0], state_hgrn[0], lb, norm_g)
    da_s = _attn_sample(q_s[0], kf_s.reshape(NS, GROUP_W), vf_s.reshape(NS, GROUP_W), cache_k, cache_v, page_table,
                        lam_row, subln_g)

    consts = (ln_g, ln_b, w_out_bf, r1(ln1_g[0]), r1(ln1_b[0]), wrh, wrl, br)
    counts0 = jnp.zeros((N_EXPERTS, 1), F32)
    x1_p, h2_p, idx_p, tw_p, rank_p, counts1 = _outproj(
        hg_p, da_p, x_prompt, (mod_p[2], mod_p[4], mod_p[3]), consts, counts0, ROW_BLOCK)
    x1_s, h2_s, idx_s, tw_s, rank_s, counts = _outproj(
        hg_s.reshape(1, NS, GROUP_W), da_s.reshape(1, NS, GROUP_W), xs, (mod_s[2], mod_s[4], mod_s[3]), consts,
        counts1, NS)

    cnt = counts[:, 0].astype(I32)
    padded = (cnt + MOE_BLOCK - 1) // MOE_BLOCK * MOE_BLOCK
    pad_end = jnp.cumsum(padded)
    pad_start = pad_end - padded
    e_ids = jnp.arange(N_EXPERTS, dtype=I32)

    def slot_of(idx, rank):
        return jnp.sum(jnp.where(idx[..., None] == e_ids, pad_start, 0), axis=-1) + rank

    dest_p = slot_of(idx_p, rank_p)
    dest_s = slot_of(idx_s, rank_s)
    n_blocks = -(-n_tot * TOP_K // MOE_BLOCK) + N_EXPERTS
    blk_row0 = jnp.arange(n_blocks, dtype=I32) * MOE_BLOCK
    blk_exp = jnp.minimum(jnp.sum((pad_end[None, :] <= blk_row0[:, None]).astype(I32), axis=1), N_EXPERTS - 1)
    n_used = (pad_end[-1:] // MOE_BLOCK).astype(I32)

    n_rows = n_blocks * MOE_BLOCK
    dest_all = jnp.concatenate([dest_p, dest_s], axis=1).reshape(-1)
    tw_all = jnp.concatenate([tw_p, tw_s], axis=1).reshape(-1)
    row_w = jnp.zeros((n_rows,), F32).at[dest_all].add(tw_all, unique_indices=True)
    row_w = jnp.broadcast_to(row_w[:, None], (n_rows, HEAD_W))

    xb = jnp.zeros((n_rows,) + ROW_TILE, F32)
    xb = _dispatch(dest_p, h2_p, xb)
    xb = _dispatch(dest_s, h2_s, xb)
    yb = _experts(blk_exp, n_used, xb, row_w, w_up[0], b_up[0], w_down[0], b_down[0])

    l2g, l2b = r1(ln2_g[0]), r1(ln2_b[0])
    y_p = _combine(dest_p, x1_p, mod_p[5], l2g, l2b, yb, TOK_BLOCK)
    y_s = _combine(dest_s, x1_s, mod_s[5], l2g, l2b, yb, TOK_BLOCK)

    smp = lambda a: a.reshape(1, NS, 1, HEADS, HEAD_W)
    return (y_p, y_s.reshape(NS, 1, D), kf_p[None], vf_p[None], s_p[None], smp(kf_s), smp(vf_s), s_s[None])
```

```python
import functools
import math

import numpy as np
import jax
import jax.numpy as jnp
from jax import lax
from jax.experimental import pallas as pl
from jax.experimental.pallas import tpu as pltpu

F32, BF16, I32, U32 = jnp.float32, jnp.bfloat16, jnp.int32, jnp.uint32

D_MODEL = 1024
HEADS = 4
HEAD_W = 128
DA_HEAD = 64
GROUP_W = HEADS * HEAD_W
HG_COLS = 4 * GROUP_W
IN_WIDTH = HG_COLS + 3 * GROUP_W
N_EXPERTS = 32
TOP_K = 4
D_FF = 1024
SWIGLU_ALPHA = 1.702
SWIGLU_LIMIT = 7.0
DEEPNORM_ALPHA = 2.0 ** 0.25
LN_EPS = 1e-5
RMS_EPS = 1e-6
LAM_INIT = 0.8 - 0.6 * math.exp(-0.3 * 0)
LOG2E = math.log2(math.e)
ALIBI_SLOPES = tuple(2.0 ** (-8.0 * (h + 1) / HEADS) for h in range(HEADS))
PAGE_SIZE = 128
EXP_CLAMP = 80.0

VMEM_LIMIT = 56 * 1024 * 1024
MOE_BLOCK = 256
HG_CHUNK = 128
ATTN_BLOCK = 1024
INPROJ_BLOCK = 512
ROW_BLOCK = 512
TOK_BLOCK = 128


def _cparams(sem):
    return pltpu.CompilerParams(dimension_semantics=sem, vmem_limit_bytes=VMEM_LIMIT)


def _sigmoid(x):
    return 1.0 / (1.0 + jnp.exp(-x))


def _layer_norm(x, g, b):
    mu = jnp.mean(x, -1, keepdims=True)
    xc = x - mu
    var = jnp.mean(xc * xc, -1, keepdims=True)
    return xc * lax.rsqrt(var + LN_EPS) * g + b


def _dot(a, b):
    return jnp.dot(a, b, preferred_element_type=F32)


def _dot_nt(a, b):
    return lax.dot_general(a, b, (((1,), (1,)), ((), ())), preferred_element_type=F32)


def _dot_tn(a, b):
    return lax.dot_general(a, b, (((0,), (0,)), ((), ())), preferred_element_type=F32)


ROW_SHAPE = (1, D_MODEL)


def _rows_copy(mat_ref, rows_hbm, row0, n, sem, to_hbm):
    rows = rows_hbm.at[pl.ds(row0, n), 0, :]
    return pltpu.make_async_copy(mat_ref, rows, sem) if to_hbm else pltpu.make_async_copy(rows, mat_ref, sem)


def _ada_kernel(c_ref, w_ref, b_ref, o_ref):
    c = c_ref[...]
    a = (c * _sigmoid(c)).astype(BF16)
    o_ref[...] = _dot(a, w_ref[...].astype(BF16)) + b_ref[...]


def _ada(c, w_ada, b_ada):
    rows, d = c.shape
    n = w_ada.shape[1]
    bn = 1536
    return pl.pallas_call(
        _ada_kernel,
        grid=(n // bn,),
        in_specs=[pl.BlockSpec((rows, d), lambda j: (0, 0)),
                  pl.BlockSpec((d, bn), lambda j: (0, j)),
                  pl.BlockSpec((1, bn), lambda j: (0, j))],
        out_specs=pl.BlockSpec((rows, bn), lambda j: (0, j)),
        out_shape=jax.ShapeDtypeStruct((rows, n), F32),
        compiler_params=_cparams(("arbitrary",)),
        name="ada",
    )(c, w_ada, b_ada.reshape(1, n))


def _inproj_common(x_ref, g_ref, b_ref, sc_ref, sh_ref, w_ref, zhg_ref, kf_ref, vf_ref):
    x0 = _layer_norm(x_ref[...], g_ref[...], b_ref[...])
    h = (x0 * (1.0 + sc_ref[...]) + sh_ref[...]).astype(BF16)
    zhg_ref[...] = _dot(h, w_ref[:, 0:HG_COLS])
    c0 = HG_COLS + GROUP_W
    k = _dot(h, w_ref[:, c0:c0 + GROUP_W])
    v = _dot(h, w_ref[:, c0 + GROUP_W:c0 + 2 * GROUP_W])
    for hd in range(HEADS):
        kf_ref[:, hd, :] = k[:, hd * HEAD_W:(hd + 1) * HEAD_W]
        vf_ref[:, hd, :] = v[:, hd * HEAD_W:(hd + 1) * HEAD_W]
    return h, k


def _inproj_prompt_kernel(x_ref, g_ref, b_ref, sc_ref, sh_ref, w_ref, wqt_ref, wvt_ref,
                          zhg_ref, kf_ref, vf_ref, kb_ref, qt_ref, vt_ref):
    h, k = _inproj_common(x_ref, g_ref, b_ref, sc_ref, sh_ref, w_ref, zhg_ref, kf_ref, vf_ref)
    kb_ref[...] = k.astype(BF16)
    qt_ref[...] = (_dot_nt(wqt_ref[...], h) * (DA_HEAD ** -0.5 * LOG2E)).astype(BF16)
    vt_ref[...] = _dot_nt(wvt_ref[...], h).astype(BF16)


def _inproj_sample_kernel(x_ref, g_ref, b_ref, sc_ref, sh_ref, w_ref, zhg_ref, kf_ref, vf_ref, q_ref):
    h, _ = _inproj_common(x_ref, g_ref, b_ref, sc_ref, sh_ref, w_ref, zhg_ref, kf_ref, vf_ref)
    q_ref[...] = (_dot(h, w_ref[:, HG_COLS:HG_COLS + GROUP_W]) * (DA_HEAD ** -0.5)).astype(BF16)


def _mod_spec(mod, bm):
    if mod.shape[1] == 1:
        return pl.BlockSpec((None, 1, mod.shape[2]), lambda g, i: (g, 0, 0))
    return pl.BlockSpec((None, bm, mod.shape[2]), lambda g, i: (g, i, 0))


def _inproj(x, sc, sh, ln_g, ln_b, w_in_bf, bm, transposed=None):
    G, R, D = x.shape
    nb = R // bm
    row = lambda w: pl.BlockSpec((None, bm, w), lambda g, i: (g, i, 0))
    full = lambda a: pl.BlockSpec(a.shape, lambda g, i: (0,) * a.ndim)
    heads = pl.BlockSpec((None, bm, HEADS, HEAD_W), lambda g, i: (g, i, 0, 0))
    heads_shape = jax.ShapeDtypeStruct((G, R, HEADS, HEAD_W), F32)
    sds = lambda w, dt: jax.ShapeDtypeStruct((G, R, w), dt)
    args = [x, ln_g, ln_b, sc, sh, w_in_bf]
    in_specs = [row(D), full(ln_g), full(ln_b), _mod_spec(sc, bm), _mod_spec(sh, bm), full(w_in_bf)]
    out_specs = [row(HG_COLS), heads, heads, row(GROUP_W)]
    out_shape = [sds(HG_COLS, F32), heads_shape, heads_shape, sds(GROUP_W, BF16)]
    body = _inproj_sample_kernel
    if transposed is not None:
        wqt, wvt, tblk = transposed
        per = tblk // bm
        tr = pl.BlockSpec((None, None, GROUP_W, bm), lambda g, i: (g, i // per, 0, i % per))
        tr_shape = jax.ShapeDtypeStruct((G, R // tblk, GROUP_W, tblk), BF16)
        args += [wqt, wvt]
        in_specs += [full(wqt), full(wvt)]
        out_specs += [tr, tr]
        out_shape += [tr_shape, tr_shape]
        body = _inproj_prompt_kernel
    return pl.pallas_call(
        body,
        grid=(G, nb),
        in_specs=in_specs,
        out_specs=out_specs,
        out_shape=out_shape,
        compiler_params=_cparams(("arbitrary", "arbitrary")),
        name="inproj",
    )(*args)


def _hgrn_gates(zq, zf, lb):
    q = zq * _sigmoid(zq)
    f = lb + (1.0 - lb) * _sigmoid(zf)
    k = (1.0 - lb) * _sigmoid(-zf)
    return q, jnp.log(f), k


def _hgrn_kernel(z_ref, lb_ref, ng_ref, lvl_ref, tri_ref, o_ref, sfin_ref, st_ref, *, C):
    t = pl.program_id(1)

    @pl.when(t == 0)
    def _():
        st_ref[...] = jnp.zeros_like(st_ref)

    lvl = lvl_ref[...]
    tri = tri_ref[...]
    n_levels = int(math.log2(C)) - 3
    ng = ng_ref[...]
    for h in range(HEADS):
        cs = slice(h * HEAD_W, (h + 1) * HEAD_W)
        zq = z_ref[:, h * HEAD_W:(h + 1) * HEAD_W]
        zf = z_ref[:, GROUP_W + h * HEAD_W:GROUP_W + (h + 1) * HEAD_W]
        v = z_ref[:, 2 * GROUP_W + h * HEAD_W:2 * GROUP_W + (h + 1) * HEAD_W].astype(BF16)
        zg = z_ref[:, 3 * GROUP_W + h * HEAD_W:3 * GROUP_W + (h + 1) * HEAD_W]
        q, g, k = _hgrn_gates(zq, zf, lb_ref[:, cs])
        g1 = g.astype(BF16)
        r1 = g - g1.astype(F32)
        g2 = r1.astype(BF16)
        g3 = (r1 - g2.astype(F32)).astype(BF16)
        b = _dot(tri, g1) + _dot(tri, g2) + _dot(tri, g3)

        b8 = b.reshape(C // 8, 8, HEAD_W)
        bmid = jnp.broadcast_to(b8[:, 3:4, :], b8.shape).reshape(C, HEAD_W)
        e = jnp.clip(b - bmid, -EXP_CLAMP, EXP_CLAMP)
        a = jnp.where(lvl == 0, _dot_nt((q * jnp.exp(e)).astype(BF16), (k * jnp.exp(-e)).astype(BF16)), 0.0)
        for li in range(1, n_levels + 1):
            m = 4 << li
            bb = b.reshape(C // (2 * m), 2 * m, HEAD_W)
            d = b - jnp.broadcast_to(bb[:, m - 1:m, :], bb.shape).reshape(C, HEAD_W)
            qs = (q * jnp.exp(jnp.minimum(d, 0.0))).astype(BF16)
            ks = (k * jnp.exp(jnp.minimum(-d, 0.0))).astype(BF16)
            a = jnp.where(lvl == li, _dot_nt(qs, ks), a)

        st = st_ref[h]
        o = _dot(a.astype(BF16), v) + _dot_nt((q * jnp.exp(b)).astype(BF16), st.astype(BF16))
        b_last = b[C - 1:C, :]
        kd = (k * jnp.exp(b_last - b)).astype(BF16)
        st_ref[h] = st * jnp.exp(b_last) + _dot_tn(v, kd)

        ms = jnp.mean(o * o, -1, keepdims=True)
        o_ref[:, cs] = (o * lax.rsqrt(ms + RMS_EPS) * ng * (zg * _sigmoid(zg))).astype(BF16)

    @pl.when(t == pl.num_programs(1) - 1)
    def _():
        for h in range(HEADS):
            sfin_ref[h] = st_ref[h].T


def _hgrn_level_table(C):
    t = np.arange(C)[:, None]
    s = np.arange(C)[None, :]
    x = t ^ s
    lvl = np.zeros((C, C), np.int32)
    m = 8
    while m < C:
        lvl += (x >= m).astype(np.int32)
        m *= 2
    return np.where(s <= t, lvl, -1).astype(np.int32)


def _hgrn_prompt(zhg, lb, norm_g):
    B, T, _ = zhg.shape
    C = HG_CHUNK
    lvl = jnp.asarray(_hgrn_level_table(C))
    tri = jnp.asarray(np.tril(np.ones((C, C), np.float32)), BF16)
    full = lambda a: pl.BlockSpec(a.shape, lambda b, t: (0,) * a.ndim)
    return pl.pallas_call(
        functools.partial(_hgrn_kernel, C=C),
        grid=(B, T // C),
        in_specs=[pl.BlockSpec((None, C, HG_COLS), lambda b, t: (b, t, 0)), full(lb), full(norm_g), full(lvl), full(tri)],
        out_specs=[pl.BlockSpec((None, C, GROUP_W), lambda b, t: (b, t, 0)),
                   pl.BlockSpec((None, HEADS, HEAD_W, HEAD_W), lambda b, t: (b, 0, 0, 0))],
        out_shape=[jax.ShapeDtypeStruct((B, T, GROUP_W), BF16),
                   jax.ShapeDtypeStruct((B, HEADS, HEAD_W, HEAD_W), F32)],
        scratch_shapes=[pltpu.VMEM((HEADS, HEAD_W, HEAD_W), F32)],
        compiler_params=_cparams(("arbitrary", "arbitrary")),
        name="hgrn_prompt",
    )(zhg, lb, norm_g, lvl, tri)


def _hgrn_step_kernel(z_ref, s_ref, lb_ref, ng_ref, o_ref, so_ref, *, G):
    ng = ng_ref[...]
    for h in range(HEADS):
        cs = slice(h * HEAD_W, (h + 1) * HEAD_W)
        zq = z_ref[:, h * HEAD_W:(h + 1) * HEAD_W]
        zf = z_ref[:, GROUP_W + h * HEAD_W:GROUP_W + (h + 1) * HEAD_W]
        v = z_ref[:, 2 * GROUP_W + h * HEAD_W:2 * GROUP_W + (h + 1) * HEAD_W]
        zg = z_ref[:, 3 * GROUP_W + h * HEAD_W:3 * GROUP_W + (h + 1) * HEAD_W]
        lb = lb_ref[:, cs]
        q = zq * _sigmoid(zq)
        f = lb + (1.0 - lb) * _sigmoid(zf)
        k = (1.0 - lb) * _sigmoid(-zf)
        qT, fT, kT = q.T, f.T, k.T
        rows = []
        for j in range(G):
            s_new = fT[:, j:j + 1] * s_ref[j, h] + kT[:, j:j + 1] * v[j:j + 1, :]
            so_ref[j, h] = s_new
            rows.append(jnp.sum(s_new * qT[:, j:j + 1], axis=0, keepdims=True))
        o = jnp.concatenate(rows, axis=0)
        ms = jnp.mean(o * o, -1, keepdims=True)
        o_ref[:, cs] = (o * lax.rsqrt(ms + RMS_EPS) * ng * (zg * _sigmoid(zg))).astype(BF16)


def _hgrn_sample(zhg, state, lb, norm_g):
    N = zhg.shape[0]
    G = 8
    full = lambda a: pl.BlockSpec(a.shape, lambda i: (0,) * a.ndim)
    st_spec = pl.BlockSpec((G, HEADS, HEAD_W, HEAD_W), lambda i: (i, 0, 0, 0))
    return pl.pallas_call(
        functools.partial(_hgrn_step_kernel, G=G),
        grid=(N // G,),
        in_specs=[pl.BlockSpec((G, HG_COLS), lambda i: (i, 0)), st_spec, full(lb), full(norm_g)],
        out_specs=[pl.BlockSpec((G, GROUP_W), lambda i: (i, 0)), st_spec],
        out_shape=[jax.ShapeDtypeStruct((N, GROUP_W), BF16), jax.ShapeDtypeStruct(state.shape, F32)],
        compiler_params=_cparams(("arbitrary",)),
        name="hgrn_sample",
    )(zhg, state, lb, norm_g)


def _attn_kernel(qt_ref, k_ref, vt_ref, sl_ref, feat_ref, lam_ref, gcol_ref, o_ref, m_ref, l_ref, acc_ref, *, blk):
    qi = pl.program_id(2)
    row = lax.broadcasted_iota(I32, (HEAD_W, blk), 0)
    qt = qt_ref[...].astype(F32)
    a_hi = sl_ref[0:1, :]
    a_lo = sl_ref[1:2, :]
    slope2 = sl_ref[2:3, :]
    def slope_rows(r0):
        in_rows = (row >= r0) & (row < r0 + 4)
        return jnp.where(in_rows, jnp.where((row - r0) % 2 == 0, a_hi, a_lo), 0.0)

    q_aug = [jnp.where(row < DA_HEAD, qt, slope_rows(DA_HEAD)).astype(BF16),
             jnp.where(row >= DA_HEAD, qt, slope_rows(0)).astype(BF16)]
    keep = [feat_ref[0] > 0, feat_ref[1] > 0]
    feats = [feat_ref[2], feat_ref[3]]
    m_ref[...] = jnp.full_like(m_ref, -jnp.inf)
    l_ref[...] = jnp.zeros_like(l_ref)
    acc_ref[...] = jnp.zeros_like(acc_ref)

    def block(kj, masked):
        k_start = pl.multiple_of(kj * blk, blk)
        kb = k_ref[pl.ds(k_start, blk), :]
        vt = vt_ref[kj]
        off = slope2 * jnp.full((1, blk), k_start - qi * blk, I32).astype(F32)
        if masked:
            ok = lax.broadcasted_iota(I32, (blk, 1), 0) <= lax.broadcasted_iota(I32, (1, blk), 1)
        for mi in range(2):
            st = _dot(jnp.where(keep[mi], kb, feats[mi]), q_aug[mi])
            if masked:
                st = jnp.where(ok, st, -jnp.inf)
            m_prev = m_ref[mi]
            m_new = jnp.maximum(m_prev, jnp.max(st, axis=0, keepdims=True) + off)
            p = jnp.exp2(st - (m_new - off))
            alpha = jnp.exp2(m_prev - m_new)
            l_ref[mi] = alpha * l_ref[mi] + jnp.sum(p, axis=0, keepdims=True)
            acc_ref[mi] = alpha * acc_ref[mi] + _dot(vt, p.astype(BF16))
            m_ref[mi] = m_new

    block(qi, True)

    def body(kj, c):
        block(kj, False)
        return c

    lax.fori_loop(0, qi, body, 0)

    ot = acc_ref[0] * (1.0 / l_ref[0]) - lam_ref[...] * (acc_ref[1] * (1.0 / l_ref[1]))
    ms = jnp.mean(ot * ot, axis=0, keepdims=True)
    o_ref[...] = (ot * lax.rsqrt(ms + RMS_EPS) * gcol_ref[...]).T.astype(BF16)


def _attn_prompt(qt, k, vt, lam, subln_g):
    B, nblk, _, blk = qt.shape
    T = k.shape[1]
    bf = lambda x: np.asarray(x, np.float32).astype(BF16).astype(np.float64)
    a = np.asarray(ALIBI_SLOPES, np.float64) * LOG2E
    a_hi = bf(a)
    a_lo = bf(a - a_hi)
    sl = np.zeros((HEADS, 8, blk), np.float32)
    sl[:, 0, :], sl[:, 1, :], sl[:, 2, :] = a_hi[:, None], a_lo[:, None], a.astype(np.float32)[:, None]
    lane = np.arange(HEAD_W)[None, :]
    r = np.arange(blk)[:, None]
    assert blk <= 256 * 256

    def pos_feat(l0):
        return (np.where((lane == l0) | (lane == l0 + 1), r // 256 * 256, 0)
                + np.where((lane == l0 + 2) | (lane == l0 + 3), r % 256, 0)).astype(np.float32)

    feat = np.stack([np.broadcast_to(lane < DA_HEAD, (blk, HEAD_W)).astype(np.float32),
                     np.broadcast_to(lane >= DA_HEAD, (blk, HEAD_W)).astype(np.float32),
                     pos_feat(DA_HEAD), pos_feat(0)])
    feat = jnp.asarray(feat, BF16)
    lam_row = jnp.full((1, blk), lam, F32)
    gcol = jnp.broadcast_to((subln_g.reshape(HEAD_W, 1) * (1.0 - LAM_INIT)), (HEAD_W, blk))
    full = lambda x: pl.BlockSpec(x.shape, lambda b, h, i: (0,) * x.ndim)
    return pl.pallas_call(
        functools.partial(_attn_kernel, blk=blk),
        grid=(B, HEADS, nblk),
        in_specs=[pl.BlockSpec((None, None, HEAD_W, blk), lambda b, h, i: (b, i, h, 0)),
                  pl.BlockSpec((None, T, HEAD_W), lambda b, h, i: (b, 0, h)),
                  pl.BlockSpec((None, nblk, HEAD_W, blk), lambda b, h, i: (b, 0, h, 0)),
                  pl.BlockSpec((None, 8, blk), lambda b, h, i: (h, 0, 0)),
                  full(feat), full(lam_row), full(gcol)],
        out_specs=pl.BlockSpec((None, blk, HEAD_W), lambda b, h, i: (b, i, h)),
        out_shape=jax.ShapeDtypeStruct((B, T, GROUP_W), BF16),
        scratch_shapes=[pltpu.VMEM((2, 1, blk), F32), pltpu.VMEM((2, 1, blk), F32), pltpu.VMEM((2, HEAD_W, blk), F32)],
        compiler_params=_cparams(("arbitrary", "arbitrary", "arbitrary")),
        name="attn_prompt",
    )(qt, k, vt, jnp.asarray(sl), feat, lam_row, gcol)


def _attn_decode_kernel(pt_ref, q_ref, kn_ref, vn_ref, bias_ref, lam_ref, g_ref, ck_hbm, cv_hbm, o_ref,
                        kbuf, vbuf, sems, *, n_pages):
    i = pl.program_id(0)
    slot = i % 2

    def page_copies(seq, s):
        for pg in range(n_pages):
            page = pt_ref[seq, pg]
            for h in range(HEADS):
                dst = (s, h, pl.ds(pg * PAGE_SIZE, PAGE_SIZE))
                yield pltpu.make_async_copy(ck_hbm.at[0, page, :, h, :], kbuf.at[dst], sems.at[0, s])
                yield pltpu.make_async_copy(cv_hbm.at[0, page, :, h, :], vbuf.at[dst], sems.at[1, s])

    @pl.when(i == 0)
    def _():
        for cp in page_copies(0, 0):
            cp.start()

    @pl.when(i + 1 < pl.num_programs(0))
    def _():
        for cp in page_copies(i + 1, 1 - slot):
            cp.start()

    for cp in page_copies(i, slot):
        cp.wait()

    r8 = lax.broadcasted_iota(I32, (8, GROUP_W), 0)
    c8 = lax.broadcasted_iota(I32, (8, GROUP_W), 1)
    qmat = jnp.where(c8 // DA_HEAD == r8, jnp.broadcast_to(q_ref[...].astype(F32), (8, GROUP_W)), 0.0)
    qmat_bf = qmat.astype(BF16)
    s = bias_ref[...]
    for h in range(HEADS):
        s = s + _dot_nt(qmat_bf[:, h * HEAD_W:(h + 1) * HEAD_W], kbuf[slot, h].astype(BF16))
    kn = kn_ref[...].astype(BF16).astype(F32)
    s_self = jnp.sum(qmat_bf.astype(F32) * kn, axis=-1, keepdims=True)
    m = jnp.maximum(jnp.max(s, -1, keepdims=True), s_self)
    p = jnp.exp(s - m)
    p_self = jnp.exp(s_self - m)
    inv_l = 1.0 / (jnp.sum(p, -1, keepdims=True) + p_self)
    coef = jnp.where(lax.broadcasted_iota(I32, (8, 1), 0) % 2 == 0, 1.0, -lam_ref[:, 0:1]) * inv_l
    w = (p * coef).astype(BF16)
    accs = [_dot(w, vbuf[slot, h].astype(BF16)) for h in range(HEADS)]
    acc = jnp.concatenate(accs, axis=1) + (p_self * coef) * vn_ref[...]
    o = jnp.sum(jnp.where(c8 // HEAD_W == r8 // 2, acc, 0.0), axis=0, keepdims=True)
    outs = []
    for h in range(HEADS):
        oh = o[:, h * HEAD_W:(h + 1) * HEAD_W]
        ms = jnp.mean(oh * oh, -1, keepdims=True)
        outs.append(oh * lax.rsqrt(ms + RMS_EPS) * g_ref[...] * (1.0 - LAM_INIT))
    o_ref[...] = jnp.concatenate(outs, axis=1).astype(BF16)


def _attn_sample(q, k_new, v_new, cache_k, cache_v, page_table, lam_row, subln_g):
    N, n_pages = page_table.shape
    past = n_pages * PAGE_SIZE
    kpos = np.arange(past, dtype=np.float32)[None, :]
    slope_rows = np.repeat(np.asarray(ALIBI_SLOPES, np.float32), 2)[:, None]
    bias = jnp.asarray(-slope_rows * (past - kpos))
    row = pl.BlockSpec((None, 1, GROUP_W), lambda i, pt: (i, 0, 0))
    full = lambda a: pl.BlockSpec(a.shape, lambda i, pt: (0,) * a.ndim)
    anyspec = pl.BlockSpec(memory_space=pl.ANY)
    grid_spec = pltpu.PrefetchScalarGridSpec(
        num_scalar_prefetch=1,
        grid=(N,),
        in_specs=[row, row, row, full(bias), full(lam_row), full(subln_g), anyspec, anyspec],
        out_specs=row,
        scratch_shapes=[pltpu.VMEM((2, HEADS, past, HEAD_W), F32), pltpu.VMEM((2, HEADS, past, HEAD_W), F32),
                        pltpu.SemaphoreType.DMA((2, 2))],
    )
    r3 = lambda a: a.reshape(N, 1, GROUP_W)
    out = pl.pallas_call(
        functools.partial(_attn_decode_kernel, n_pages=n_pages),
        grid_spec=grid_spec,
        out_shape=jax.ShapeDtypeStruct((N, 1, GROUP_W), BF16),
        compiler_params=_cparams(("arbitrary",)),
        name="attn_sample",
    )(page_table, r3(q), r3(k_new), r3(v_new), bias, lam_row, subln_g, cache_k, cache_v)
    return out.reshape(N, GROUP_W)


def _outproj_kernel(hg_ref, da_ref, x_ref, lg_ref, lb_ref, g1_ref, sc2_ref, sh2_ref, w_ref, l1g_ref, l1b_ref,
                    wrh_ref, wrl_ref, br_ref, u_ref, cin_ref,
                    x1_ref, h2_hbm, idx_ref, tw_ref, rank_ref, cout_ref, run_ref, hbuf, hsem, *, n_steps):
    step = pl.program_id(0) * pl.num_programs(1) + pl.program_id(1)
    slot = step % 2
    bm = x_ref.shape[0]
    h2_copies = lambda st, s: [_rows_copy(hbuf.at[s], h2_hbm, st * bm, bm, hsem.at[s], True)]

    @pl.when(step == 0)
    def _():
        run_ref[...] = cin_ref[...]

    x0 = _layer_norm(x_ref[...], lg_ref[...], lb_ref[...])
    mix = _dot(hg_ref[...], w_ref[0:GROUP_W, :]) + _dot(da_ref[...], w_ref[GROUP_W:2 * GROUP_W, :])
    x1 = _layer_norm(DEEPNORM_ALPHA * x0 + (1.0 + g1_ref[...]) * mix, l1g_ref[...], l1b_ref[...])
    x1_ref[...] = x1
    h2 = x1 * (1.0 + sc2_ref[...]) + sh2_ref[...]

    @pl.when(step >= 2)
    def _():
        for cp in h2_copies(step - 2, slot):
            cp.wait()

    hbuf[slot] = h2
    for cp in h2_copies(step, slot):
        cp.start()

    @pl.when(step == n_steps - 1)
    def _():
        tail = h2_copies(step, slot)
        if n_steps > 1:
            tail = h2_copies(step - 1, 1 - slot) + tail
        for cp in tail:
            cp.wait()

    hi = h2.astype(BF16)
    lo = (h2 - hi.astype(F32)).astype(BF16)
    wrh = wrh_ref[...]
    logits = _dot_nt(wrh, hi) + _dot_nt(wrh, lo) + _dot_nt(wrl_ref[...], hi) + br_ref[...]

    n_e, bm = logits.shape
    rows = lax.broadcasted_iota(I32, (n_e, bm), 0).astype(F32)
    vals, sels = [], []
    work = logits
    for kk in range(TOP_K):
        mx = jnp.max(work, axis=0, keepdims=True)
        ix = jnp.min(jnp.where(work == mx, rows, float(n_e)), axis=0, keepdims=True)
        sel = rows == ix
        idx_ref[kk:kk + 1, :] = ix.astype(I32)
        vals.append(mx)
        sels.append(sel)
        work = jnp.where(sel, -jnp.inf, work)
    es = [jnp.exp(vv - vals[0]) for vv in vals]
    inv = 1.0 / (es[0] + es[1] + es[2] + es[3])
    for kk in range(TOP_K):
        tw_ref[kk:kk + 1, :] = es[kk] * inv

    base = run_ref[...]
    for kk in range(TOP_K):
        oh = jnp.where(sels[kk], 1.0, 0.0)
        before = base + _dot(oh.astype(BF16), u_ref[...])
        rank_ref[kk:kk + 1, :] = jnp.sum(jnp.where(sels[kk], before, 0.0), axis=0, keepdims=True).astype(I32)
        base = base + jnp.sum(oh, axis=1, keepdims=True)
    run_ref[...] = base
    cout_ref[...] = base


def _outproj(hg, da, x, mods, consts, counts_in, bm):
    G, R, D = x.shape
    g1, sc2, sh2 = mods
    ln_g, ln_b, w_out_bf, l1g, l1b, wrh, wrl, br = consts
    nb = R // bm
    n_tok = G * R
    u = jnp.asarray(np.triu(np.ones((bm, bm), np.float32), 1), BF16)
    row = lambda w: pl.BlockSpec((None, bm, w), lambda g, i: (g, i, 0))
    full = lambda a: pl.BlockSpec(a.shape, lambda g, i: (0,) * a.ndim)
    tok_lanes = pl.BlockSpec((TOP_K, bm), lambda g, i: (0, g * nb + i))
    return pl.pallas_call(
        functools.partial(_outproj_kernel, n_steps=G * nb),
        grid=(G, nb),
        in_specs=[row(GROUP_W), row(GROUP_W), row(D), full(ln_g), full(ln_b),
                  _mod_spec(g1, bm), _mod_spec(sc2, bm), _mod_spec(sh2, bm),
                  full(w_out_bf), full(l1g), full(l1b), full(wrh), full(wrl), full(br), full(u), full(counts_in)],
        out_specs=[row(D), pl.BlockSpec(memory_space=pl.ANY), tok_lanes, tok_lanes, tok_lanes, full(counts_in)],
        out_shape=[jax.ShapeDtypeStruct((G, R, D), F32),
                   jax.ShapeDtypeStruct((n_tok,) + ROW_SHAPE, F32),
                   jax.ShapeDtypeStruct((TOP_K, n_tok), I32),
                   jax.ShapeDtypeStruct((TOP_K, n_tok), F32),
                   jax.ShapeDtypeStruct((TOP_K, n_tok), I32),
                   jax.ShapeDtypeStruct(counts_in.shape, F32)],
        scratch_shapes=[pltpu.VMEM(counts_in.shape, F32), pltpu.VMEM((2, bm, D), F32), pltpu.SemaphoreType.DMA((2,))],
        compiler_params=_cparams(("arbitrary", "arbitrary")),
        name="outproj",
    )(hg, da, x, ln_g, ln_b, g1, sc2, sh2, w_out_bf, l1g, l1b, wrh, wrl, br, u, counts_in)


def _dispatch_kernel(dest_ref, h2_ref, xb_in_hbm, xb_hbm, sem, *, bt):
    del xb_in_hbm

    def body(t, c):
        for kk in range(TOP_K):
            pltpu.make_async_copy(h2_ref.at[t], xb_hbm.at[dest_ref[kk, t]], sem).start(priority=kk % 2)
        return c

    lax.fori_loop(0, bt, body, 0, unroll=4)
    for kk in range(TOP_K):
        pltpu.make_async_copy(h2_ref, xb_hbm.at[pl.ds(0, bt)], sem).wait()


def _dispatch(dest, h2, xb):
    n_tok = h2.shape[0]
    bt = TOK_BLOCK
    anyspec = pl.BlockSpec(memory_space=pl.ANY)
    return pl.pallas_call(
        functools.partial(_dispatch_kernel, bt=bt),
        grid=(n_tok // bt,),
        in_specs=[pl.BlockSpec((TOP_K, bt), lambda i: (0, i), memory_space=pltpu.SMEM),
                  pl.BlockSpec((bt,) + ROW_SHAPE, lambda i: (i, 0, 0)), anyspec],
        out_specs=anyspec,
        out_shape=jax.ShapeDtypeStruct(xb.shape, xb.dtype),
        scratch_shapes=[pltpu.SemaphoreType.DMA(())],
        input_output_aliases={2: 0},
        compiler_params=_cparams(("arbitrary",)),
        name="dispatch",
    )(dest, h2, xb)


def _experts_kernel(be_ref, nu_ref, xb_hbm, wu_ref, bu_ref, wd_ref, bd_ref, yb_hbm,
                    wu_bf, wd_bf, xbuf, ybuf, sems, *, bm):
    j = pl.program_id(0)
    nb = pl.num_programs(0)
    slot = j % 2
    nu = nu_ref[0]
    active = j < nu
    x_copies = lambda blk, s: [_rows_copy(xbuf.at[s], xb_hbm, blk * bm, bm, sems.at[0, s], False)]
    y_copies = lambda blk, s: [_rows_copy(ybuf.at[s], yb_hbm, blk * bm, bm, sems.at[1, s], True)]

    @pl.when(j == 0)
    def _():
        for cp in x_copies(0, 0):
            cp.start()

    @pl.when(j + 1 < nu)
    def _():
        for cp in x_copies(j + 1, 1 - slot):
            cp.start()

    @pl.when(j >= 2)
    def _():
        for cp in y_copies(j - 2, slot):
            cp.wait()

    @pl.when(active & ((j == 0) | (be_ref[j] != be_ref[jnp.maximum(j - 1, 0)])))
    def _():
        wu_bf[...] = wu_ref[...].astype(BF16)
        wd_bf[...] = wd_ref[...].astype(BF16)

    @pl.when(active)
    def _():
        for cp in x_copies(j, slot):
            cp.wait()
        u = _dot(xbuf[slot].astype(BF16), wu_bf[...]) + bu_ref[...]
        glu = jnp.minimum(u[:, :D_FF], SWIGLU_LIMIT)
        lin = jnp.clip(u[:, D_FF:], -SWIGLU_LIMIT, SWIGLU_LIMIT)
        act = glu * _sigmoid(SWIGLU_ALPHA * glu) * (lin + 1.0)
        ybuf[slot] = _dot(act.astype(BF16), wd_bf[...]) + bd_ref[...]

    @pl.when(jnp.logical_not(active))
    def _():
        ybuf[slot] = jnp.zeros((bm, D_MODEL), F32)

    for cp in y_copies(j, slot):
        cp.start()

    @pl.when(j == nb - 1)
    def _():
        for cp in y_copies(j - 1, 1 - slot) + y_copies(j, slot):
            cp.wait()


def _experts(blk_exp, n_used, xb, w_up, b_up, w_down, b_down):
    n_rows = xb.shape[0]
    bm = MOE_BLOCK
    nb = n_rows // bm
    grid_spec = pltpu.PrefetchScalarGridSpec(
        num_scalar_prefetch=2,
        grid=(nb,),
        in_specs=[pl.BlockSpec(memory_space=pl.ANY),
                  pl.BlockSpec((None, D_MODEL, 2 * D_FF), lambda j, be, nu: (be[j], 0, 0)),
                  pl.BlockSpec((None, 1, 2 * D_FF), lambda j, be, nu: (be[j], 0, 0)),
                  pl.BlockSpec((None, D_FF, D_MODEL), lambda j, be, nu: (be[j], 0, 0)),
                  pl.BlockSpec((None, 1, D_MODEL), lambda j, be, nu: (be[j], 0, 0))],
        out_specs=pl.BlockSpec(memory_space=pl.ANY),
        scratch_shapes=[pltpu.VMEM((D_MODEL, 2 * D_FF), BF16), pltpu.VMEM((D_FF, D_MODEL), BF16),
                        pltpu.VMEM((2, bm, D_MODEL), F32), pltpu.VMEM((2, bm, D_MODEL), F32),
                        pltpu.SemaphoreType.DMA((2, 2))],
    )
    return pl.pallas_call(
        functools.partial(_experts_kernel, bm=bm),
        grid_spec=grid_spec,
        out_shape=jax.ShapeDtypeStruct((n_rows,) + ROW_SHAPE, F32),
        compiler_params=_cparams(("arbitrary",)),
        name="experts",
    )(blk_exp, n_used, xb, w_up, b_up.reshape(N_EXPERTS, 1, 2 * D_FF), w_down, b_down.reshape(N_EXPERTS, 1, D_MODEL))


def _combine_kernel(dcur_ref, dnext_ref, x1_ref, tw_ref, g2_ref, lg_ref, lb_ref, yb_hbm, o_ref, buf, sems, *, bm):
    i = pl.program_id(0) * pl.num_programs(1) + pl.program_id(1)
    n = pl.num_programs(0) * pl.num_programs(1)
    slot = i % 2

    def issue(dref, s):
        def body(t, c):
            for kk in range(TOP_K):
                pltpu.make_async_copy(yb_hbm.at[dref[kk, t]], buf.at[s, kk, pl.ds(t, 1), :],
                                      sems.at[s]).start(priority=kk % 2)
            return c
        lax.fori_loop(0, bm, body, 0, unroll=4)

    @pl.when(i == 0)
    def _():
        issue(dcur_ref, 0)

    @pl.when(i + 1 < n)
    def _():
        issue(dnext_ref, 1 - slot)

    for kk in range(TOP_K):
        pltpu.make_async_copy(yb_hbm.at[pl.ds(0, bm), 0, :], buf.at[slot, kk], sems.at[slot]).wait()

    tw = tw_ref[...]
    ff = tw[:, 0:1] * buf[slot, 0]
    for kk in range(1, TOP_K):
        ff = ff + tw[:, kk:kk + 1] * buf[slot, kk]
    o_ref[...] = _layer_norm(DEEPNORM_ALPHA * x1_ref[...] + (1.0 + g2_ref[...]) * ff, lg_ref[...], lb_ref[...])


def _combine(dest, tw_rows, x1, g2, ln_g, ln_b, yb, bm):
    G, R, D = x1.shape
    nb = R // bm
    n_blk = G * nb
    row = lambda w: pl.BlockSpec((None, bm, w), lambda g, i: (g, i, 0))
    full = lambda a: pl.BlockSpec(a.shape, lambda g, i: (0,) * a.ndim)
    cur = pl.BlockSpec((TOP_K, bm), lambda g, i: (0, g * nb + i), memory_space=pltpu.SMEM)
    nxt = pl.BlockSpec((TOP_K, bm), lambda g, i: (0, jnp.minimum(g * nb + i + 1, n_blk - 1)),
                       memory_space=pltpu.SMEM)
    return pl.pallas_call(
        functools.partial(_combine_kernel, bm=bm),
        grid=(G, nb),
        in_specs=[cur, nxt, row(D), pl.BlockSpec((bm, TOP_K), lambda g, i: (g * nb + i, 0)),
                  _mod_spec(g2, bm), full(ln_g), full(ln_b), pl.BlockSpec(memory_space=pl.ANY)],
        out_specs=row(D),
        out_shape=jax.ShapeDtypeStruct((G, R, D), F32),
        scratch_shapes=[pltpu.VMEM((2, TOP_K, bm, D), F32), pltpu.SemaphoreType.DMA((2,))],
        compiler_params=_cparams(("arbitrary", "arbitrary")),
        name="combine",
    )(dest, dest, x1, tw_rows, g2, ln_g, ln_b, yb)


def kernel(x_prompt, x_sample, c_prompt, c_sample, cache_k, cache_v, state_hgrn, page_table, ln_in_g, ln_in_b, w_ada, b_ada, w_in, hg_lb, hg_norm_g, da_lq1, da_lk1, da_lq2, da_lk2, da_subln_g, w_out, ln1_g, ln1_b, w_router, b_router, w_up, b_up, w_down, b_down, ln2_g, ln2_b):
    assert w_in.shape[0] == 1, "single-layer trunk"
    B, T, D = x_prompt.shape
    NS = x_sample.shape[0]
    n_prompt = B * T
    n_tot = n_prompt + NS
    r1 = lambda a: a.reshape(1, -1)

    lb = r1(jax.nn.softmax(hg_lb.astype(F32), axis=0)[0])
    lam = (jnp.exp(jnp.sum(da_lq1[0].astype(F32) * da_lk1[0].astype(F32)))
           - jnp.exp(jnp.sum(da_lq2[0].astype(F32) * da_lk2[0].astype(F32))) + LAM_INIT)
    lam_row = jnp.full((1, HEAD_W), lam, F32)
    ln_g, ln_b = r1(ln_in_g), r1(ln_in_b)
    w_in_bf = w_in[0].astype(BF16)
    wqt = w_in_bf[:, HG_COLS:HG_COLS + GROUP_W].T
    wvt = w_in_bf[:, HG_COLS + 2 * GROUP_W:].T
    w_out_bf = w_out[0].astype(BF16)
    wr_t = w_router[0].T
    wrh = wr_t.astype(BF16)
    wrl = (wr_t - wrh.astype(F32)).astype(BF16)
    br = b_router[0].reshape(N_EXPERTS, 1)
    norm_g, subln_g = r1(hg_norm_g[0]), r1(da_subln_g[0])

    n_c = B + NS
    c_all = jnp.concatenate([c_prompt, c_sample, jnp.zeros((-n_c % 8, D), F32)], axis=0)
    mod = _ada(c_all, w_ada[0], b_ada[0])
    mod_p = [mod[:B, j * D:(j + 1) * D].reshape(B, 1, D) for j in range(6)]
    mod_s = [mod[B:n_c, j * D:(j + 1) * D].reshape(1, NS, D) for j in range(6)]

    zhg_p, kf_p, vf_p, kb_p, qt_p, vt_p = _inproj(x_prompt, mod_p[1], mod_p[0], ln_g, ln_b, w_in_bf, INPROJ_BLOCK,
                                                  (wqt, wvt, ATTN_BLOCK))
    hg_p, s_p = _hgrn_prompt(zhg_p, lb, norm_g)
    da_p = _attn_prompt(qt_p, kb_p, vt_p, lam, subln_g)

    xs = x_sample.reshape(1, NS, D)
    zhg_s, kf_s, vf_s, q_s = _inproj(xs, mod_s[1], mod_s[0], ln_g, ln_b, w_in_bf, NS)
    hg_s, s_s = _hgrn_sample(zhg_s[0], state_hgrn[0], lb, norm_g)
    da_s = _attn_sample(q_s[0], kf_s.reshape(NS, GROUP_W), vf_s.reshape(NS, GROUP_W), cache_k, cache_v, page_table,
                        lam_row, subln_g)

    consts = (ln_g, ln_b, w_out_bf, r1(ln1_g[0]), r1(ln1_b[0]), wrh, wrl, br)
    counts0 = jnp.zeros((N_EXPERTS, 1), F32)
    x1_p, h2_p, idx_p, tw_p, rank_p, counts1 = _outproj(
        hg_p, da_p, x_prompt, (mod_p[2], mod_p[4], mod_p[3]), consts, counts0, ROW_BLOCK)
    x1_s, h2_s, idx_s, tw_s, rank_s, counts = _outproj(
        hg_s.reshape(1, NS, GROUP_W), da_s.reshape(1, NS, GROUP_W), xs, (mod_s[2], mod_s[4], mod_s[3]), consts,
        counts1, NS)

    cnt = counts[:, 0].astype(I32)
    padded = (cnt + MOE_BLOCK - 1) // MOE_BLOCK * MOE_BLOCK
    pad_end = jnp.cumsum(padded)
    pad_start = pad_end - padded
    e_ids = jnp.arange(N_EXPERTS, dtype=I32)

    def slot_of(idx, rank):
        return jnp.sum(jnp.where(idx[..., None] == e_ids, pad_start, 0), axis=-1) + rank

    dest_p = slot_of(idx_p, rank_p)
    dest_s = slot_of(idx_s, rank_s)
    n_blocks = -(-n_tot * TOP_K // MOE_BLOCK) + N_EXPERTS
    blk_row0 = jnp.arange(n_blocks, dtype=I32) * MOE_BLOCK
    blk_exp = jnp.minimum(jnp.sum((pad_end[None, :] <= blk_row0[:, None]).astype(I32), axis=1), N_EXPERTS - 1)
    n_used = (pad_end[-1:] // MOE_BLOCK).astype(I32)

    xb = jnp.zeros((n_blocks * MOE_BLOCK,) + ROW_SHAPE, F32)
    xb = _dispatch(dest_p, h2_p, xb)
    xb = _dispatch(dest_s, h2_s, xb)
    yb = _experts(blk_exp, n_used, xb, w_up[0], b_up[0], w_down[0], b_down[0])

    l2g, l2b = r1(ln2_g[0]), r1(ln2_b[0])
    y_p = _combine(dest_p, tw_p.T, x1_p, mod_p[5], l2g, l2b, yb, TOK_BLOCK)
    y_s = _combine(dest_s, tw_s.T, x1_s, mod_s[5], l2g, l2b, yb, TOK_BLOCK)

    smp = lambda a: a.reshape(1, NS, 1, HEADS, HEAD_W)
    return (y_p, y_s.reshape(NS, 1, D), kf_p[None], vf_p[None], s_p[None], smp(kf_s), smp(vf_s), s_s[None])
```

```python
import functools
import math

import numpy as np
import jax
import jax.numpy as jnp
from jax import lax
from jax.experimental import pallas as pl
from jax.experimental.pallas import tpu as pltpu

F32, BF16, I32, U32 = jnp.float32, jnp.bfloat16, jnp.int32, jnp.uint32

D_MODEL = 1024
HEADS = 4
HEAD_W = 128
DA_HEAD = 64
GROUP_W = HEADS * HEAD_W
HG_COLS = 4 * GROUP_W
IN_WIDTH = HG_COLS + 3 * GROUP_W
N_EXPERTS = 32
TOP_K = 4
D_FF = 1024
SWIGLU_ALPHA = 1.702
SWIGLU_LIMIT = 7.0
DEEPNORM_ALPHA = 2.0 ** 0.25
LN_EPS = 1e-5
RMS_EPS = 1e-6
LAM_INIT = 0.8 - 0.6 * math.exp(-0.3 * 0)
LOG2E = math.log2(math.e)
ALIBI_SLOPES = tuple(2.0 ** (-8.0 * (h + 1) / HEADS) for h in range(HEADS))
PAGE_SIZE = 128
EXP_CLAMP = 80.0
SKIP_LOG2 = 151.0

VMEM_LIMIT = 56 * 1024 * 1024
MOE_BLOCK = 256
HG_CHUNK = 128
ATTN_BLOCK = 1024
INPROJ_BLOCK = 512
ROW_BLOCK = 512
TOK_BLOCK = 128


def _cparams(sem):
    return pltpu.CompilerParams(dimension_semantics=sem, vmem_limit_bytes=VMEM_LIMIT)


def _sigmoid(x):
    return 1.0 / (1.0 + jnp.exp(-x))


def _layer_norm(x, g, b):
    mu = jnp.mean(x, -1, keepdims=True)
    xc = x - mu
    var = jnp.mean(xc * xc, -1, keepdims=True)
    return xc * lax.rsqrt(var + LN_EPS) * g + b


def _dot(a, b):
    return jnp.dot(a, b, preferred_element_type=F32)


def _dot_nt(a, b):
    return lax.dot_general(a, b, (((1,), (1,)), ((), ())), preferred_element_type=F32)


def _dot_tn(a, b):
    return lax.dot_general(a, b, (((0,), (0,)), ((), ())), preferred_element_type=F32)


ROW_SHAPE = (1, D_MODEL)


def _rows_copy(mat_ref, rows_hbm, row0, n, sem, to_hbm):
    rows = rows_hbm.at[pl.ds(row0, n), 0, :]
    return pltpu.make_async_copy(mat_ref, rows, sem) if to_hbm else pltpu.make_async_copy(rows, mat_ref, sem)


def _ada_kernel(c_ref, w_ref, b_ref, o_ref):
    c = c_ref[...]
    a = (c * _sigmoid(c)).astype(BF16)
    o_ref[...] = _dot(a, w_ref[...].astype(BF16)) + b_ref[...]


def _ada(c, w_ada, b_ada):
    rows, d = c.shape
    n = w_ada.shape[1]
    bn = 1536
    return pl.pallas_call(
        _ada_kernel,
        grid=(n // bn,),
        in_specs=[pl.BlockSpec((rows, d), lambda j: (0, 0)),
                  pl.BlockSpec((d, bn), lambda j: (0, j)),
                  pl.BlockSpec((1, bn), lambda j: (0, j))],
        out_specs=pl.BlockSpec((rows, bn), lambda j: (0, j)),
        out_shape=jax.ShapeDtypeStruct((rows, n), F32),
        compiler_params=_cparams(("arbitrary",)),
        name="ada",
    )(c, w_ada, b_ada.reshape(1, n))


def _inproj_common(x_ref, g_ref, b_ref, sc_ref, sh_ref, w_ref, zhg_ref, kf_ref, vf_ref):
    x0 = _layer_norm(x_ref[...], g_ref[...], b_ref[...])
    h = (x0 * (1.0 + sc_ref[...]) + sh_ref[...]).astype(BF16)
    zhg_ref[...] = _dot(h, w_ref[:, 0:HG_COLS])
    c0 = HG_COLS + GROUP_W
    k = _dot(h, w_ref[:, c0:c0 + GROUP_W])
    v = _dot(h, w_ref[:, c0 + GROUP_W:c0 + 2 * GROUP_W])
    for hd in range(HEADS):
        kf_ref[:, hd, :] = k[:, hd * HEAD_W:(hd + 1) * HEAD_W]
        vf_ref[:, hd, :] = v[:, hd * HEAD_W:(hd + 1) * HEAD_W]
    return h, k


def _inproj_prompt_kernel(x_ref, g_ref, b_ref, sc_ref, sh_ref, w_ref, wqt_ref, wvt_ref,
                          zhg_ref, kf_ref, vf_ref, kb_ref, qt_ref, vt_ref):
    h, k = _inproj_common(x_ref, g_ref, b_ref, sc_ref, sh_ref, w_ref, zhg_ref, kf_ref, vf_ref)
    kb_ref[...] = k.astype(BF16)
    qt_ref[...] = (_dot_nt(wqt_ref[...], h) * (DA_HEAD ** -0.5 * LOG2E)).astype(BF16)
    vt_ref[...] = _dot_nt(wvt_ref[...], h).astype(BF16)


def _inproj_sample_kernel(x_ref, g_ref, b_ref, sc_ref, sh_ref, w_ref, zhg_ref, kf_ref, vf_ref, q_ref):
    h, _ = _inproj_common(x_ref, g_ref, b_ref, sc_ref, sh_ref, w_ref, zhg_ref, kf_ref, vf_ref)
    q_ref[...] = (_dot(h, w_ref[:, HG_COLS:HG_COLS + GROUP_W]) * (DA_HEAD ** -0.5)).astype(BF16)


def _mod_spec(mod, bm):
    if mod.shape[1] == 1:
        return pl.BlockSpec((None, 1, mod.shape[2]), lambda g, i: (g, 0, 0))
    return pl.BlockSpec((None, bm, mod.shape[2]), lambda g, i: (g, i, 0))


def _inproj(x, sc, sh, ln_g, ln_b, w_in_bf, bm, transposed=None):
    G, R, D = x.shape
    nb = R // bm
    row = lambda w: pl.BlockSpec((None, bm, w), lambda g, i: (g, i, 0))
    full = lambda a: pl.BlockSpec(a.shape, lambda g, i: (0,) * a.ndim)
    heads = pl.BlockSpec((None, bm, HEADS, HEAD_W), lambda g, i: (g, i, 0, 0))
    heads_shape = jax.ShapeDtypeStruct((G, R, HEADS, HEAD_W), F32)
    sds = lambda w, dt: jax.ShapeDtypeStruct((G, R, w), dt)
    args = [x, ln_g, ln_b, sc, sh, w_in_bf]
    in_specs = [row(D), full(ln_g), full(ln_b), _mod_spec(sc, bm), _mod_spec(sh, bm), full(w_in_bf)]
    out_specs = [row(HG_COLS), heads, heads, row(GROUP_W)]
    out_shape = [sds(HG_COLS, F32), heads_shape, heads_shape, sds(GROUP_W, BF16)]
    body = _inproj_sample_kernel
    if transposed is not None:
        wqt, wvt, tblk = transposed
        per = tblk // bm
        tr = pl.BlockSpec((None, None, GROUP_W, bm), lambda g, i: (g, i // per, 0, i % per))
        tr_shape = jax.ShapeDtypeStruct((G, R // tblk, GROUP_W, tblk), BF16)
        args += [wqt, wvt]
        in_specs += [full(wqt), full(wvt)]
        out_specs += [tr, tr]
        out_shape += [tr_shape, tr_shape]
        body = _inproj_prompt_kernel
    return pl.pallas_call(
        body,
        grid=(G, nb),
        in_specs=in_specs,
        out_specs=out_specs,
        out_shape=out_shape,
        compiler_params=_cparams(("arbitrary", "arbitrary")),
        name="inproj",
    )(*args)


def _hgrn_gates(zq, zf, lb):
    q = zq * _sigmoid(zq)
    f = lb + (1.0 - lb) * _sigmoid(zf)
    k = (1.0 - lb) * _sigmoid(-zf)
    return q, jnp.log(f), k


def _hgrn_kernel(z_ref, lb_ref, ng_ref, lvl_ref, tri_ref, o_ref, sfin_ref, st_ref, *, C):
    t = pl.program_id(1)

    @pl.when(t == 0)
    def _():
        st_ref[...] = jnp.zeros_like(st_ref)

    lvl = lvl_ref[...]
    tri = tri_ref[...]
    n_levels = int(math.log2(C)) - 3
    ng = ng_ref[...]
    for h in range(HEADS):
        cs = slice(h * HEAD_W, (h + 1) * HEAD_W)
        zq = z_ref[:, h * HEAD_W:(h + 1) * HEAD_W]
        zf = z_ref[:, GROUP_W + h * HEAD_W:GROUP_W + (h + 1) * HEAD_W]
        v = z_ref[:, 2 * GROUP_W + h * HEAD_W:2 * GROUP_W + (h + 1) * HEAD_W].astype(BF16)
        zg = z_ref[:, 3 * GROUP_W + h * HEAD_W:3 * GROUP_W + (h + 1) * HEAD_W]
        q, g, k = _hgrn_gates(zq, zf, lb_ref[:, cs])
        g1 = g.astype(BF16)
        r1 = g - g1.astype(F32)
        g2 = r1.astype(BF16)
        g3 = (r1 - g2.astype(F32)).astype(BF16)
        b = _dot(tri, g1) + _dot(tri, g2) + _dot(tri, g3)

        b8 = b.reshape(C // 8, 8, HEAD_W)
        bmid = jnp.broadcast_to(b8[:, 3:4, :], b8.shape).reshape(C, HEAD_W)
        e = jnp.clip(b - bmid, -EXP_CLAMP, EXP_CLAMP)
        a = jnp.where(lvl == 0, _dot_nt((q * jnp.exp(e)).astype(BF16), (k * jnp.exp(-e)).astype(BF16)), 0.0)
        for li in range(1, n_levels + 1):
            m = 4 << li
            bb = b.reshape(C // (2 * m), 2 * m, HEAD_W)
            d = b - jnp.broadcast_to(bb[:, m - 1:m, :], bb.shape).reshape(C, HEAD_W)
            qs = (q * jnp.exp(jnp.minimum(d, 0.0))).astype(BF16)
            ks = (k * jnp.exp(jnp.minimum(-d, 0.0))).astype(BF16)
            a = jnp.where(lvl == li, _dot_nt(qs, ks), a)

        st = st_ref[h]
        o = _dot(a.astype(BF16), v) + _dot_nt((q * jnp.exp(b)).astype(BF16), st.astype(BF16))
        b_last = b[C - 1:C, :]
        kd = (k * jnp.exp(b_last - b)).astype(BF16)
        st_ref[h] = st * jnp.exp(b_last) + _dot_tn(v, kd)

        ms = jnp.mean(o * o, -1, keepdims=True)
        o_ref[:, cs] = (o * lax.rsqrt(ms + RMS_EPS) * ng * (zg * _sigmoid(zg))).astype(BF16)

    @pl.when(t == pl.num_programs(1) - 1)
    def _():
        for h in range(HEADS):
            sfin_ref[h] = st_ref[h].T


def _hgrn_level_table(C):
    t = np.arange(C)[:, None]
    s = np.arange(C)[None, :]
    x = t ^ s
    lvl = np.zeros((C, C), np.int32)
    m = 8
    while m < C:
        lvl += (x >= m).astype(np.int32)
        m *= 2
    return np.where(s <= t, lvl, -1).astype(np.int32)


def _hgrn_prompt(zhg, lb, norm_g):
    B, T, _ = zhg.shape
    C = HG_CHUNK
    lvl = jnp.asarray(_hgrn_level_table(C))
    tri = jnp.asarray(np.tril(np.ones((C, C), np.float32)), BF16)
    full = lambda a: pl.BlockSpec(a.shape, lambda b, t: (0,) * a.ndim)
    return pl.pallas_call(
        functools.partial(_hgrn_kernel, C=C),
        grid=(B, T // C),
        in_specs=[pl.BlockSpec((None, C, HG_COLS), lambda b, t: (b, t, 0)), full(lb), full(norm_g), full(lvl), full(tri)],
        out_specs=[pl.BlockSpec((None, C, GROUP_W), lambda b, t: (b, t, 0)),
                   pl.BlockSpec((None, HEADS, HEAD_W, HEAD_W), lambda b, t: (b, 0, 0, 0))],
        out_shape=[jax.ShapeDtypeStruct((B, T, GROUP_W), BF16),
                   jax.ShapeDtypeStruct((B, HEADS, HEAD_W, HEAD_W), F32)],
        scratch_shapes=[pltpu.VMEM((HEADS, HEAD_W, HEAD_W), F32)],
        compiler_params=_cparams(("arbitrary", "arbitrary")),
        name="hgrn_prompt",
    )(zhg, lb, norm_g, lvl, tri)


def _hgrn_step_kernel(z_ref, s_ref, lb_ref, ng_ref, o_ref, so_ref, *, G):
    ng = ng_ref[...]
    for h in range(HEADS):
        cs = slice(h * HEAD_W, (h + 1) * HEAD_W)
        zq = z_ref[:, h * HEAD_W:(h + 1) * HEAD_W]
        zf = z_ref[:, GROUP_W + h * HEAD_W:GROUP_W + (h + 1) * HEAD_W]
        v = z_ref[:, 2 * GROUP_W + h * HEAD_W:2 * GROUP_W + (h + 1) * HEAD_W]
        zg = z_ref[:, 3 * GROUP_W + h * HEAD_W:3 * GROUP_W + (h + 1) * HEAD_W]
        lb = lb_ref[:, cs]
        q = zq * _sigmoid(zq)
        f = lb + (1.0 - lb) * _sigmoid(zf)
        k = (1.0 - lb) * _sigmoid(-zf)
        qT, fT, kT = q.T, f.T, k.T
        rows = []
        for j in range(G):
            s_new = fT[:, j:j + 1] * s_ref[j, h] + kT[:, j:j + 1] * v[j:j + 1, :]
            so_ref[j, h] = s_new
            rows.append(jnp.sum(s_new * qT[:, j:j + 1], axis=0, keepdims=True))
        o = jnp.concatenate(rows, axis=0)
        ms = jnp.mean(o * o, -1, keepdims=True)
        o_ref[:, cs] = (o * lax.rsqrt(ms + RMS_EPS) * ng * (zg * _sigmoid(zg))).astype(BF16)


def _hgrn_sample(zhg, state, lb, norm_g):
    N = zhg.shape[0]
    G = 8
    full = lambda a: pl.BlockSpec(a.shape, lambda i: (0,) * a.ndim)
    st_spec = pl.BlockSpec((G, HEADS, HEAD_W, HEAD_W), lambda i: (i, 0, 0, 0))
    return pl.pallas_call(
        functools.partial(_hgrn_step_kernel, G=G),
        grid=(N // G,),
        in_specs=[pl.BlockSpec((G, HG_COLS), lambda i: (i, 0)), st_spec, full(lb), full(norm_g)],
        out_specs=[pl.BlockSpec((G, GROUP_W), lambda i: (i, 0)), st_spec],
        out_shape=[jax.ShapeDtypeStruct((N, GROUP_W), BF16), jax.ShapeDtypeStruct(state.shape, F32)],
        compiler_params=_cparams(("arbitrary",)),
        name="hgrn_sample",
    )(zhg, state, lb, norm_g)


def _attn_kernel(qt_ref, k_ref, vt_ref, kpm_ref, sl_ref, feat_ref, lam_ref, gcol_ref, o_ref, m_ref, l_ref, acc_ref,
                 *, blk):
    qi = pl.program_id(2)
    row = lax.broadcasted_iota(I32, (HEAD_W, blk), 0)
    qt = qt_ref[...].astype(F32)
    a_hi = sl_ref[0:1, :]
    a_lo = sl_ref[1:2, :]
    slope2 = sl_ref[2:3, :]
    def slope_rows(r0):
        in_rows = (row >= r0) & (row < r0 + 4)
        return jnp.where(in_rows, jnp.where((row - r0) % 2 == 0, a_hi, a_lo), 0.0)

    q_aug = [jnp.where(row < DA_HEAD, qt, slope_rows(DA_HEAD)).astype(BF16),
             jnp.where(row >= DA_HEAD, qt, slope_rows(0)).astype(BF16)]
    keep = [feat_ref[0] > 0, feat_ref[1] > 0]
    feats = [feat_ref[2], feat_ref[3]]
    m_ref[...] = jnp.full_like(m_ref, -jnp.inf)
    l_ref[...] = jnp.zeros_like(l_ref)
    acc_ref[...] = jnp.zeros_like(acc_ref)

    def block(kj, masked):
        k_start = pl.multiple_of(kj * blk, blk)
        kb = k_ref[pl.ds(k_start, blk), :]
        vt = vt_ref[kj]
        off = slope2 * jnp.full((1, blk), k_start - qi * blk, I32).astype(F32)
        if masked:
            ok = lax.broadcasted_iota(I32, (blk, 1), 0) <= lax.broadcasted_iota(I32, (1, blk), 1)
        for mi in range(2):
            st = _dot(jnp.where(keep[mi], kb, feats[mi]), q_aug[mi])
            if masked:
                st = jnp.where(ok, st, -jnp.inf)
            m_prev = m_ref[mi]
            m_new = jnp.maximum(m_prev, jnp.max(st, axis=0, keepdims=True) + off)
            p = jnp.exp2(st - (m_new - off))
            alpha = jnp.exp2(m_prev - m_new)
            l_ref[mi] = alpha * l_ref[mi] + jnp.sum(p, axis=0, keepdims=True)
            acc_ref[mi] = alpha * acc_ref[mi] + _dot(vt, p.astype(BF16))
            m_ref[mi] = m_new

    block(qi, True)

    nblk = kpm_ref.shape[1]
    jlane = lax.broadcasted_iota(I32, (1, nblk), 1)
    far = slope2[:, 0:1] * ((jlane - qi) * blk + (blk - 1)).astype(F32)
    need = jlane < 0
    for mi in range(2):
        qm = qt[mi * DA_HEAD:(mi + 1) * DA_HEAD, :]
        qn = jnp.sqrt(jnp.max(jnp.sum(qm * qm, axis=0, keepdims=True), axis=1, keepdims=True))
        m_lo = jnp.min(m_ref[mi], axis=1, keepdims=True)
        need = need | (1.02 * qn * kpm_ref[mi:mi + 1, :] + 1.0 + far - m_lo >= -SKIP_LOG2)
    n_visit = jnp.sum((need & (jlane < qi)).astype(I32))

    def body(kj, c):
        block(kj, False)
        return c

    lax.fori_loop(qi - n_visit, qi, body, 0)

    ot = acc_ref[0] * (1.0 / l_ref[0]) - lam_ref[...] * (acc_ref[1] * (1.0 / l_ref[1]))
    ms = jnp.mean(ot * ot, axis=0, keepdims=True)
    o_ref[...] = (ot * lax.rsqrt(ms + RMS_EPS) * gcol_ref[...]).T.astype(BF16)


def _attn_prompt(qt, k, vt, lam, subln_g):
    B, nblk, _, blk = qt.shape
    T = k.shape[1]
    bf = lambda x: np.asarray(x, np.float32).astype(BF16).astype(np.float64)
    a = np.asarray(ALIBI_SLOPES, np.float64) * LOG2E
    a_hi = bf(a)
    a_lo = bf(a - a_hi)
    sl = np.zeros((HEADS, 8, blk), np.float32)
    sl[:, 0, :], sl[:, 1, :], sl[:, 2, :] = a_hi[:, None], a_lo[:, None], a.astype(np.float32)[:, None]
    lane = np.arange(HEAD_W)[None, :]
    r = np.arange(blk)[:, None]
    assert blk <= 256 * 256

    def pos_feat(l0):
        return (np.where((lane == l0) | (lane == l0 + 1), r // 256 * 256, 0)
                + np.where((lane == l0 + 2) | (lane == l0 + 3), r % 256, 0)).astype(np.float32)

    feat = np.stack([np.broadcast_to(lane < DA_HEAD, (blk, HEAD_W)).astype(np.float32),
                     np.broadcast_to(lane >= DA_HEAD, (blk, HEAD_W)).astype(np.float32),
                     pos_feat(DA_HEAD), pos_feat(0)])
    feat = jnp.asarray(feat, BF16)
    lam_row = jnp.full((1, blk), lam, F32)
    gcol = jnp.broadcast_to((subln_g.reshape(HEAD_W, 1) * (1.0 - LAM_INIT)), (HEAD_W, blk))
    kf = k.astype(F32).reshape(B, nblk, blk, HEADS, 2, DA_HEAD)
    kpm = lax.cummax(jnp.sqrt(jnp.max(jnp.sum(kf * kf, axis=-1), axis=2)), axis=1).transpose(0, 2, 3, 1)
    full = lambda x: pl.BlockSpec(x.shape, lambda b, h, i: (0,) * x.ndim)
    return pl.pallas_call(
        functools.partial(_attn_kernel, blk=blk),
        grid=(B, HEADS, nblk),
        in_specs=[pl.BlockSpec((None, None, HEAD_W, blk), lambda b, h, i: (b, i, h, 0)),
                  pl.BlockSpec((None, T, HEAD_W), lambda b, h, i: (b, 0, h)),
                  pl.BlockSpec((None, nblk, HEAD_W, blk), lambda b, h, i: (b, 0, h, 0)),
                  pl.BlockSpec((None, None, 2, nblk), lambda b, h, i: (b, h, 0, 0)),
                  pl.BlockSpec((None, 8, blk), lambda b, h, i: (h, 0, 0)),
                  full(feat), full(lam_row), full(gcol)],
        out_specs=pl.BlockSpec((None, blk, HEAD_W), lambda b, h, i: (b, i, h)),
        out_shape=jax.ShapeDtypeStruct((B, T, GROUP_W), BF16),
        scratch_shapes=[pltpu.VMEM((2, 1, blk), F32), pltpu.VMEM((2, 1, blk), F32), pltpu.VMEM((2, HEAD_W, blk), F32)],
        compiler_params=_cparams(("arbitrary", "arbitrary", "arbitrary")),
        name="attn_prompt",
    )(qt, k, vt, kpm, jnp.asarray(sl), feat, lam_row, gcol)


def _attn_decode_kernel(pt_ref, q_ref, kn_ref, vn_ref, bias_ref, lam_ref, g_ref, ck_hbm, cv_hbm, o_ref,
                        kbuf, vbuf, sems, *, n_pages):
    i = pl.program_id(0)
    slot = i % 2

    def page_copies(seq, s):
        for pg in range(n_pages):
            page = pt_ref[seq, pg]
            for h in range(HEADS):
                dst = (s, h, pl.ds(pg * PAGE_SIZE, PAGE_SIZE))
                yield pltpu.make_async_copy(ck_hbm.at[0, page, :, h, :], kbuf.at[dst], sems.at[0, s])
                yield pltpu.make_async_copy(cv_hbm.at[0, page, :, h, :], vbuf.at[dst], sems.at[1, s])

    @pl.when(i == 0)
    def _():
        for cp in page_copies(0, 0):
            cp.start()

    @pl.when(i + 1 < pl.num_programs(0))
    def _():
        for cp in page_copies(i + 1, 1 - slot):
            cp.start()

    for cp in page_copies(i, slot):
        cp.wait()

    r8 = lax.broadcasted_iota(I32, (8, GROUP_W), 0)
    c8 = lax.broadcasted_iota(I32, (8, GROUP_W), 1)
    qmat = jnp.where(c8 // DA_HEAD == r8, jnp.broadcast_to(q_ref[...].astype(F32), (8, GROUP_W)), 0.0)
    qmat_bf = qmat.astype(BF16)
    s = bias_ref[...]
    for h in range(HEADS):
        s = s + _dot_nt(qmat_bf[:, h * HEAD_W:(h + 1) * HEAD_W], kbuf[slot, h].astype(BF16))
    kn = kn_ref[...].astype(BF16).astype(F32)
    s_self = jnp.sum(qmat_bf.astype(F32) * kn, axis=-1, keepdims=True)
    m = jnp.maximum(jnp.max(s, -1, keepdims=True), s_self)
    p = jnp.exp(s - m)
    p_self = jnp.exp(s_self - m)
    inv_l = 1.0 / (jnp.sum(p, -1, keepdims=True) + p_self)
    coef = jnp.where(lax.broadcasted_iota(I32, (8, 1), 0) % 2 == 0, 1.0, -lam_ref[:, 0:1]) * inv_l
    w = (p * coef).astype(BF16)
    accs = [_dot(w, vbuf[slot, h].astype(BF16)) for h in range(HEADS)]
    acc = jnp.concatenate(accs, axis=1) + (p_self * coef) * vn_ref[...]
    o = jnp.sum(jnp.where(c8 // HEAD_W == r8 // 2, acc, 0.0), axis=0, keepdims=True)
    outs = []
    for h in range(HEADS):
        oh = o[:, h * HEAD_W:(h + 1) * HEAD_W]
        ms = jnp.mean(oh * oh, -1, keepdims=True)
        outs.append(oh * lax.rsqrt(ms + RMS_EPS) * g_ref[...] * (1.0 - LAM_INIT))
    o_ref[...] = jnp.concatenate(outs, axis=1).astype(BF16)


def _attn_sample(q, k_new, v_new, cache_k, cache_v, page_table, lam_row, subln_g):
    N, n_pages = page_table.shape
    past = n_pages * PAGE_SIZE
    kpos = np.arange(past, dtype=np.float32)[None, :]
    slope_rows = np.repeat(np.asarray(ALIBI_SLOPES, np.float32), 2)[:, None]
    bias = jnp.asarray(-slope_rows * (past - kpos))
    row = pl.BlockSpec((None, 1, GROUP_W), lambda i, pt: (i, 0, 0))
    full = lambda a: pl.BlockSpec(a.shape, lambda i, pt: (0,) * a.ndim)
    anyspec = pl.BlockSpec(memory_space=pl.ANY)
    grid_spec = pltpu.PrefetchScalarGridSpec(
        num_scalar_prefetch=1,
        grid=(N,),
        in_specs=[row, row, row, full(bias), full(lam_row), full(subln_g), anyspec, anyspec],
        out_specs=row,
        scratch_shapes=[pltpu.VMEM((2, HEADS, past, HEAD_W), F32), pltpu.VMEM((2, HEADS, past, HEAD_W), F32),
                        pltpu.SemaphoreType.DMA((2, 2))],
    )
    r3 = lambda a: a.reshape(N, 1, GROUP_W)
    out = pl.pallas_call(
        functools.partial(_attn_decode_kernel, n_pages=n_pages),
        grid_spec=grid_spec,
        out_shape=jax.ShapeDtypeStruct((N, 1, GROUP_W), BF16),
        compiler_params=_cparams(("arbitrary",)),
        name="attn_sample",
    )(page_table, r3(q), r3(k_new), r3(v_new), bias, lam_row, subln_g, cache_k, cache_v)
    return out.reshape(N, GROUP_W)


def _outproj_kernel(hg_ref, da_ref, x_ref, lg_ref, lb_ref, g1_ref, sc2_ref, sh2_ref, w_ref, l1g_ref, l1b_ref,
                    wrh_ref, wrl_ref, br_ref, u_ref, cin_ref,
                    x1_ref, h2_hbm, idx_ref, tw_ref, rank_ref, cout_ref, run_ref, hbuf, hsem, *, n_steps):
    step = pl.program_id(0) * pl.num_programs(1) + pl.program_id(1)
    slot = step % 2
    bm = x_ref.shape[0]
    h2_copies = lambda st, s: [_rows_copy(hbuf.at[s], h2_hbm, st * bm, bm, hsem.at[s], True)]

    @pl.when(step == 0)
    def _():
        run_ref[...] = cin_ref[...]

    x0 = _layer_norm(x_ref[...], lg_ref[...], lb_ref[...])
    mix = _dot(hg_ref[...], w_ref[0:GROUP_W, :]) + _dot(da_ref[...], w_ref[GROUP_W:2 * GROUP_W, :])
    x1 = _layer_norm(DEEPNORM_ALPHA * x0 + (1.0 + g1_ref[...]) * mix, l1g_ref[...], l1b_ref[...])
    x1_ref[...] = x1
    h2 = x1 * (1.0 + sc2_ref[...]) + sh2_ref[...]

    @pl.when(step >= 2)
    def _():
        for cp in h2_copies(step - 2, slot):
            cp.wait()

    hbuf[slot] = h2
    for cp in h2_copies(step, slot):
        cp.start()

    @pl.when(step == n_steps - 1)
    def _():
        tail = h2_copies(step, slot)
        if n_steps > 1:
            tail = h2_copies(step - 1, 1 - slot) + tail
        for cp in tail:
            cp.wait()

    hi = h2.astype(BF16)
    lo = (h2 - hi.astype(F32)).astype(BF16)
    wrh = wrh_ref[...]
    logits = _dot_nt(wrh, hi) + _dot_nt(wrh, lo) + _dot_nt(wrl_ref[...], hi) + br_ref[...]

    n_e, bm = logits.shape
    rows = lax.broadcasted_iota(I32, (n_e, bm), 0).astype(F32)
    vals, sels = [], []
    work = logits
    for kk in range(TOP_K):
        mx = jnp.max(work, axis=0, keepdims=True)
        ix = jnp.min(jnp.where(work == mx, rows, float(n_e)), axis=0, keepdims=True)
        sel = rows == ix
        idx_ref[kk:kk + 1, :] = ix.astype(I32)
        vals.append(mx)
        sels.append(sel)
        work = jnp.where(sel, -jnp.inf, work)
    es = [jnp.exp(vv - vals[0]) for vv in vals]
    inv = 1.0 / (es[0] + es[1] + es[2] + es[3])
    for kk in range(TOP_K):
        tw_ref[kk:kk + 1, :] = es[kk] * inv

    base = run_ref[...]
    for kk in range(TOP_K):
        oh = jnp.where(sels[kk], 1.0, 0.0)
        before = base + _dot(oh.astype(BF16), u_ref[...])
        rank_ref[kk:kk + 1, :] = jnp.sum(jnp.where(sels[kk], before, 0.0), axis=0, keepdims=True).astype(I32)
        base = base + jnp.sum(oh, axis=1, keepdims=True)
    run_ref[...] = base
    cout_ref[...] = base


def _outproj(hg, da, x, mods, consts, counts_in, bm):
    G, R, D = x.shape
    g1, sc2, sh2 = mods
    ln_g, ln_b, w_out_bf, l1g, l1b, wrh, wrl, br = consts
    nb = R // bm
    n_tok = G * R
    u = jnp.asarray(np.triu(np.ones((bm, bm), np.float32), 1), BF16)
    row = lambda w: pl.BlockSpec((None, bm, w), lambda g, i: (g, i, 0))
    full = lambda a: pl.BlockSpec(a.shape, lambda g, i: (0,) * a.ndim)
    tok_lanes = pl.BlockSpec((TOP_K, bm), lambda g, i: (0, g * nb + i))
    return pl.pallas_call(
        functools.partial(_outproj_kernel, n_steps=G * nb),
        grid=(G, nb),
        in_specs=[row(GROUP_W), row(GROUP_W), row(D), full(ln_g), full(ln_b),
                  _mod_spec(g1, bm), _mod_spec(sc2, bm), _mod_spec(sh2, bm),
                  full(w_out_bf), full(l1g), full(l1b), full(wrh), full(wrl), full(br), full(u), full(counts_in)],
        out_specs=[row(D), pl.BlockSpec(memory_space=pl.ANY), tok_lanes, tok_lanes, tok_lanes, full(counts_in)],
        out_shape=[jax.ShapeDtypeStruct((G, R, D), F32),
                   jax.ShapeDtypeStruct((n_tok,) + ROW_SHAPE, F32),
                   jax.ShapeDtypeStruct((TOP_K, n_tok), I32),
                   jax.ShapeDtypeStruct((TOP_K, n_tok), F32),
                   jax.ShapeDtypeStruct((TOP_K, n_tok), I32),
                   jax.ShapeDtypeStruct(counts_in.shape, F32)],
        scratch_shapes=[pltpu.VMEM(counts_in.shape, F32), pltpu.VMEM((2, bm, D), F32), pltpu.SemaphoreType.DMA((2,))],
        compiler_params=_cparams(("arbitrary", "arbitrary")),
        name="outproj",
    )(hg, da, x, ln_g, ln_b, g1, sc2, sh2, w_out_bf, l1g, l1b, wrh, wrl, br, u, counts_in)


def _dispatch_kernel(pe_ref, nu_ref, dest_ref, h2p_ref, h2s_ref, xb_hbm, zbuf, zsem, sem, *, bt, n_prompt_steps, nb):
    i = pl.program_id(0)
    bm = MOE_BLOCK

    @pl.when(i == 0)
    def _():
        zbuf[...] = jnp.zeros_like(zbuf)
        zero_block = lambda row0: pltpu.make_async_copy(zbuf, xb_hbm.at[pl.ds(row0, bm)], zsem)
        has_rows = [pe_ref[e] > (pe_ref[e - 1] if e else 0) for e in range(N_EXPERTS)]
        for e in range(N_EXPERTS):
            @pl.when(has_rows[e])
            def _():
                zero_block(pe_ref[e] - bm).start()

        def tail_start(j, c):
            zero_block(j * bm).start()
            return c

        def tail_wait(j, c):
            zero_block(j * bm).wait()
            return c

        lax.fori_loop(nu_ref[0], nb, tail_start, 0)
        for e in range(N_EXPERTS):
            @pl.when(has_rows[e])
            def _():
                zero_block(pe_ref[e] - bm).wait()
        lax.fori_loop(nu_ref[0], nb, tail_wait, 0)

    def scatter(h2_ref):
        def body(t, c):
            for kk in range(TOP_K):
                pltpu.make_async_copy(h2_ref.at[t], xb_hbm.at[dest_ref[kk, t]], sem).start(priority=kk % 2)
            return c

        lax.fori_loop(0, bt, body, 0, unroll=4)
        for kk in range(TOP_K):
            pltpu.make_async_copy(h2_ref, xb_hbm.at[pl.ds(0, bt)], sem).wait()

    @pl.when(i < n_prompt_steps)
    def _():
        scatter(h2p_ref)

    @pl.when(i >= n_prompt_steps)
    def _():
        scatter(h2s_ref)


def _dispatch(pad_end, n_used, dest, h2_p, h2_s, n_rows):
    bt = TOK_BLOCK
    n_p, n_s = h2_p.shape[0] // bt, h2_s.shape[0] // bt
    nb = n_rows // MOE_BLOCK
    grid_spec = pltpu.PrefetchScalarGridSpec(
        num_scalar_prefetch=2,
        grid=(n_p + n_s,),
        in_specs=[pl.BlockSpec((TOP_K, bt), lambda i, pe, nu: (0, i), memory_space=pltpu.SMEM),
                  pl.BlockSpec((bt,) + ROW_SHAPE, lambda i, pe, nu: (jnp.minimum(i, n_p - 1), 0, 0)),
                  pl.BlockSpec((bt,) + ROW_SHAPE, lambda i, pe, nu: (jnp.maximum(i - n_p, 0), 0, 0))],
        out_specs=pl.BlockSpec(memory_space=pl.ANY),
        scratch_shapes=[pltpu.VMEM((MOE_BLOCK,) + ROW_SHAPE, F32), pltpu.SemaphoreType.DMA(()),
                        pltpu.SemaphoreType.DMA(())],
    )
    return pl.pallas_call(
        functools.partial(_dispatch_kernel, bt=bt, n_prompt_steps=n_p, nb=nb),
        grid_spec=grid_spec,
        out_shape=jax.ShapeDtypeStruct((n_rows,) + ROW_SHAPE, F32),
        compiler_params=_cparams(("arbitrary",)),
        name="dispatch",
    )(pad_end, n_used, dest, h2_p, h2_s)


def _experts_kernel(be_ref, nu_ref, xb_hbm, wu_ref, bu_ref, wd_ref, bd_ref, yb_hbm,
                    wu_bf, wd_bf, xbuf, ybuf, sems, *, bm):
    j = pl.program_id(0)
    nb = pl.num_programs(0)
    slot = j % 2
    nu = nu_ref[0]
    active = j < nu
    x_copies = lambda blk, s: [_rows_copy(xbuf.at[s], xb_hbm, blk * bm, bm, sems.at[0, s], False)]
    y_copies = lambda blk, s: [_rows_copy(ybuf.at[s], yb_hbm, blk * bm, bm, sems.at[1, s], True)]

    @pl.when(j == 0)
    def _():
        for cp in x_copies(0, 0):
            cp.start()

    @pl.when(j + 1 < nu)
    def _():
        for cp in x_copies(j + 1, 1 - slot):
            cp.start()

    @pl.when(j >= 2)
    def _():
        for cp in y_copies(j - 2, slot):
            cp.wait()

    @pl.when(active & ((j == 0) | (be_ref[j] != be_ref[jnp.maximum(j - 1, 0)])))
    def _():
        wu_bf[...] = wu_ref[...].astype(BF16)
        wd_bf[...] = wd_ref[...].astype(BF16)

    @pl.when(active)
    def _():
        for cp in x_copies(j, slot):
            cp.wait()
        u = _dot(xbuf[slot].astype(BF16), wu_bf[...]) + bu_ref[...]
        glu = jnp.minimum(u[:, :D_FF], SWIGLU_LIMIT)
        lin = jnp.clip(u[:, D_FF:], -SWIGLU_LIMIT, SWIGLU_LIMIT)
        act = glu * _sigmoid(SWIGLU_ALPHA * glu) * (lin + 1.0)
        ybuf[slot] = _dot(act.astype(BF16), wd_bf[...]) + bd_ref[...]

    @pl.when(jnp.logical_not(active))
    def _():
        ybuf[slot] = jnp.zeros((bm, D_MODEL), F32)

    for cp in y_copies(j, slot):
        cp.start()

    @pl.when(j == nb - 1)
    def _():
        for cp in y_copies(j - 1, 1 - slot) + y_copies(j, slot):
            cp.wait()


def _experts(blk_exp, n_used, xb, w_up, b_up, w_down, b_down):
    n_rows = xb.shape[0]
    bm = MOE_BLOCK
    nb = n_rows // bm
    grid_spec = pltpu.PrefetchScalarGridSpec(
        num_scalar_prefetch=2,
        grid=(nb,),
        in_specs=[pl.BlockSpec(memory_space=pl.ANY),
                  pl.BlockSpec((None, D_MODEL, 2 * D_FF), lambda j, be, nu: (be[j], 0, 0)),
                  pl.BlockSpec((None, 1, 2 * D_FF), lambda j, be, nu: (be[j], 0, 0)),
                  pl.BlockSpec((None, D_FF, D_MODEL), lambda j, be, nu: (be[j], 0, 0)),
                  pl.BlockSpec((None, 1, D_MODEL), lambda j, be, nu: (be[j], 0, 0))],
        out_specs=pl.BlockSpec(memory_space=pl.ANY),
        scratch_shapes=[pltpu.VMEM((D_MODEL, 2 * D_FF), BF16), pltpu.VMEM((D_FF, D_MODEL), BF16),
                        pltpu.VMEM((2, bm, D_MODEL), F32), pltpu.VMEM((2, bm, D_MODEL), F32),
                        pltpu.SemaphoreType.DMA((2, 2))],
    )
    return pl.pallas_call(
        functools.partial(_experts_kernel, bm=bm),
        grid_spec=grid_spec,
        out_shape=jax.ShapeDtypeStruct((n_rows,) + ROW_SHAPE, F32),
        compiler_params=_cparams(("arbitrary",)),
        name="experts",
    )(blk_exp, n_used, xb, w_up, b_up.reshape(N_EXPERTS, 1, 2 * D_FF), w_down, b_down.reshape(N_EXPERTS, 1, D_MODEL))


def _combine_kernel(dcur_ref, dnext_ref, x1_ref, tw_ref, g2_ref, lg_ref, lb_ref, yb_hbm, o_ref, buf, sems, *, bm):
    i = pl.program_id(0) * pl.num_programs(1) + pl.program_id(1)
    n = pl.num_programs(0) * pl.num_programs(1)
    slot = i % 2

    def issue(dref, s):
        def body(t, c):
            for kk in range(TOP_K):
                pltpu.make_async_copy(yb_hbm.at[dref[kk, t]], buf.at[s, kk, pl.ds(t, 1), :],
                                      sems.at[s]).start(priority=kk % 2)
            return c
        lax.fori_loop(0, bm, body, 0, unroll=4)

    @pl.when(i == 0)
    def _():
        issue(dcur_ref, 0)

    @pl.when(i + 1 < n)
    def _():
        issue(dnext_ref, 1 - slot)

    for kk in range(TOP_K):
        pltpu.make_async_copy(yb_hbm.at[pl.ds(0, bm), 0, :], buf.at[slot, kk], sems.at[slot]).wait()

    tw = tw_ref[...]
    ff = tw[:, 0:1] * buf[slot, 0]
    for kk in range(1, TOP_K):
        ff = ff + tw[:, kk:kk + 1] * buf[slot, kk]
    o_ref[...] = _layer_norm(DEEPNORM_ALPHA * x1_ref[...] + (1.0 + g2_ref[...]) * ff, lg_ref[...], lb_ref[...])


def _combine(dest, tw_rows, x1, g2, ln_g, ln_b, yb, bm):
    G, R, D = x1.shape
    nb = R // bm
    n_blk = G * nb
    row = lambda w: pl.BlockSpec((None, bm, w), lambda g, i: (g, i, 0))
    full = lambda a: pl.BlockSpec(a.shape, lambda g, i: (0,) * a.ndim)
    cur = pl.BlockSpec((TOP_K, bm), lambda g, i: (0, g * nb + i), memory_space=pltpu.SMEM)
    nxt = pl.BlockSpec((TOP_K, bm), lambda g, i: (0, jnp.minimum(g * nb + i + 1, n_blk - 1)),
                       memory_space=pltpu.SMEM)
    return pl.pallas_call(
        functools.partial(_combine_kernel, bm=bm),
        grid=(G, nb),
        in_specs=[cur, nxt, row(D), pl.BlockSpec((bm, TOP_K), lambda g, i: (g * nb + i, 0)),
                  _mod_spec(g2, bm), full(ln_g), full(ln_b), pl.BlockSpec(memory_space=pl.ANY)],
        out_specs=row(D),
        out_shape=jax.ShapeDtypeStruct((G, R, D), F32),
        scratch_shapes=[pltpu.VMEM((2, TOP_K, bm, D), F32), pltpu.SemaphoreType.DMA((2,))],
        compiler_params=_cparams(("arbitrary", "arbitrary")),
        name="combine",
    )(dest, dest, x1, tw_rows, g2, ln_g, ln_b, yb)


def kernel(x_prompt, x_sample, c_prompt, c_sample, cache_k, cache_v, state_hgrn, page_table, ln_in_g, ln_in_b, w_ada, b_ada, w_in, hg_lb, hg_norm_g, da_lq1, da_lk1, da_lq2, da_lk2, da_subln_g, w_out, ln1_g, ln1_b, w_router, b_router, w_up, b_up, w_down, b_down, ln2_g, ln2_b):
    assert w_in.shape[0] == 1, "single-layer trunk"
    B, T, D = x_prompt.shape
    NS = x_sample.shape[0]
    n_prompt = B * T
    n_tot = n_prompt + NS
    r1 = lambda a: a.reshape(1, -1)

    lb = r1(jax.nn.softmax(hg_lb.astype(F32), axis=0)[0])
    lam = (jnp.exp(jnp.sum(da_lq1[0].astype(F32) * da_lk1[0].astype(F32)))
           - jnp.exp(jnp.sum(da_lq2[0].astype(F32) * da_lk2[0].astype(F32))) + LAM_INIT)
    lam_row = jnp.full((1, HEAD_W), lam, F32)
    ln_g, ln_b = r1(ln_in_g), r1(ln_in_b)
    w_in_bf = w_in[0].astype(BF16)
    wqt = w_in_bf[:, HG_COLS:HG_COLS + GROUP_W].T
    wvt = w_in_bf[:, HG_COLS + 2 * GROUP_W:].T
    w_out_bf = w_out[0].astype(BF16)
    wr_t = w_router[0].T
    wrh = wr_t.astype(BF16)
    wrl = (wr_t - wrh.astype(F32)).astype(BF16)
    br = b_router[0].reshape(N_EXPERTS, 1)
    norm_g, subln_g = r1(hg_norm_g[0]), r1(da_subln_g[0])

    n_c = B + NS
    c_all = jnp.concatenate([c_prompt, c_sample, jnp.zeros((-n_c % 8, D), F32)], axis=0)
    mod = _ada(c_all, w_ada[0], b_ada[0])
    mod_p = [mod[:B, j * D:(j + 1) * D].reshape(B, 1, D) for j in range(6)]
    mod_s = [mod[B:n_c, j * D:(j + 1) * D].reshape(1, NS, D) for j in range(6)]

    zhg_p, kf_p, vf_p, kb_p, qt_p, vt_p = _inproj(x_prompt, mod_p[1], mod_p[0], ln_g, ln_b, w_in_bf, INPROJ_BLOCK,
                                                  (wqt, wvt, ATTN_BLOCK))
    hg_p, s_p = _hgrn_prompt(zhg_p, lb, norm_g)
    da_p = _attn_prompt(qt_p, kb_p, vt_p, lam, subln_g)

    xs = x_sample.reshape(1, NS, D)
    zhg_s, kf_s, vf_s, q_s = _inproj(xs, mod_s[1], mod_s[0], ln_g, ln_b, w_in_bf, NS)
    hg_s, s_s = _hgrn_sample(zhg_s[0], state_hgrn[0], lb, norm_g)
    da_s = _attn_sample(q_s[0], kf_s.reshape(NS, GROUP_W), vf_s.reshape(NS, GROUP_W), cache_k, cache_v, page_table,
                        lam_row, subln_g)

    consts = (ln_g, ln_b, w_out_bf, r1(ln1_g[0]), r1(ln1_b[0]), wrh, wrl, br)
    counts0 = jnp.zeros((N_EXPERTS, 1), F32)
    x1_p, h2_p, idx_p, tw_p, rank_p, counts1 = _outproj(
        hg_p, da_p, x_prompt, (mod_p[2], mod_p[4], mod_p[3]), consts, counts0, ROW_BLOCK)
    x1_s, h2_s, idx_s, tw_s, rank_s, counts = _outproj(
        hg_s.reshape(1, NS, GROUP_W), da_s.reshape(1, NS, GROUP_W), xs, (mod_s[2], mod_s[4], mod_s[3]), consts,
        counts1, NS)

    cnt = counts[:, 0].astype(I32)
    padded = (cnt + MOE_BLOCK - 1) // MOE_BLOCK * MOE_BLOCK
    pad_end = jnp.cumsum(padded)
    pad_start = pad_end - padded
    e_ids = jnp.arange(N_EXPERTS, dtype=I32)

    def slot_of(idx, rank):
        return jnp.sum(jnp.where(idx[..., None] == e_ids, pad_start, 0), axis=-1) + rank

    dest_p = slot_of(idx_p, rank_p)
    dest_s = slot_of(idx_s, rank_s)
    n_blocks = -(-n_tot * TOP_K // MOE_BLOCK) + N_EXPERTS
    blk_row0 = jnp.arange(n_blocks, dtype=I32) * MOE_BLOCK
    blk_exp = jnp.minimum(jnp.sum((pad_end[None, :] <= blk_row0[:, None]).astype(I32), axis=1), N_EXPERTS - 1)
    n_used = (pad_end[-1:] // MOE_BLOCK).astype(I32)

    xb = _dispatch(pad_end.astype(I32), n_used, jnp.concatenate([dest_p, dest_s], axis=1), h2_p, h2_s,
                   n_blocks * MOE_BLOCK)
    yb = _experts(blk_exp, n_used, xb, w_up[0], b_up[0], w_down[0], b_down[0])

    l2g, l2b = r1(ln2_g[0]), r1(ln2_b[0])
    y_p = _combine(dest_p, tw_p.T, x1_p, mod_p[5], l2g, l2b, yb, TOK_BLOCK)
    y_s = _combine(dest_s, tw_s.T, x1_s, mod_s[5], l2g, l2b, yb, TOK_BLOCK)

    smp = lambda a: a.reshape(1, NS, 1, HEADS, HEAD_W)
    return (y_p, y_s.reshape(NS, 1, D), kf_p[None], vf_p[None], s_p[None], smp(kf_s), smp(vf_s), s_s[None])
```

```python
import functools
import math

import numpy as np
import jax
import jax.numpy as jnp
from jax import lax
from jax.experimental import pallas as pl
from jax.experimental.pallas import tpu as pltpu

F32, BF16, I32, U32 = jnp.float32, jnp.bfloat16, jnp.int32, jnp.uint32

D_MODEL = 1024
HEADS = 4
HEAD_W = 128
DA_HEAD = 64
GROUP_W = HEADS * HEAD_W
HG_COLS = 4 * GROUP_W
IN_WIDTH = HG_COLS + 3 * GROUP_W
N_EXPERTS = 32
TOP_K = 4
D_FF = 1024
SWIGLU_ALPHA = 1.702
SWIGLU_LIMIT = 7.0
DEEPNORM_ALPHA = 2.0 ** 0.25
LN_EPS = 1e-5
RMS_EPS = 1e-6
LAM_INIT = 0.8 - 0.6 * math.exp(-0.3 * 0)
LOG2E = math.log2(math.e)
ALIBI_SLOPES = tuple(2.0 ** (-8.0 * (h + 1) / HEADS) for h in range(HEADS))
PAGE_SIZE = 128
EXP_CLAMP = 80.0
SKIP_LOG2 = 151.0

VMEM_LIMIT = 56 * 1024 * 1024
MOE_BLOCK = 256
HG_CHUNK = 128
ATTN_BLOCK = 1024
INPROJ_BLOCK = 512
ROW_BLOCK = 512
TOK_BLOCK = 128


def _cparams(sem):
    return pltpu.CompilerParams(dimension_semantics=sem, vmem_limit_bytes=VMEM_LIMIT)


def _sigmoid(x):
    return 1.0 / (1.0 + jnp.exp(-x))


def _layer_norm(x, g, b):
    mu = jnp.mean(x, -1, keepdims=True)
    xc = x - mu
    var = jnp.mean(xc * xc, -1, keepdims=True)
    return xc * lax.rsqrt(var + LN_EPS) * g + b


def _dot(a, b):
    return jnp.dot(a, b, preferred_element_type=F32)


def _dot_nt(a, b):
    return lax.dot_general(a, b, (((1,), (1,)), ((), ())), preferred_element_type=F32)


def _dot_tn(a, b):
    return lax.dot_general(a, b, (((0,), (0,)), ((), ())), preferred_element_type=F32)


ROW_SHAPE = (1, D_MODEL)


def _rows_copy(mat_ref, rows_hbm, row0, n, sem, to_hbm):
    rows = rows_hbm.at[pl.ds(row0, n), 0, :]
    return pltpu.make_async_copy(mat_ref, rows, sem) if to_hbm else pltpu.make_async_copy(rows, mat_ref, sem)


def _ada_kernel(c_ref, w_ref, b_ref, o_ref):
    c = c_ref[...]
    a = (c * _sigmoid(c)).astype(BF16)
    o_ref[...] = _dot(a, w_ref[...].astype(BF16)) + b_ref[...]


def _ada(c, w_ada, b_ada):
    rows, d = c.shape
    n = w_ada.shape[1]
    bn = 1536
    return pl.pallas_call(
        _ada_kernel,
        grid=(n // bn,),
        in_specs=[pl.BlockSpec((rows, d), lambda j: (0, 0)),
                  pl.BlockSpec((d, bn), lambda j: (0, j)),
                  pl.BlockSpec((1, bn), lambda j: (0, j))],
        out_specs=pl.BlockSpec((rows, bn), lambda j: (0, j)),
        out_shape=jax.ShapeDtypeStruct((rows, n), F32),
        compiler_params=_cparams(("arbitrary",)),
        name="ada",
    )(c, w_ada, b_ada.reshape(1, n))


def _inproj_common(x_ref, g_ref, b_ref, sc_ref, sh_ref, w_ref, zhg_ref, kf_ref, vf_ref):
    x0 = _layer_norm(x_ref[...], g_ref[...], b_ref[...])
    h = (x0 * (1.0 + sc_ref[...]) + sh_ref[...]).astype(BF16)
    zhg_ref[...] = _dot(h, w_ref[:, 0:HG_COLS])
    c0 = HG_COLS + GROUP_W
    k = _dot(h, w_ref[:, c0:c0 + GROUP_W])
    v = _dot(h, w_ref[:, c0 + GROUP_W:c0 + 2 * GROUP_W])
    for hd in range(HEADS):
        kf_ref[:, hd, :] = k[:, hd * HEAD_W:(hd + 1) * HEAD_W]
        vf_ref[:, hd, :] = v[:, hd * HEAD_W:(hd + 1) * HEAD_W]
    return h, k


def _inproj_prompt_kernel(x_ref, g_ref, b_ref, sc_ref, sh_ref, w_ref, wqt_ref, wvt_ref,
                          zhg_ref, kf_ref, vf_ref, kb_ref, qt_ref, vt_ref):
    h, k = _inproj_common(x_ref, g_ref, b_ref, sc_ref, sh_ref, w_ref, zhg_ref, kf_ref, vf_ref)
    kb_ref[...] = k.astype(BF16)
    qt_ref[...] = (_dot_nt(wqt_ref[...], h) * (DA_HEAD ** -0.5 * LOG2E)).astype(BF16)
    vt_ref[...] = _dot_nt(wvt_ref[...], h).astype(BF16)


def _inproj_sample_kernel(x_ref, g_ref, b_ref, sc_ref, sh_ref, w_ref, zhg_ref, kf_ref, vf_ref, q_ref):
    h, _ = _inproj_common(x_ref, g_ref, b_ref, sc_ref, sh_ref, w_ref, zhg_ref, kf_ref, vf_ref)
    q_ref[...] = (_dot(h, w_ref[:, HG_COLS:HG_COLS + GROUP_W]) * (DA_HEAD ** -0.5)).astype(BF16)


def _mod_spec(mod, bm):
    if mod.shape[1] == 1:
        return pl.BlockSpec((None, 1, mod.shape[2]), lambda g, i: (g, 0, 0))
    return pl.BlockSpec((None, bm, mod.shape[2]), lambda g, i: (g, i, 0))


def _inproj(x, sc, sh, ln_g, ln_b, w_in_bf, bm, transposed=None):
    G, R, D = x.shape
    nb = R // bm
    row = lambda w: pl.BlockSpec((None, bm, w), lambda g, i: (g, i, 0))
    full = lambda a: pl.BlockSpec(a.shape, lambda g, i: (0,) * a.ndim)
    heads = pl.BlockSpec((None, bm, HEADS, HEAD_W), lambda g, i: (g, i, 0, 0))
    heads_shape = jax.ShapeDtypeStruct((G, R, HEADS, HEAD_W), F32)
    sds = lambda w, dt: jax.ShapeDtypeStruct((G, R, w), dt)
    args = [x, ln_g, ln_b, sc, sh, w_in_bf]
    in_specs = [row(D), full(ln_g), full(ln_b), _mod_spec(sc, bm), _mod_spec(sh, bm), full(w_in_bf)]
    out_specs = [row(HG_COLS), heads, heads, row(GROUP_W)]
    out_shape = [sds(HG_COLS, F32), heads_shape, heads_shape, sds(GROUP_W, BF16)]
    body = _inproj_sample_kernel
    if transposed is not None:
        wqt, wvt, tblk = transposed
        per = tblk // bm
        tr = pl.BlockSpec((None, None, GROUP_W, bm), lambda g, i: (g, i // per, 0, i % per))
        tr_shape = jax.ShapeDtypeStruct((G, R // tblk, GROUP_W, tblk), BF16)
        args += [wqt, wvt]
        in_specs += [full(wqt), full(wvt)]
        out_specs += [tr, tr]
        out_shape += [tr_shape, tr_shape]
        body = _inproj_prompt_kernel
    return pl.pallas_call(
        body,
        grid=(G, nb),
        in_specs=in_specs,
        out_specs=out_specs,
        out_shape=out_shape,
        compiler_params=_cparams(("arbitrary", "arbitrary")),
        name="inproj",
    )(*args)


def _hgrn_gates(zq, zf, lb):
    q = zq * _sigmoid(zq)
    f = lb + (1.0 - lb) * _sigmoid(zf)
    k = (1.0 - lb) * _sigmoid(-zf)
    return q, jnp.log(f), k


def _hgrn_kernel(z_ref, lb_ref, ng_ref, lvl_ref, tri_ref, o_ref, sfin_ref, st_ref, *, C):
    t = pl.program_id(1)

    @pl.when(t == 0)
    def _():
        st_ref[...] = jnp.zeros_like(st_ref)

    lvl = lvl_ref[...]
    tri = tri_ref[...]
    n_levels = int(math.log2(C)) - 3
    ng = ng_ref[...]
    for h in range(HEADS):
        cs = slice(h * HEAD_W, (h + 1) * HEAD_W)
        zq = z_ref[:, h * HEAD_W:(h + 1) * HEAD_W]
        zf = z_ref[:, GROUP_W + h * HEAD_W:GROUP_W + (h + 1) * HEAD_W]
        v = z_ref[:, 2 * GROUP_W + h * HEAD_W:2 * GROUP_W + (h + 1) * HEAD_W].astype(BF16)
        zg = z_ref[:, 3 * GROUP_W + h * HEAD_W:3 * GROUP_W + (h + 1) * HEAD_W]
        q, g, k = _hgrn_gates(zq, zf, lb_ref[:, cs])
        g1 = g.astype(BF16)
        r1 = g - g1.astype(F32)
        g2 = r1.astype(BF16)
        g3 = (r1 - g2.astype(F32)).astype(BF16)
        b = _dot(tri, g1) + _dot(tri, g2) + _dot(tri, g3)

        b8 = b.reshape(C // 8, 8, HEAD_W)
        bmid = jnp.broadcast_to(b8[:, 3:4, :], b8.shape).reshape(C, HEAD_W)
        e = jnp.clip(b - bmid, -EXP_CLAMP, EXP_CLAMP)
        a = jnp.where(lvl == 0, _dot_nt((q * jnp.exp(e)).astype(BF16), (k * jnp.exp(-e)).astype(BF16)), 0.0)
        for li in range(1, n_levels + 1):
            m = 4 << li
            bb = b.reshape(C // (2 * m), 2 * m, HEAD_W)
            d = b - jnp.broadcast_to(bb[:, m - 1:m, :], bb.shape).reshape(C, HEAD_W)
            qs = (q * jnp.exp(jnp.minimum(d, 0.0))).astype(BF16)
            ks = (k * jnp.exp(jnp.minimum(-d, 0.0))).astype(BF16)
            a = jnp.where(lvl == li, _dot_nt(qs, ks), a)

        st = st_ref[h]
        o = _dot(a.astype(BF16), v) + _dot_nt((q * jnp.exp(b)).astype(BF16), st.astype(BF16))
        b_last = b[C - 1:C, :]
        kd = (k * jnp.exp(b_last - b)).astype(BF16)
        st_ref[h] = st * jnp.exp(b_last) + _dot_tn(v, kd)

        ms = jnp.mean(o * o, -1, keepdims=True)
        o_ref[:, cs] = (o * lax.rsqrt(ms + RMS_EPS) * ng * (zg * _sigmoid(zg))).astype(BF16)

    @pl.when(t == pl.num_programs(1) - 1)
    def _():
        for h in range(HEADS):
            sfin_ref[h] = st_ref[h].T


def _hgrn_level_table(C):
    t = np.arange(C)[:, None]
    s = np.arange(C)[None, :]
    x = t ^ s
    lvl = np.zeros((C, C), np.int32)
    m = 8
    while m < C:
        lvl += (x >= m).astype(np.int32)
        m *= 2
    return np.where(s <= t, lvl, -1).astype(np.int32)


def _hgrn_prompt(zhg, lb, norm_g):
    B, T, _ = zhg.shape
    C = HG_CHUNK
    lvl = jnp.asarray(_hgrn_level_table(C))
    tri = jnp.asarray(np.tril(np.ones((C, C), np.float32)), BF16)
    full = lambda a: pl.BlockSpec(a.shape, lambda b, t: (0,) * a.ndim)
    return pl.pallas_call(
        functools.partial(_hgrn_kernel, C=C),
        grid=(B, T // C),
        in_specs=[pl.BlockSpec((None, C, HG_COLS), lambda b, t: (b, t, 0)), full(lb), full(norm_g), full(lvl), full(tri)],
        out_specs=[pl.BlockSpec((None, C, GROUP_W), lambda b, t: (b, t, 0)),
                   pl.BlockSpec((None, HEADS, HEAD_W, HEAD_W), lambda b, t: (b, 0, 0, 0))],
        out_shape=[jax.ShapeDtypeStruct((B, T, GROUP_W), BF16),
                   jax.ShapeDtypeStruct((B, HEADS, HEAD_W, HEAD_W), F32)],
        scratch_shapes=[pltpu.VMEM((HEADS, HEAD_W, HEAD_W), F32)],
        compiler_params=_cparams(("arbitrary", "arbitrary")),
        name="hgrn_prompt",
    )(zhg, lb, norm_g, lvl, tri)


def _hgrn_step_kernel(z_ref, s_ref, lb_ref, ng_ref, o_ref, so_ref, *, G):
    ng = ng_ref[...]
    for h in range(HEADS):
        cs = slice(h * HEAD_W, (h + 1) * HEAD_W)
        zq = z_ref[:, h * HEAD_W:(h + 1) * HEAD_W]
        zf = z_ref[:, GROUP_W + h * HEAD_W:GROUP_W + (h + 1) * HEAD_W]
        v = z_ref[:, 2 * GROUP_W + h * HEAD_W:2 * GROUP_W + (h + 1) * HEAD_W]
        zg = z_ref[:, 3 * GROUP_W + h * HEAD_W:3 * GROUP_W + (h + 1) * HEAD_W]
        lb = lb_ref[:, cs]
        q = zq * _sigmoid(zq)
        f = lb + (1.0 - lb) * _sigmoid(zf)
        k = (1.0 - lb) * _sigmoid(-zf)
        qT, fT, kT = q.T, f.T, k.T
        rows = []
        for j in range(G):
            s_new = fT[:, j:j + 1] * s_ref[j, h] + kT[:, j:j + 1] * v[j:j + 1, :]
            so_ref[j, h] = s_new
            rows.append(jnp.sum(s_new * qT[:, j:j + 1], axis=0, keepdims=True))
        o = jnp.concatenate(rows, axis=0)
        ms = jnp.mean(o * o, -1, keepdims=True)
        o_ref[:, cs] = (o * lax.rsqrt(ms + RMS_EPS) * ng * (zg * _sigmoid(zg))).astype(BF16)


def _hgrn_sample(zhg, state, lb, norm_g):
    N = zhg.shape[0]
    G = 8
    full = lambda a: pl.BlockSpec(a.shape, lambda i: (0,) * a.ndim)
    st_spec = pl.BlockSpec((G, HEADS, HEAD_W, HEAD_W), lambda i: (i, 0, 0, 0))
    return pl.pallas_call(
        functools.partial(_hgrn_step_kernel, G=G),
        grid=(N // G,),
        in_specs=[pl.BlockSpec((G, HG_COLS), lambda i: (i, 0)), st_spec, full(lb), full(norm_g)],
        out_specs=[pl.BlockSpec((G, GROUP_W), lambda i: (i, 0)), st_spec],
        out_shape=[jax.ShapeDtypeStruct((N, GROUP_W), BF16), jax.ShapeDtypeStruct(state.shape, F32)],
        compiler_params=_cparams(("arbitrary",)),
        name="hgrn_sample",
    )(zhg, state, lb, norm_g)


def _attn_kernel(qt_ref, k_ref, vt_ref, kpm_ref, sl_ref, feat_ref, lam_ref, gcol_ref, o_ref, m_ref, l_ref, acc_ref,
                 *, blk):
    qi = pl.program_id(2)
    row = lax.broadcasted_iota(I32, (HEAD_W, blk), 0)
    qt = qt_ref[...].astype(F32)
    a_hi = sl_ref[0:1, :]
    a_lo = sl_ref[1:2, :]
    slope2 = sl_ref[2:3, :]
    def slope_rows(r0):
        in_rows = (row >= r0) & (row < r0 + 4)
        return jnp.where(in_rows, jnp.where((row - r0) % 2 == 0, a_hi, a_lo), 0.0)

    q_aug = [jnp.where(row < DA_HEAD, qt, slope_rows(DA_HEAD)).astype(BF16),
             jnp.where(row >= DA_HEAD, qt, slope_rows(0)).astype(BF16)]
    keep = [feat_ref[0] > 0, feat_ref[1] > 0]
    feats = [feat_ref[2], feat_ref[3]]
    m_ref[...] = jnp.full_like(m_ref, -jnp.inf)
    l_ref[...] = jnp.zeros_like(l_ref)
    acc_ref[...] = jnp.zeros_like(acc_ref)

    def block(kj, masked):
        k_start = pl.multiple_of(kj * blk, blk)
        kb = k_ref[pl.ds(k_start, blk), :]
        vt = vt_ref[kj]
        off = slope2 * jnp.full((1, blk), k_start - qi * blk, I32).astype(F32)
        if masked:
            ok = lax.broadcasted_iota(I32, (blk, 1), 0) <= lax.broadcasted_iota(I32, (1, blk), 1)
        for mi in range(2):
            st = _dot(jnp.where(keep[mi], kb, feats[mi]), q_aug[mi])
            if masked:
                st = jnp.where(ok, st, -jnp.inf)
            m_prev = m_ref[mi]
            m_new = jnp.maximum(m_prev, jnp.max(st, axis=0, keepdims=True) + off)
            p = jnp.exp2(st - (m_new - off))
            alpha = jnp.exp2(m_prev - m_new)
            l_ref[mi] = alpha * l_ref[mi] + jnp.sum(p, axis=0, keepdims=True)
            acc_ref[mi] = alpha * acc_ref[mi] + _dot(vt, p.astype(BF16))
            m_ref[mi] = m_new

    block(qi, True)

    nblk = kpm_ref.shape[1]
    jlane = lax.broadcasted_iota(I32, (1, nblk), 1)
    far = slope2[:, 0:1] * ((jlane - qi) * blk + (blk - 1)).astype(F32)
    need = jlane < 0
    for mi in range(2):
        qm = qt[mi * DA_HEAD:(mi + 1) * DA_HEAD, :]
        qn = jnp.sqrt(jnp.max(jnp.sum(qm * qm, axis=0, keepdims=True), axis=1, keepdims=True))
        m_lo = jnp.min(m_ref[mi], axis=1, keepdims=True)
        need = need | (1.02 * qn * kpm_ref[mi:mi + 1, :] + 1.0 + far - m_lo >= -SKIP_LOG2)
    n_visit = jnp.sum((need & (jlane < qi)).astype(I32))

    def body(kj, c):
        block(kj, False)
        return c

    lax.fori_loop(qi - n_visit, qi, body, 0)

    ot = acc_ref[0] * (1.0 / l_ref[0]) - lam_ref[...] * (acc_ref[1] * (1.0 / l_ref[1]))
    ms = jnp.mean(ot * ot, axis=0, keepdims=True)
    o_ref[...] = (ot * lax.rsqrt(ms + RMS_EPS) * gcol_ref[...]).T.astype(BF16)


def _attn_prompt(qt, k, vt, lam, subln_g):
    B, nblk, _, blk = qt.shape
    T = k.shape[1]
    bf = lambda x: np.asarray(x, np.float32).astype(BF16).astype(np.float64)
    a = np.asarray(ALIBI_SLOPES, np.float64) * LOG2E
    a_hi = bf(a)
    a_lo = bf(a - a_hi)
    sl = np.zeros((HEADS, 8, blk), np.float32)
    sl[:, 0, :], sl[:, 1, :], sl[:, 2, :] = a_hi[:, None], a_lo[:, None], a.astype(np.float32)[:, None]
    lane = np.arange(HEAD_W)[None, :]
    r = np.arange(blk)[:, None]
    assert blk <= 256 * 256

    def pos_feat(l0):
        return (np.where((lane == l0) | (lane == l0 + 1), r // 256 * 256, 0)
                + np.where((lane == l0 + 2) | (lane == l0 + 3), r % 256, 0)).astype(np.float32)

    feat = np.stack([np.broadcast_to(lane < DA_HEAD, (blk, HEAD_W)).astype(np.float32),
                     np.broadcast_to(lane >= DA_HEAD, (blk, HEAD_W)).astype(np.float32),
                     pos_feat(DA_HEAD), pos_feat(0)])
    feat = jnp.asarray(feat, BF16)
    lam_row = jnp.full((1, blk), lam, F32)
    gcol = jnp.broadcast_to((subln_g.reshape(HEAD_W, 1) * (1.0 - LAM_INIT)), (HEAD_W, blk))
    kf = k.astype(F32).reshape(B, nblk, blk, HEADS, 2, DA_HEAD)
    kpm = lax.cummax(jnp.sqrt(jnp.max(jnp.sum(kf * kf, axis=-1), axis=2)), axis=1).transpose(0, 2, 3, 1)
    full = lambda x: pl.BlockSpec(x.shape, lambda b, h, i: (0,) * x.ndim)
    return pl.pallas_call(
        functools.partial(_attn_kernel, blk=blk),
        grid=(B, HEADS, nblk),
        in_specs=[pl.BlockSpec((None, None, HEAD_W, blk), lambda b, h, i: (b, i, h, 0)),
                  pl.BlockSpec((None, T, HEAD_W), lambda b, h, i: (b, 0, h)),
                  pl.BlockSpec((None, nblk, HEAD_W, blk), lambda b, h, i: (b, 0, h, 0)),
                  pl.BlockSpec((None, None, 2, nblk), lambda b, h, i: (b, h, 0, 0)),
                  pl.BlockSpec((None, 8, blk), lambda b, h, i: (h, 0, 0)),
                  full(feat), full(lam_row), full(gcol)],
        out_specs=pl.BlockSpec((None, blk, HEAD_W), lambda b, h, i: (b, i, h)),
        out_shape=jax.ShapeDtypeStruct((B, T, GROUP_W), BF16),
        scratch_shapes=[pltpu.VMEM((2, 1, blk), F32), pltpu.VMEM((2, 1, blk), F32), pltpu.VMEM((2, HEAD_W, blk), F32)],
        compiler_params=_cparams(("arbitrary", "arbitrary", "arbitrary")),
        name="attn_prompt",
    )(qt, k, vt, kpm, jnp.asarray(sl), feat, lam_row, gcol)


def _attn_decode_kernel(pt_ref, q_ref, kn_ref, vn_ref, bias_ref, lam_ref, g_ref, ck_hbm, cv_hbm, o_ref,
                        kbuf, vbuf, sems, *, n_pages):
    i = pl.program_id(0)
    slot = i % 2

    def page_copies(seq, s):
        for pg in range(n_pages):
            page = pt_ref[seq, pg]
            for h in range(HEADS):
                dst = (s, h, pl.ds(pg * PAGE_SIZE, PAGE_SIZE))
                yield pltpu.make_async_copy(ck_hbm.at[0, page, :, h, :], kbuf.at[dst], sems.at[0, s])
                yield pltpu.make_async_copy(cv_hbm.at[0, page, :, h, :], vbuf.at[dst], sems.at[1, s])

    @pl.when(i == 0)
    def _():
        for cp in page_copies(0, 0):
            cp.start()

    @pl.when(i + 1 < pl.num_programs(0))
    def _():
        for cp in page_copies(i + 1, 1 - slot):
            cp.start()

    for cp in page_copies(i, slot):
        cp.wait()

    r8 = lax.broadcasted_iota(I32, (8, GROUP_W), 0)
    c8 = lax.broadcasted_iota(I32, (8, GROUP_W), 1)
    qmat = jnp.where(c8 // DA_HEAD == r8, jnp.broadcast_to(q_ref[...].astype(F32), (8, GROUP_W)), 0.0)
    qmat_bf = qmat.astype(BF16)
    s = bias_ref[...]
    for h in range(HEADS):
        s = s + _dot_nt(qmat_bf[:, h * HEAD_W:(h + 1) * HEAD_W], kbuf[slot, h].astype(BF16))
    kn = kn_ref[...].astype(BF16).astype(F32)
    s_self = jnp.sum(qmat_bf.astype(F32) * kn, axis=-1, keepdims=True)
    m = jnp.maximum(jnp.max(s, -1, keepdims=True), s_self)
    p = jnp.exp(s - m)
    p_self = jnp.exp(s_self - m)
    inv_l = 1.0 / (jnp.sum(p, -1, keepdims=True) + p_self)
    coef = jnp.where(lax.broadcasted_iota(I32, (8, 1), 0) % 2 == 0, 1.0, -lam_ref[:, 0:1]) * inv_l
    w = (p * coef).astype(BF16)
    accs = [_dot(w, vbuf[slot, h].astype(BF16)) for h in range(HEADS)]
    acc = jnp.concatenate(accs, axis=1) + (p_self * coef) * vn_ref[...]
    o = jnp.sum(jnp.where(c8 // HEAD_W == r8 // 2, acc, 0.0), axis=0, keepdims=True)
    outs = []
    for h in range(HEADS):
        oh = o[:, h * HEAD_W:(h + 1) * HEAD_W]
        ms = jnp.mean(oh * oh, -1, keepdims=True)
        outs.append(oh * lax.rsqrt(ms + RMS_EPS) * g_ref[...] * (1.0 - LAM_INIT))
    o_ref[...] = jnp.concatenate(outs, axis=1).astype(BF16)


def _attn_sample(q, k_new, v_new, cache_k, cache_v, page_table, lam_row, subln_g):
    N, n_pages = page_table.shape
    past = n_pages * PAGE_SIZE
    kpos = np.arange(past, dtype=np.float32)[None, :]
    slope_rows = np.repeat(np.asarray(ALIBI_SLOPES, np.float32), 2)[:, None]
    bias = jnp.asarray(-slope_rows * (past - kpos))
    row = pl.BlockSpec((None, 1, GROUP_W), lambda i, pt: (i, 0, 0))
    full = lambda a: pl.BlockSpec(a.shape, lambda i, pt: (0,) * a.ndim)
    anyspec = pl.BlockSpec(memory_space=pl.ANY)
    grid_spec = pltpu.PrefetchScalarGridSpec(
        num_scalar_prefetch=1,
        grid=(N,),
        in_specs=[row, row, row, full(bias), full(lam_row), full(subln_g), anyspec, anyspec],
        out_specs=row,
        scratch_shapes=[pltpu.VMEM((2, HEADS, past, HEAD_W), F32), pltpu.VMEM((2, HEADS, past, HEAD_W), F32),
                        pltpu.SemaphoreType.DMA((2, 2))],
    )
    r3 = lambda a: a.reshape(N, 1, GROUP_W)
    out = pl.pallas_call(
        functools.partial(_attn_decode_kernel, n_pages=n_pages),
        grid_spec=grid_spec,
        out_shape=jax.ShapeDtypeStruct((N, 1, GROUP_W), BF16),
        compiler_params=_cparams(("arbitrary",)),
        name="attn_sample",
    )(page_table, r3(q), r3(k_new), r3(v_new), bias, lam_row, subln_g, cache_k, cache_v)
    return out.reshape(N, GROUP_W)


def _outproj_kernel(hg_ref, da_ref, x_ref, lg_ref, lb_ref, g1_ref, sc2_ref, sh2_ref, w_ref, l1g_ref, l1b_ref,
                    wrh_ref, wrl_ref, br_ref, u_ref, cin_ref,
                    x1_ref, h2_hbm, idx_ref, tw_ref, rank_ref, cout_ref, run_ref, hbuf, hsem, *, n_steps):
    step = pl.program_id(0) * pl.num_programs(1) + pl.program_id(1)
    slot = step % 2
    bm = x_ref.shape[0]
    h2_copies = lambda st, s: [_rows_copy(hbuf.at[s], h2_hbm, st * bm, bm, hsem.at[s], True)]

    @pl.when(step == 0)
    def _():
        run_ref[...] = cin_ref[...]

    x0 = _layer_norm(x_ref[...], lg_ref[...], lb_ref[...])
    mix = _dot(hg_ref[...], w_ref[0:GROUP_W, :]) + _dot(da_ref[...], w_ref[GROUP_W:2 * GROUP_W, :])
    x1 = _layer_norm(DEEPNORM_ALPHA * x0 + (1.0 + g1_ref[...]) * mix, l1g_ref[...], l1b_ref[...])
    x1_ref[...] = x1
    h2 = x1 * (1.0 + sc2_ref[...]) + sh2_ref[...]

    @pl.when(step >= 2)
    def _():
        for cp in h2_copies(step - 2, slot):
            cp.wait()

    hbuf[slot] = h2
    for cp in h2_copies(step, slot):
        cp.start()

    @pl.when(step == n_steps - 1)
    def _():
        tail = h2_copies(step, slot)
        if n_steps > 1:
            tail = h2_copies(step - 1, 1 - slot) + tail
        for cp in tail:
            cp.wait()

    hi = h2.astype(BF16)
    lo = (h2 - hi.astype(F32)).astype(BF16)
    wrh = wrh_ref[...]
    logits = _dot_nt(wrh, hi) + _dot_nt(wrh, lo) + _dot_nt(wrl_ref[...], hi) + br_ref[...]

    n_e, bm = logits.shape
    rows = lax.broadcasted_iota(I32, (n_e, bm), 0).astype(F32)
    vals, sels = [], []
    work = logits
    for kk in range(TOP_K):
        mx = jnp.max(work, axis=0, keepdims=True)
        ix = jnp.min(jnp.where(work == mx, rows, float(n_e)), axis=0, keepdims=True)
        sel = rows == ix
        idx_ref[kk:kk + 1, :] = ix.astype(I32)
        vals.append(mx)
        sels.append(sel)
        work = jnp.where(sel, -jnp.inf, work)
    es = [jnp.exp(vv - vals[0]) for vv in vals]
    inv = 1.0 / (es[0] + es[1] + es[2] + es[3])
    for kk in range(TOP_K):
        tw_ref[kk:kk + 1, :] = es[kk] * inv

    base = run_ref[...]
    for kk in range(TOP_K):
        oh = jnp.where(sels[kk], 1.0, 0.0)
        before = base + _dot(oh.astype(BF16), u_ref[...])
        rank_ref[kk:kk + 1, :] = jnp.sum(jnp.where(sels[kk], before, 0.0), axis=0, keepdims=True).astype(I32)
        base = base + jnp.sum(oh, axis=1, keepdims=True)
    run_ref[...] = base
    cout_ref[...] = base


def _outproj(hg, da, x, mods, consts, counts_in, bm):
    G, R, D = x.shape
    g1, sc2, sh2 = mods
    ln_g, ln_b, w_out_bf, l1g, l1b, wrh, wrl, br = consts
    nb = R // bm
    n_tok = G * R
    u = jnp.asarray(np.triu(np.ones((bm, bm), np.float32), 1), BF16)
    row = lambda w: pl.BlockSpec((None, bm, w), lambda g, i: (g, i, 0))
    full = lambda a: pl.BlockSpec(a.shape, lambda g, i: (0,) * a.ndim)
    tok_lanes = pl.BlockSpec((TOP_K, bm), lambda g, i: (0, g * nb + i))
    return pl.pallas_call(
        functools.partial(_outproj_kernel, n_steps=G * nb),
        grid=(G, nb),
        in_specs=[row(GROUP_W), row(GROUP_W), row(D), full(ln_g), full(ln_b),
                  _mod_spec(g1, bm), _mod_spec(sc2, bm), _mod_spec(sh2, bm),
                  full(w_out_bf), full(l1g), full(l1b), full(wrh), full(wrl), full(br), full(u), full(counts_in)],
        out_specs=[row(D), pl.BlockSpec(memory_space=pl.ANY), tok_lanes, tok_lanes, tok_lanes, full(counts_in)],
        out_shape=[jax.ShapeDtypeStruct((G, R, D), F32),
                   jax.ShapeDtypeStruct((n_tok,) + ROW_SHAPE, F32),
                   jax.ShapeDtypeStruct((TOP_K, n_tok), I32),
                   jax.ShapeDtypeStruct((TOP_K, n_tok), F32),
                   jax.ShapeDtypeStruct((TOP_K, n_tok), I32),
                   jax.ShapeDtypeStruct(counts_in.shape, F32)],
        scratch_shapes=[pltpu.VMEM(counts_in.shape, F32), pltpu.VMEM((2, bm, D), F32), pltpu.SemaphoreType.DMA((2,))],
        compiler_params=_cparams(("arbitrary", "arbitrary")),
        name="outproj",
    )(hg, da, x, ln_g, ln_b, g1, sc2, sh2, w_out_bf, l1g, l1b, wrh, wrl, br, u, counts_in)


def _dispatch_kernel(pe_ref, nu_ref, dest_ref, h2p_ref, h2s_ref, xb_hbm, zbuf, zsem, sem, *, bt, n_prompt_steps, nb):
    i = pl.program_id(0)
    bm = MOE_BLOCK

    @pl.when(i == 0)
    def _():
        zbuf[...] = jnp.zeros_like(zbuf)
        zero_block = lambda row0: pltpu.make_async_copy(zbuf, xb_hbm.at[pl.ds(row0, bm)], zsem)
        has_rows = [pe_ref[e] > (pe_ref[e - 1] if e else 0) for e in range(N_EXPERTS)]
        for e in range(N_EXPERTS):
            @pl.when(has_rows[e])
            def _():
                zero_block(pe_ref[e] - bm).start()

        def tail_start(j, c):
            zero_block(j * bm).start()
            return c

        def tail_wait(j, c):
            zero_block(j * bm).wait()
            return c

        lax.fori_loop(nu_ref[0], nb, tail_start, 0)
        for e in range(N_EXPERTS):
            @pl.when(has_rows[e])
            def _():
                zero_block(pe_ref[e] - bm).wait()
        lax.fori_loop(nu_ref[0], nb, tail_wait, 0)

    def scatter(h2_ref):
        def body(t, c):
            for kk in range(TOP_K):
                pltpu.make_async_copy(h2_ref.at[t], xb_hbm.at[dest_ref[kk, t]], sem).start(priority=kk % 2)
            return c

        lax.fori_loop(0, bt, body, 0, unroll=4)
        for kk in range(TOP_K):
            pltpu.make_async_copy(h2_ref, xb_hbm.at[pl.ds(0, bt)], sem).wait()

    @pl.when(i < n_prompt_steps)
    def _():
        scatter(h2p_ref)

    @pl.when(i >= n_prompt_steps)
    def _():
        scatter(h2s_ref)


def _dispatch(pad_end, n_used, dest, h2_p, h2_s, n_rows):
    bt = TOK_BLOCK
    n_p, n_s = h2_p.shape[0] // bt, h2_s.shape[0] // bt
    nb = n_rows // MOE_BLOCK
    grid_spec = pltpu.PrefetchScalarGridSpec(
        num_scalar_prefetch=2,
        grid=(n_p + n_s,),
        in_specs=[pl.BlockSpec((TOP_K, bt), lambda i, pe, nu: (0, i), memory_space=pltpu.SMEM),
                  pl.BlockSpec((bt,) + ROW_SHAPE, lambda i, pe, nu: (jnp.minimum(i, n_p - 1), 0, 0)),
                  pl.BlockSpec((bt,) + ROW_SHAPE, lambda i, pe, nu: (jnp.maximum(i - n_p, 0), 0, 0))],
        out_specs=pl.BlockSpec(memory_space=pl.ANY),
        scratch_shapes=[pltpu.VMEM((MOE_BLOCK,) + ROW_SHAPE, F32), pltpu.SemaphoreType.DMA(()),
                        pltpu.SemaphoreType.DMA(())],
    )
    return pl.pallas_call(
        functools.partial(_dispatch_kernel, bt=bt, n_prompt_steps=n_p, nb=nb),
        grid_spec=grid_spec,
        out_shape=jax.ShapeDtypeStruct((n_rows,) + ROW_SHAPE, F32),
        compiler_params=_cparams(("arbitrary",)),
        name="dispatch",
    )(pad_end, n_used, dest, h2_p, h2_s)


def _experts_kernel(be_ref, nu_ref, nxt_ref, par_ref, xb_hbm, wu_hbm, bu_ref, wd_hbm, bd_ref, yb_hbm,
                    wu_bf, wd_bf, wu_f32, wd_f32, xbuf, ybuf, sems, wsems, *, bm):
    j = pl.program_id(0)
    nb = pl.num_programs(0)
    slot = j % 2
    nu = nu_ref[0]
    active = j < nu
    x_copies = lambda blk, s: [_rows_copy(xbuf.at[s], xb_hbm, blk * bm, bm, sems.at[0, s], False)]
    y_copies = lambda blk, s: [_rows_copy(ybuf.at[s], yb_hbm, blk * bm, bm, sems.at[1, s], True)]
    w_copies = lambda e, s: [pltpu.make_async_copy(wu_hbm.at[e], wu_f32.at[s], wsems.at[0, s]),
                             pltpu.make_async_copy(wd_hbm.at[e], wd_f32.at[s], wsems.at[1, s])]

    @pl.when(j == 0)
    def _():
        for cp in x_copies(0, 0) + w_copies(be_ref[0], par_ref[0]):
            cp.start()

    @pl.when(j + 1 < nu)
    def _():
        for cp in x_copies(j + 1, 1 - slot):
            cp.start()

    @pl.when(j >= 2)
    def _():
        for cp in y_copies(j - 2, slot):
            cp.wait()

    @pl.when(active & ((j == 0) | (be_ref[j] != be_ref[jnp.maximum(j - 1, 0)])))
    def _():
        half = par_ref[j]
        for cp in w_copies(be_ref[j], half):
            cp.wait()
        wu_bf[...] = wu_f32[half].astype(BF16)
        wd_bf[...] = wd_f32[half].astype(BF16)

        @pl.when(nxt_ref[j] >= 0)
        def _():
            for cp in w_copies(nxt_ref[j], 1 - half):
                cp.start()

    @pl.when(active)
    def _():
        for cp in x_copies(j, slot):
            cp.wait()
        u = _dot(xbuf[slot].astype(BF16), wu_bf[...]) + bu_ref[...]
        glu = jnp.minimum(u[:, :D_FF], SWIGLU_LIMIT)
        lin = jnp.clip(u[:, D_FF:], -SWIGLU_LIMIT, SWIGLU_LIMIT)
        act = glu * _sigmoid(SWIGLU_ALPHA * glu) * (lin + 1.0)
        ybuf[slot] = _dot(act.astype(BF16), wd_bf[...]) + bd_ref[...]

    @pl.when(jnp.logical_not(active))
    def _():
        ybuf[slot] = jnp.zeros((bm, D_MODEL), F32)

    for cp in y_copies(j, slot):
        cp.start()

    @pl.when(j == nb - 1)
    def _():
        for cp in y_copies(j - 1, 1 - slot) + y_copies(j, slot):
            cp.wait()


def _experts(blk_exp, n_used, blk_next, blk_half, xb, w_up, b_up, w_down, b_down):
    n_rows = xb.shape[0]
    bm = MOE_BLOCK
    nb = n_rows // bm
    anyspec = pl.BlockSpec(memory_space=pl.ANY)
    grid_spec = pltpu.PrefetchScalarGridSpec(
        num_scalar_prefetch=4,
        grid=(nb,),
        in_specs=[anyspec, anyspec,
                  pl.BlockSpec((None, 1, 2 * D_FF), lambda j, be, nu, nx, hf: (be[j], 0, 0)),
                  anyspec,
                  pl.BlockSpec((None, 1, D_MODEL), lambda j, be, nu, nx, hf: (be[j], 0, 0))],
        out_specs=anyspec,
        scratch_shapes=[pltpu.VMEM((D_MODEL, 2 * D_FF), BF16), pltpu.VMEM((D_FF, D_MODEL), BF16),
                        pltpu.VMEM((2, D_MODEL, 2 * D_FF), F32), pltpu.VMEM((2, D_FF, D_MODEL), F32),
                        pltpu.VMEM((2, bm, D_MODEL), F32), pltpu.VMEM((2, bm, D_MODEL), F32),
                        pltpu.SemaphoreType.DMA((2, 2)), pltpu.SemaphoreType.DMA((2, 2))],
    )
    return pl.pallas_call(
        functools.partial(_experts_kernel, bm=bm),
        grid_spec=grid_spec,
        out_shape=jax.ShapeDtypeStruct((n_rows,) + ROW_SHAPE, F32),
        compiler_params=_cparams(("arbitrary",)),
        name="experts",
    )(blk_exp, n_used, blk_next, blk_half, xb, w_up, b_up.reshape(N_EXPERTS, 1, 2 * D_FF), w_down,
      b_down.reshape(N_EXPERTS, 1, D_MODEL))


def _combine_kernel(dcur_ref, dnext_ref, x1_ref, tw_ref, g2_ref, lg_ref, lb_ref, yb_hbm, o_ref, buf, sems, *, bm):
    i = pl.program_id(0) * pl.num_programs(1) + pl.program_id(1)
    n = pl.num_programs(0) * pl.num_programs(1)
    slot = i % 2

    def issue(dref, s):
        def body(t, c):
            for kk in range(TOP_K):
                pltpu.make_async_copy(yb_hbm.at[dref[kk, t]], buf.at[s, kk, pl.ds(t, 1), :],
                                      sems.at[s]).start(priority=kk % 2)
            return c
        lax.fori_loop(0, bm, body, 0, unroll=4)

    @pl.when(i == 0)
    def _():
        issue(dcur_ref, 0)

    @pl.when(i + 1 < n)
    def _():
        issue(dnext_ref, 1 - slot)

    for kk in range(TOP_K):
        pltpu.make_async_copy(yb_hbm.at[pl.ds(0, bm), 0, :], buf.at[slot, kk], sems.at[slot]).wait()

    tw = tw_ref[...]
    ff = tw[:, 0:1] * buf[slot, 0]
    for kk in range(1, TOP_K):
        ff = ff + tw[:, kk:kk + 1] * buf[slot, kk]
    o_ref[...] = _layer_norm(DEEPNORM_ALPHA * x1_ref[...] + (1.0 + g2_ref[...]) * ff, lg_ref[...], lb_ref[...])


def _combine(dest, tw_rows, x1, g2, ln_g, ln_b, yb, bm):
    G, R, D = x1.shape
    nb = R // bm
    n_blk = G * nb
    row = lambda w: pl.BlockSpec((None, bm, w), lambda g, i: (g, i, 0))
    full = lambda a: pl.BlockSpec(a.shape, lambda g, i: (0,) * a.ndim)
    cur = pl.BlockSpec((TOP_K, bm), lambda g, i: (0, g * nb + i), memory_space=pltpu.SMEM)
    nxt = pl.BlockSpec((TOP_K, bm), lambda g, i: (0, jnp.minimum(g * nb + i + 1, n_blk - 1)),
                       memory_space=pltpu.SMEM)
    return pl.pallas_call(
        functools.partial(_combine_kernel, bm=bm),
        grid=(G, nb),
        in_specs=[cur, nxt, row(D), pl.BlockSpec((bm, TOP_K), lambda g, i: (g * nb + i, 0)),
                  _mod_spec(g2, bm), full(ln_g), full(ln_b), pl.BlockSpec(memory_space=pl.ANY)],
        out_specs=row(D),
        out_shape=jax.ShapeDtypeStruct((G, R, D), F32),
        scratch_shapes=[pltpu.VMEM((2, TOP_K, bm, D), F32), pltpu.SemaphoreType.DMA((2,))],
        compiler_params=_cparams(("arbitrary", "arbitrary")),
        name="combine",
    )(dest, dest, x1, tw_rows, g2, ln_g, ln_b, yb)


def kernel(x_prompt, x_sample, c_prompt, c_sample, cache_k, cache_v, state_hgrn, page_table, ln_in_g, ln_in_b, w_ada, b_ada, w_in, hg_lb, hg_norm_g, da_lq1, da_lk1, da_lq2, da_lk2, da_subln_g, w_out, ln1_g, ln1_b, w_router, b_router, w_up, b_up, w_down, b_down, ln2_g, ln2_b):
    assert w_in.shape[0] == 1, "single-layer trunk"
    B, T, D = x_prompt.shape
    NS = x_sample.shape[0]
    n_prompt = B * T
    n_tot = n_prompt + NS
    r1 = lambda a: a.reshape(1, -1)

    lb = r1(jax.nn.softmax(hg_lb.astype(F32), axis=0)[0])
    lam = (jnp.exp(jnp.sum(da_lq1[0].astype(F32) * da_lk1[0].astype(F32)))
           - jnp.exp(jnp.sum(da_lq2[0].astype(F32) * da_lk2[0].astype(F32))) + LAM_INIT)
    lam_row = jnp.full((1, HEAD_W), lam, F32)
    ln_g, ln_b = r1(ln_in_g), r1(ln_in_b)
    w_in_bf = w_in[0].astype(BF16)
    wqt = w_in_bf[:, HG_COLS:HG_COLS + GROUP_W].T
    wvt = w_in_bf[:, HG_COLS + 2 * GROUP_W:].T
    w_out_bf = w_out[0].astype(BF16)
    wr_t = w_router[0].T
    wrh = wr_t.astype(BF16)
    wrl = (wr_t - wrh.astype(F32)).astype(BF16)
    br = b_router[0].reshape(N_EXPERTS, 1)
    norm_g, subln_g = r1(hg_norm_g[0]), r1(da_subln_g[0])

    n_c = B + NS
    c_all = jnp.concatenate([c_prompt, c_sample, jnp.zeros((-n_c % 8, D), F32)], axis=0)
    mod = _ada(c_all, w_ada[0], b_ada[0])
    mod_p = [mod[:B, j * D:(j + 1) * D].reshape(B, 1, D) for j in range(6)]
    mod_s = [mod[B:n_c, j * D:(j + 1) * D].reshape(1, NS, D) for j in range(6)]

    zhg_p, kf_p, vf_p, kb_p, qt_p, vt_p = _inproj(x_prompt, mod_p[1], mod_p[0], ln_g, ln_b, w_in_bf, INPROJ_BLOCK,
                                                  (wqt, wvt, ATTN_BLOCK))
    hg_p, s_p = _hgrn_prompt(zhg_p, lb, norm_g)
    da_p = _attn_prompt(qt_p, kb_p, vt_p, lam, subln_g)

    xs = x_sample.reshape(1, NS, D)
    zhg_s, kf_s, vf_s, q_s = _inproj(xs, mod_s[1], mod_s[0], ln_g, ln_b, w_in_bf, NS)
    hg_s, s_s = _hgrn_sample(zhg_s[0], state_hgrn[0], lb, norm_g)
    da_s = _attn_sample(q_s[0], kf_s.reshape(NS, GROUP_W), vf_s.reshape(NS, GROUP_W), cache_k, cache_v, page_table,
                        lam_row, subln_g)

    consts = (ln_g, ln_b, w_out_bf, r1(ln1_g[0]), r1(ln1_b[0]), wrh, wrl, br)
    counts0 = jnp.zeros((N_EXPERTS, 1), F32)
    x1_p, h2_p, idx_p, tw_p, rank_p, counts1 = _outproj(
        hg_p, da_p, x_prompt, (mod_p[2], mod_p[4], mod_p[3]), consts, counts0, ROW_BLOCK)
    x1_s, h2_s, idx_s, tw_s, rank_s, counts = _outproj(
        hg_s.reshape(1, NS, GROUP_W), da_s.reshape(1, NS, GROUP_W), xs, (mod_s[2], mod_s[4], mod_s[3]), consts,
        counts1, NS)

    cnt = counts[:, 0].astype(I32)
    padded = (cnt + MOE_BLOCK - 1) // MOE_BLOCK * MOE_BLOCK
    pad_end = jnp.cumsum(padded)
    pad_start = pad_end - padded
    e_ids = jnp.arange(N_EXPERTS, dtype=I32)

    def slot_of(idx, rank):
        return jnp.sum(jnp.where(idx[..., None] == e_ids, pad_start, 0), axis=-1) + rank

    dest_p = slot_of(idx_p, rank_p)
    dest_s = slot_of(idx_s, rank_s)
    n_blocks = -(-n_tot * TOP_K // MOE_BLOCK) + N_EXPERTS
    blk_row0 = jnp.arange(n_blocks, dtype=I32) * MOE_BLOCK
    blk_exp = jnp.minimum(jnp.sum((pad_end[None, :] <= blk_row0[:, None]).astype(I32), axis=1), N_EXPERTS - 1)
    n_used = (pad_end[-1:] // MOE_BLOCK).astype(I32)

    xb = _dispatch(pad_end.astype(I32), n_used, jnp.concatenate([dest_p, dest_s], axis=1), h2_p, h2_s,
                   n_blocks * MOE_BLOCK)
    has_rows = padded > 0
    later = (e_ids[None, :] > e_ids[:, None]) & has_rows[None, :]
    next_e = jnp.where(jnp.any(later, axis=1), jnp.min(jnp.where(later, e_ids[None, :], N_EXPERTS), axis=1), -1)
    half_e = (jnp.cumsum(has_rows.astype(I32)) - 1) % 2
    per_block = lambda tab: jnp.sum(jnp.where(blk_exp[:, None] == e_ids, tab, 0), axis=1).astype(I32)
    yb = _experts(blk_exp, n_used, per_block(next_e), per_block(half_e), xb, w_up[0], b_up[0], w_down[0], b_down[0])

    l2g, l2b = r1(ln2_g[0]), r1(ln2_b[0])
    y_p = _combine(dest_p, tw_p.T, x1_p, mod_p[5], l2g, l2b, yb, TOK_BLOCK)
    y_s = _combine(dest_s, tw_s.T, x1_s, mod_s[5], l2g, l2b, yb, TOK_BLOCK)

    smp = lambda a: a.reshape(1, NS, 1, HEADS, HEAD_W)
    return (y_p, y_s.reshape(NS, 1, D), kf_p[None], vf_p[None], s_p[None], smp(kf_s), smp(vf_s), s_s[None])
```

```python
import functools
import math

import numpy as np
import jax
import jax.numpy as jnp
from jax import lax
from jax.experimental import pallas as pl
from jax.experimental.pallas import tpu as pltpu

F32, BF16, I32, U32 = jnp.float32, jnp.bfloat16, jnp.int32, jnp.uint32

D_MODEL = 1024
HEADS = 4
HEAD_W = 128
DA_HEAD = 64
GROUP_W = HEADS * HEAD_W
HG_COLS = 4 * GROUP_W
IN_WIDTH = HG_COLS + 3 * GROUP_W
N_EXPERTS = 32
TOP_K = 4
D_FF = 1024
SWIGLU_ALPHA = 1.702
SWIGLU_LIMIT = 7.0
DEEPNORM_ALPHA = 2.0 ** 0.25
LN_EPS = 1e-5
RMS_EPS = 1e-6
LAM_INIT = 0.8 - 0.6 * math.exp(-0.3 * 0)
LOG2E = math.log2(math.e)
ALIBI_SLOPES = tuple(2.0 ** (-8.0 * (h + 1) / HEADS) for h in range(HEADS))
PAGE_SIZE = 128
EXP_CLAMP = 80.0
SKIP_LOG2 = 151.0

VMEM_LIMIT = 56 * 1024 * 1024
MOE_BLOCK = 256
HG_CHUNK = 256
ATTN_BLOCK = 1024
INPROJ_BLOCK = 512
ROW_BLOCK = 512
TOK_BLOCK = 128


def _cparams(sem):
    return pltpu.CompilerParams(dimension_semantics=sem, vmem_limit_bytes=VMEM_LIMIT)


def _sigmoid(x):
    return 1.0 / (1.0 + jnp.exp(-x))


def _layer_norm(x, g, b):
    mu = jnp.mean(x, -1, keepdims=True)
    xc = x - mu
    var = jnp.mean(xc * xc, -1, keepdims=True)
    return xc * lax.rsqrt(var + LN_EPS) * g + b


def _dot(a, b):
    return jnp.dot(a, b, preferred_element_type=F32)


def _dot_nt(a, b):
    return lax.dot_general(a, b, (((1,), (1,)), ((), ())), preferred_element_type=F32)


def _dot_tn(a, b):
    return lax.dot_general(a, b, (((0,), (0,)), ((), ())), preferred_element_type=F32)


ROW_SHAPE = (1, D_MODEL)


def _rows_copy(mat_ref, rows_hbm, row0, n, sem, to_hbm):
    rows = rows_hbm.at[pl.ds(row0, n), 0, :]
    return pltpu.make_async_copy(mat_ref, rows, sem) if to_hbm else pltpu.make_async_copy(rows, mat_ref, sem)


def _ada_kernel(c_ref, w_ref, b_ref, o_ref):
    c = c_ref[...]
    a = (c * _sigmoid(c)).astype(BF16)
    o_ref[...] = _dot(a, w_ref[...].astype(BF16)) + b_ref[...]


def _ada(c, w_ada, b_ada):
    rows, d = c.shape
    n = w_ada.shape[1]
    bn = 1536
    return pl.pallas_call(
        _ada_kernel,
        grid=(n // bn,),
        in_specs=[pl.BlockSpec((rows, d), lambda j: (0, 0)),
                  pl.BlockSpec((d, bn), lambda j: (0, j)),
                  pl.BlockSpec((1, bn), lambda j: (0, j))],
        out_specs=pl.BlockSpec((rows, bn), lambda j: (0, j)),
        out_shape=jax.ShapeDtypeStruct((rows, n), F32),
        compiler_params=_cparams(("arbitrary",)),
        name="ada",
    )(c, w_ada, b_ada.reshape(1, n))


def _inproj_common(x_ref, g_ref, b_ref, sc_ref, sh_ref, w_ref, zhg_ref, kf_ref, vf_ref):
    x0 = _layer_norm(x_ref[...], g_ref[...], b_ref[...])
    h = (x0 * (1.0 + sc_ref[...]) + sh_ref[...]).astype(BF16)
    zhg_ref[...] = _dot(h, w_ref[:, 0:HG_COLS])
    c0 = HG_COLS + GROUP_W
    k = _dot(h, w_ref[:, c0:c0 + GROUP_W])
    v = _dot(h, w_ref[:, c0 + GROUP_W:c0 + 2 * GROUP_W])
    for hd in range(HEADS):
        kf_ref[:, hd, :] = k[:, hd * HEAD_W:(hd + 1) * HEAD_W]
        vf_ref[:, hd, :] = v[:, hd * HEAD_W:(hd + 1) * HEAD_W]
    return h, k, v


def _inproj_prompt_kernel(x_ref, g_ref, b_ref, sc_ref, sh_ref, w_ref, wqt_ref,
                          zhg_ref, kf_ref, vf_ref, kb_ref, qt_ref, vt_ref):
    h, k, v = _inproj_common(x_ref, g_ref, b_ref, sc_ref, sh_ref, w_ref, zhg_ref, kf_ref, vf_ref)
    kb_ref[...] = k.astype(BF16)
    qt_ref[...] = (_dot_nt(wqt_ref[...], h) * (DA_HEAD ** -0.5 * LOG2E)).astype(BF16)
    vt_ref[...] = v.T.astype(BF16)


def _inproj_sample_kernel(x_ref, g_ref, b_ref, sc_ref, sh_ref, w_ref, zhg_ref, kf_ref, vf_ref, q_ref):
    h, _, _ = _inproj_common(x_ref, g_ref, b_ref, sc_ref, sh_ref, w_ref, zhg_ref, kf_ref, vf_ref)
    q_ref[...] = (_dot(h, w_ref[:, HG_COLS:HG_COLS + GROUP_W]) * (DA_HEAD ** -0.5)).astype(BF16)


def _mod_spec(mod, bm):
    if mod.shape[1] == 1:
        return pl.BlockSpec((None, 1, mod.shape[2]), lambda g, i: (g, 0, 0))
    return pl.BlockSpec((None, bm, mod.shape[2]), lambda g, i: (g, i, 0))


def _inproj(x, sc, sh, ln_g, ln_b, w_in_bf, bm, transposed=None):
    G, R, D = x.shape
    nb = R // bm
    row = lambda w: pl.BlockSpec((None, bm, w), lambda g, i: (g, i, 0))
    full = lambda a: pl.BlockSpec(a.shape, lambda g, i: (0,) * a.ndim)
    heads = pl.BlockSpec((None, bm, HEADS, HEAD_W), lambda g, i: (g, i, 0, 0))
    heads_shape = jax.ShapeDtypeStruct((G, R, HEADS, HEAD_W), F32)
    sds = lambda w, dt: jax.ShapeDtypeStruct((G, R, w), dt)
    args = [x, ln_g, ln_b, sc, sh, w_in_bf]
    in_specs = [row(D), full(ln_g), full(ln_b), _mod_spec(sc, bm), _mod_spec(sh, bm), full(w_in_bf)]
    out_specs = [row(HG_COLS), heads, heads, row(GROUP_W)]
    out_shape = [sds(HG_COLS, F32), heads_shape, heads_shape, sds(GROUP_W, BF16)]
    body = _inproj_sample_kernel
    if transposed is not None:
        wqt, tblk = transposed
        per = tblk // bm
        tr = pl.BlockSpec((None, None, GROUP_W, bm), lambda g, i: (g, i // per, 0, i % per))
        tr_shape = jax.ShapeDtypeStruct((G, R // tblk, GROUP_W, tblk), BF16)
        args += [wqt]
        in_specs += [full(wqt)]
        out_specs += [tr, tr]
        out_shape += [tr_shape, tr_shape]
        body = _inproj_prompt_kernel
    return pl.pallas_call(
        body,
        grid=(G, nb),
        in_specs=in_specs,
        out_specs=out_specs,
        out_shape=out_shape,
        compiler_params=_cparams(("arbitrary", "arbitrary")),
        name="inproj",
    )(*args)


def _hgrn_gates(zq, zf, lb):
    q = zq * _sigmoid(zq)
    f = lb + (1.0 - lb) * _sigmoid(zf)
    k = (1.0 - lb) * _sigmoid(-zf)
    return q, jnp.log(f), k


def _hgrn_kernel(z_ref, lb_ref, ng_ref, lvl_ref, tri_ref, o_ref, sfin_ref, st_ref, *, C):
    t = pl.program_id(1)

    @pl.when(t == 0)
    def _():
        st_ref[...] = jnp.zeros_like(st_ref)

    lvl = lvl_ref[...]
    tri = tri_ref[...]
    n_levels = int(math.log2(C)) - 3
    ng = ng_ref[...]
    for h in range(HEADS):
        cs = slice(h * HEAD_W, (h + 1) * HEAD_W)
        zq = z_ref[:, h * HEAD_W:(h + 1) * HEAD_W]
        zf = z_ref[:, GROUP_W + h * HEAD_W:GROUP_W + (h + 1) * HEAD_W]
        v = z_ref[:, 2 * GROUP_W + h * HEAD_W:2 * GROUP_W + (h + 1) * HEAD_W].astype(BF16)
        zg = z_ref[:, 3 * GROUP_W + h * HEAD_W:3 * GROUP_W + (h + 1) * HEAD_W]
        q, g, k = _hgrn_gates(zq, zf, lb_ref[:, cs])
        g1 = g.astype(BF16)
        r1 = g - g1.astype(F32)
        g2 = r1.astype(BF16)
        g3 = (r1 - g2.astype(F32)).astype(BF16)
        b = _dot(tri, g1) + _dot(tri, g2) + _dot(tri, g3)

        b8 = b.reshape(C // 8, 8, HEAD_W)
        bmid = jnp.broadcast_to(b8[:, 3:4, :], b8.shape).reshape(C, HEAD_W)
        e = jnp.clip(b - bmid, -EXP_CLAMP, EXP_CLAMP)
        a = jnp.where(lvl == 0, _dot_nt((q * jnp.exp(e)).astype(BF16), (k * jnp.exp(-e)).astype(BF16)), 0.0)
        for li in range(1, n_levels + 1):
            m = 4 << li
            bb = b.reshape(C // (2 * m), 2 * m, HEAD_W)
            d = b - jnp.broadcast_to(bb[:, m - 1:m, :], bb.shape).reshape(C, HEAD_W)
            qs = (q * jnp.exp(jnp.minimum(d, 0.0))).astype(BF16)
            ks = (k * jnp.exp(jnp.minimum(-d, 0.0))).astype(BF16)
            a = jnp.where(lvl == li, _dot_nt(qs, ks), a)

        st = st_ref[h]
        o = _dot(a.astype(BF16), v) + _dot_nt((q * jnp.exp(b)).astype(BF16), st.astype(BF16))
        b_last = b[C - 1:C, :]
        kd = (k * jnp.exp(b_last - b)).astype(BF16)
        st_ref[h] = st * jnp.exp(b_last) + _dot_tn(v, kd)

        ms = jnp.mean(o * o, -1, keepdims=True)
        o_ref[:, cs] = (o * lax.rsqrt(ms + RMS_EPS) * ng * (zg * _sigmoid(zg))).astype(BF16)

    @pl.when(t == pl.num_programs(1) - 1)
    def _():
        for h in range(HEADS):
            sfin_ref[h] = st_ref[h].T


def _hgrn_level_table(C):
    t = np.arange(C)[:, None]
    s = np.arange(C)[None, :]
    x = t ^ s
    lvl = np.zeros((C, C), np.int32)
    m = 8
    while m < C:
        lvl += (x >= m).astype(np.int32)
        m *= 2
    return np.where(s <= t, lvl, -1).astype(np.int32)


def _hgrn_prompt(zhg, lb, norm_g):
    B, T, _ = zhg.shape
    C = HG_CHUNK
    lvl = jnp.asarray(_hgrn_level_table(C))
    tri = jnp.asarray(np.tril(np.ones((C, C), np.float32)), BF16)
    full = lambda a: pl.BlockSpec(a.shape, lambda b, t: (0,) * a.ndim)
    return pl.pallas_call(
        functools.partial(_hgrn_kernel, C=C),
        grid=(B, T // C),
        in_specs=[pl.BlockSpec((None, C, HG_COLS), lambda b, t: (b, t, 0)), full(lb), full(norm_g), full(lvl), full(tri)],
        out_specs=[pl.BlockSpec((None, C, GROUP_W), lambda b, t: (b, t, 0)),
                   pl.BlockSpec((None, HEADS, HEAD_W, HEAD_W), lambda b, t: (b, 0, 0, 0))],
        out_shape=[jax.ShapeDtypeStruct((B, T, GROUP_W), BF16),
                   jax.ShapeDtypeStruct((B, HEADS, HEAD_W, HEAD_W), F32)],
        scratch_shapes=[pltpu.VMEM((HEADS, HEAD_W, HEAD_W), F32)],
        compiler_params=_cparams(("arbitrary", "arbitrary")),
        name="hgrn_prompt",
    )(zhg, lb, norm_g, lvl, tri)


def _hgrn_step_kernel(z_ref, s_ref, lb_ref, ng_ref, o_ref, so_ref, *, G):
    ng = ng_ref[...]
    for h in range(HEADS):
        cs = slice(h * HEAD_W, (h + 1) * HEAD_W)
        zq = z_ref[:, h * HEAD_W:(h + 1) * HEAD_W]
        zf = z_ref[:, GROUP_W + h * HEAD_W:GROUP_W + (h + 1) * HEAD_W]
        v = z_ref[:, 2 * GROUP_W + h * HEAD_W:2 * GROUP_W + (h + 1) * HEAD_W]
        zg = z_ref[:, 3 * GROUP_W + h * HEAD_W:3 * GROUP_W + (h + 1) * HEAD_W]
        lb = lb_ref[:, cs]
        q = zq * _sigmoid(zq)
        f = lb + (1.0 - lb) * _sigmoid(zf)
        k = (1.0 - lb) * _sigmoid(-zf)
        qT, fT, kT = q.T, f.T, k.T
        rows = []
        for j in range(G):
            s_new = fT[:, j:j + 1] * s_ref[j, h] + kT[:, j:j + 1] * v[j:j + 1, :]
            so_ref[j, h] = s_new
            rows.append(jnp.sum(s_new * qT[:, j:j + 1], axis=0, keepdims=True))
        o = jnp.concatenate(rows, axis=0)
        ms = jnp.mean(o * o, -1, keepdims=True)
        o_ref[:, cs] = (o * lax.rsqrt(ms + RMS_EPS) * ng * (zg * _sigmoid(zg))).astype(BF16)


def _hgrn_sample(zhg, state, lb, norm_g):
    N = zhg.shape[0]
    G = 8
    full = lambda a: pl.BlockSpec(a.shape, lambda i: (0,) * a.ndim)
    st_spec = pl.BlockSpec((G, HEADS, HEAD_W, HEAD_W), lambda i: (i, 0, 0, 0))
    return pl.pallas_call(
        functools.partial(_hgrn_step_kernel, G=G),
        grid=(N // G,),
        in_specs=[pl.BlockSpec((G, HG_COLS), lambda i: (i, 0)), st_spec, full(lb), full(norm_g)],
        out_specs=[pl.BlockSpec((G, GROUP_W), lambda i: (i, 0)), st_spec],
        out_shape=[jax.ShapeDtypeStruct((N, GROUP_W), BF16), jax.ShapeDtypeStruct(state.shape, F32)],
        compiler_params=_cparams(("arbitrary",)),
        name="hgrn_sample",
    )(zhg, state, lb, norm_g)


def _attn_kernel(qt_ref, k_ref, vt_ref, kpm_ref, sl_ref, feat_ref, lam_ref, gcol_ref, o_ref, m_ref, l_ref, acc_ref,
                 *, blk):
    qi = pl.program_id(2)
    row = lax.broadcasted_iota(I32, (HEAD_W, blk), 0)
    qt = qt_ref[...].astype(F32)
    a_hi = sl_ref[0:1, :]
    a_lo = sl_ref[1:2, :]
    slope2 = sl_ref[2:3, :]
    def slope_rows(r0):
        in_rows = (row >= r0) & (row < r0 + 4)
        return jnp.where(in_rows, jnp.where((row - r0) % 2 == 0, a_hi, a_lo), 0.0)

    q_aug = [jnp.where(row < DA_HEAD, qt, slope_rows(DA_HEAD)).astype(BF16),
             jnp.where(row >= DA_HEAD, qt, slope_rows(0)).astype(BF16)]
    keep = [feat_ref[0] > 0, feat_ref[1] > 0]
    feats = [feat_ref[2], feat_ref[3]]
    m_ref[...] = jnp.full_like(m_ref, -jnp.inf)
    l_ref[...] = jnp.zeros_like(l_ref)
    acc_ref[...] = jnp.zeros_like(acc_ref)

    def block(kj, masked):
        k_start = pl.multiple_of(kj * blk, blk)
        kb = k_ref[pl.ds(k_start, blk), :]
        vt = vt_ref[kj]
        off = slope2 * jnp.full((1, blk), k_start - qi * blk, I32).astype(F32)
        if masked:
            ok = lax.broadcasted_iota(I32, (blk, 1), 0) <= lax.broadcasted_iota(I32, (1, blk), 1)
        for mi in range(2):
            st = _dot(jnp.where(keep[mi], kb, feats[mi]), q_aug[mi])
            if masked:
                st = jnp.where(ok, st, -jnp.inf)
            m_prev = m_ref[mi]
            m_new = jnp.maximum(m_prev, jnp.max(st, axis=0, keepdims=True) + off)
            p = jnp.exp2(st - (m_new - off))
            alpha = jnp.exp2(m_prev - m_new)
            l_ref[mi] = alpha * l_ref[mi] + jnp.sum(p, axis=0, keepdims=True)
            acc_ref[mi] = alpha * acc_ref[mi] + _dot(vt, p.astype(BF16))
            m_ref[mi] = m_new

    block(qi, True)

    nblk = kpm_ref.shape[1]
    jlane = lax.broadcasted_iota(I32, (1, nblk), 1)
    far = slope2[:, 0:1] * ((jlane - qi) * blk + (blk - 1)).astype(F32)
    need = jlane < 0
    for mi in range(2):
        qm = qt[mi * DA_HEAD:(mi + 1) * DA_HEAD, :]
        qn = jnp.sqrt(jnp.max(jnp.sum(qm * qm, axis=0, keepdims=True), axis=1, keepdims=True))
        m_lo = jnp.min(m_ref[mi], axis=1, keepdims=True)
        need = need | (1.02 * qn * kpm_ref[mi:mi + 1, :] + 1.0 + far - m_lo >= -SKIP_LOG2)
    n_visit = jnp.sum((need & (jlane < qi)).astype(I32))

    def body(kj, c):
        block(kj, False)
        return c

    lax.fori_loop(qi - n_visit, qi, body, 0)

    ot = acc_ref[0] * (1.0 / l_ref[0]) - lam_ref[...] * (acc_ref[1] * (1.0 / l_ref[1]))
    ms = jnp.mean(ot * ot, axis=0, keepdims=True)
    o_ref[...] = (ot * lax.rsqrt(ms + RMS_EPS) * gcol_ref[...]).T.astype(BF16)


def _attn_prompt(qt, k, vt, lam, subln_g):
    B, nblk, _, blk = qt.shape
    T = k.shape[1]
    bf = lambda x: np.asarray(x, np.float32).astype(BF16).astype(np.float64)
    a = np.asarray(ALIBI_SLOPES, np.float64) * LOG2E
    a_hi = bf(a)
    a_lo = bf(a - a_hi)
    sl = np.zeros((HEADS, 8, blk), np.float32)
    sl[:, 0, :], sl[:, 1, :], sl[:, 2, :] = a_hi[:, None], a_lo[:, None], a.astype(np.float32)[:, None]
    lane = np.arange(HEAD_W)[None, :]
    r = np.arange(blk)[:, None]
    assert blk <= 256 * 256

    def pos_feat(l0):
        return (np.where((lane == l0) | (lane == l0 + 1), r // 256 * 256, 0)
                + np.where((lane == l0 + 2) | (lane == l0 + 3), r % 256, 0)).astype(np.float32)

    feat = np.stack([np.broadcast_to(lane < DA_HEAD, (blk, HEAD_W)).astype(np.float32),
                     np.broadcast_to(lane >= DA_HEAD, (blk, HEAD_W)).astype(np.float32),
                     pos_feat(DA_HEAD), pos_feat(0)])
    feat = jnp.asarray(feat, BF16)
    lam_row = jnp.full((1, blk), lam, F32)
    gcol = jnp.broadcast_to((subln_g.reshape(HEAD_W, 1) * (1.0 - LAM_INIT)), (HEAD_W, blk))
    kf = k.astype(F32).reshape(B, nblk, blk, HEADS, HEAD_W)
    kn = jnp.sqrt(jnp.max(jnp.sum(kf * kf, axis=-1), axis=2))
    upto = np.arange(nblk)[:, None] >= np.arange(nblk)[None, :]
    kpm = jnp.max(jnp.where(upto[None, :, :, None], kn[:, None, :, :], 0.0), axis=2)
    kpm = jnp.broadcast_to(kpm.transpose(0, 2, 1)[:, :, None, :], (B, HEADS, 2, nblk))
    full = lambda x: pl.BlockSpec(x.shape, lambda b, h, i: (0,) * x.ndim)
    return pl.pallas_call(
        functools.partial(_attn_kernel, blk=blk),
        grid=(B, HEADS, nblk),
        in_specs=[pl.BlockSpec((None, None, HEAD_W, blk), lambda b, h, i: (b, i, h, 0)),
                  pl.BlockSpec((None, T, HEAD_W), lambda b, h, i: (b, 0, h)),
                  pl.BlockSpec((None, nblk, HEAD_W, blk), lambda b, h, i: (b, 0, h, 0)),
                  pl.BlockSpec((None, None, 2, nblk), lambda b, h, i: (b, h, 0, 0)),
                  pl.BlockSpec((None, 8, blk), lambda b, h, i: (h, 0, 0)),
                  full(feat), full(lam_row), full(gcol)],
        out_specs=pl.BlockSpec((None, blk, HEAD_W), lambda b, h, i: (b, i, h)),
        out_shape=jax.ShapeDtypeStruct((B, T, GROUP_W), BF16),
        scratch_shapes=[pltpu.VMEM((2, 1, blk), F32), pltpu.VMEM((2, 1, blk), F32), pltpu.VMEM((2, HEAD_W, blk), F32)],
        compiler_params=_cparams(("arbitrary", "arbitrary", "arbitrary")),
        name="attn_prompt",
    )(qt, k, vt, kpm, jnp.asarray(sl), feat, lam_row, gcol)


def _attn_decode_kernel(pt_ref, q_ref, kn_ref, vn_ref, bias_ref, lam_ref, g_ref, ck_hbm, cv_hbm, o_ref,
                        kbuf, vbuf, sems, *, n_pages):
    i = pl.program_id(0)
    slot = i % 2

    def page_copies(seq, s):
        for pg in range(n_pages):
            page = pt_ref[seq, pg]
            for h in range(HEADS):
                dst = (s, h, pl.ds(pg * PAGE_SIZE, PAGE_SIZE))
                yield pltpu.make_async_copy(ck_hbm.at[0, page, :, h, :], kbuf.at[dst], sems.at[0, s])
                yield pltpu.make_async_copy(cv_hbm.at[0, page, :, h, :], vbuf.at[dst], sems.at[1, s])

    @pl.when(i == 0)
    def _():
        for cp in page_copies(0, 0):
            cp.start()

    @pl.when(i + 1 < pl.num_programs(0))
    def _():
        for cp in page_copies(i + 1, 1 - slot):
            cp.start()

    for cp in page_copies(i, slot):
        cp.wait()

    r8 = lax.broadcasted_iota(I32, (8, GROUP_W), 0)
    c8 = lax.broadcasted_iota(I32, (8, GROUP_W), 1)
    qmat = jnp.where(c8 // DA_HEAD == r8, jnp.broadcast_to(q_ref[...].astype(F32), (8, GROUP_W)), 0.0)
    qmat_bf = qmat.astype(BF16)
    s = bias_ref[...]
    for h in range(HEADS):
        s = s + _dot_nt(qmat_bf[:, h * HEAD_W:(h + 1) * HEAD_W], kbuf[slot, h].astype(BF16))
    kn = kn_ref[...].astype(BF16).astype(F32)
    s_self = jnp.sum(qmat_bf.astype(F32) * kn, axis=-1, keepdims=True)
    m = jnp.maximum(jnp.max(s, -1, keepdims=True), s_self)
    p = jnp.exp(s - m)
    p_self = jnp.exp(s_self - m)
    inv_l = 1.0 / (jnp.sum(p, -1, keepdims=True) + p_self)
    coef = jnp.where(lax.broadcasted_iota(I32, (8, 1), 0) % 2 == 0, 1.0, -lam_ref[:, 0:1]) * inv_l
    w = (p * coef).astype(BF16)
    accs = [_dot(w, vbuf[slot, h].astype(BF16)) for h in range(HEADS)]
    acc = jnp.concatenate(accs, axis=1) + (p_self * coef) * vn_ref[...]
    o = jnp.sum(jnp.where(c8 // HEAD_W == r8 // 2, acc, 0.0), axis=0, keepdims=True)
    outs = []
    for h in range(HEADS):
        oh = o[:, h * HEAD_W:(h + 1) * HEAD_W]
        ms = jnp.mean(oh * oh, -1, keepdims=True)
        outs.append(oh * lax.rsqrt(ms + RMS_EPS) * g_ref[...] * (1.0 - LAM_INIT))
    o_ref[...] = jnp.concatenate(outs, axis=1).astype(BF16)


def _attn_sample(q, k_new, v_new, cache_k, cache_v, page_table, lam_row, subln_g):
    N, n_pages = page_table.shape
    past = n_pages * PAGE_SIZE
    kpos = np.arange(past, dtype=np.float32)[None, :]
    slope_rows = np.repeat(np.asarray(ALIBI_SLOPES, np.float32), 2)[:, None]
    bias = jnp.asarray(-slope_rows * (past - kpos))
    row = pl.BlockSpec((None, 1, GROUP_W), lambda i, pt: (i, 0, 0))
    full = lambda a: pl.BlockSpec(a.shape, lambda i, pt: (0,) * a.ndim)
    anyspec = pl.BlockSpec(memory_space=pl.ANY)
    grid_spec = pltpu.PrefetchScalarGridSpec(
        num_scalar_prefetch=1,
        grid=(N,),
        in_specs=[row, row, row, full(bias), full(lam_row), full(subln_g), anyspec, anyspec],
        out_specs=row,
        scratch_shapes=[pltpu.VMEM((2, HEADS, past, HEAD_W), F32), pltpu.VMEM((2, HEADS, past, HEAD_W), F32),
                        pltpu.SemaphoreType.DMA((2, 2))],
    )
    r3 = lambda a: a.reshape(N, 1, GROUP_W)
    out = pl.pallas_call(
        functools.partial(_attn_decode_kernel, n_pages=n_pages),
        grid_spec=grid_spec,
        out_shape=jax.ShapeDtypeStruct((N, 1, GROUP_W), BF16),
        compiler_params=_cparams(("arbitrary",)),
        name="attn_sample",
    )(page_table, r3(q), r3(k_new), r3(v_new), bias, lam_row, subln_g, cache_k, cache_v)
    return out.reshape(N, GROUP_W)


def _outproj_kernel(hg_ref, da_ref, x_ref, lg_ref, lb_ref, g1_ref, sc2_ref, sh2_ref, w_ref, l1g_ref, l1b_ref,
                    wrh_ref, wrl_ref, br_ref, u_ref, cin_ref,
                    x1_ref, h2_hbm, idx_ref, tw_ref, rank_ref, cout_ref, run_ref, hbuf, hsem, *, n_steps):
    step = pl.program_id(0) * pl.num_programs(1) + pl.program_id(1)
    slot = step % 2
    bm = x_ref.shape[0]
    h2_copies = lambda st, s: [_rows_copy(hbuf.at[s], h2_hbm, st * bm, bm, hsem.at[s], True)]

    @pl.when(step == 0)
    def _():
        run_ref[...] = cin_ref[...]

    x0 = _layer_norm(x_ref[...], lg_ref[...], lb_ref[...])
    mix = _dot(hg_ref[...], w_ref[0:GROUP_W, :]) + _dot(da_ref[...], w_ref[GROUP_W:2 * GROUP_W, :])
    x1 = _layer_norm(DEEPNORM_ALPHA * x0 + (1.0 + g1_ref[...]) * mix, l1g_ref[...], l1b_ref[...])
    x1_ref[...] = x1
    h2 = x1 * (1.0 + sc2_ref[...]) + sh2_ref[...]

    @pl.when(step >= 2)
    def _():
        for cp in h2_copies(step - 2, slot):
            cp.wait()

    hbuf[slot] = h2
    for cp in h2_copies(step, slot):
        cp.start()

    @pl.when(step == n_steps - 1)
    def _():
        tail = h2_copies(step, slot)
        if n_steps > 1:
            tail = h2_copies(step - 1, 1 - slot) + tail
        for cp in tail:
            cp.wait()

    hi = h2.astype(BF16)
    lo = (h2 - hi.astype(F32)).astype(BF16)
    wrh = wrh_ref[...]
    logits = _dot_nt(wrh, hi) + _dot_nt(wrh, lo) + _dot_nt(wrl_ref[...], hi) + br_ref[...]

    n_e, bm = logits.shape
    rows = lax.broadcasted_iota(I32, (n_e, bm), 0).astype(F32)
    vals, sels = [], []
    work = logits
    for kk in range(TOP_K):
        mx = jnp.max(work, axis=0, keepdims=True)
        ix = jnp.min(jnp.where(work == mx, rows, float(n_e)), axis=0, keepdims=True)
        sel = rows == ix
        idx_ref[kk:kk + 1, :] = ix.astype(I32)
        vals.append(mx)
        sels.append(sel)
        work = jnp.where(sel, -jnp.inf, work)
    es = [jnp.exp(vv - vals[0]) for vv in vals]
    inv = 1.0 / (es[0] + es[1] + es[2] + es[3])
    for kk in range(TOP_K):
        tw_ref[kk:kk + 1, :] = es[kk] * inv

    base = run_ref[...]
    for kk in range(TOP_K):
        oh = jnp.where(sels[kk], 1.0, 0.0)
        before = base + _dot(oh.astype(BF16), u_ref[...])
        rank_ref[kk:kk + 1, :] = jnp.sum(jnp.where(sels[kk], before, 0.0), axis=0, keepdims=True).astype(I32)
        base = base + jnp.sum(oh, axis=1, keepdims=True)
    run_ref[...] = base
    cout_ref[...] = base


def _outproj(hg, da, x, mods, consts, counts_in, bm):
    G, R, D = x.shape
    g1, sc2, sh2 = mods
    ln_g, ln_b, w_out_bf, l1g, l1b, wrh, wrl, br = consts
    nb = R // bm
    n_tok = G * R
    u = jnp.asarray(np.triu(np.ones((bm, bm), np.float32), 1), BF16)
    row = lambda w: pl.BlockSpec((None, bm, w), lambda g, i: (g, i, 0))
    full = lambda a: pl.BlockSpec(a.shape, lambda g, i: (0,) * a.ndim)
    tok_lanes = pl.BlockSpec((TOP_K, bm), lambda g, i: (0, g * nb + i))
    return pl.pallas_call(
        functools.partial(_outproj_kernel, n_steps=G * nb),
        grid=(G, nb),
        in_specs=[row(GROUP_W), row(GROUP_W), row(D), full(ln_g), full(ln_b),
                  _mod_spec(g1, bm), _mod_spec(sc2, bm), _mod_spec(sh2, bm),
                  full(w_out_bf), full(l1g), full(l1b), full(wrh), full(wrl), full(br), full(u), full(counts_in)],
        out_specs=[row(D), pl.BlockSpec(memory_space=pl.ANY), tok_lanes, tok_lanes, tok_lanes, full(counts_in)],
        out_shape=[jax.ShapeDtypeStruct((G, R, D), F32),
                   jax.ShapeDtypeStruct((n_tok,) + ROW_SHAPE, F32),
                   jax.ShapeDtypeStruct((TOP_K, n_tok), I32),
                   jax.ShapeDtypeStruct((TOP_K, n_tok), F32),
                   jax.ShapeDtypeStruct((TOP_K, n_tok), I32),
                   jax.ShapeDtypeStruct(counts_in.shape, F32)],
        scratch_shapes=[pltpu.VMEM(counts_in.shape, F32), pltpu.VMEM((2, bm, D), F32), pltpu.SemaphoreType.DMA((2,))],
        compiler_params=_cparams(("arbitrary", "arbitrary")),
        name="outproj",
    )(hg, da, x, ln_g, ln_b, g1, sc2, sh2, w_out_bf, l1g, l1b, wrh, wrl, br, u, counts_in)


def _dispatch_kernel(pe_ref, nu_ref, dest_ref, h2p_ref, h2s_ref, xb_hbm, zbuf, zsem, sem, *, bt, n_prompt_steps, nb):
    i = pl.program_id(0)
    bm = MOE_BLOCK

    @pl.when(i == 0)
    def _():
        zbuf[...] = jnp.zeros_like(zbuf)
        zero_block = lambda row0: pltpu.make_async_copy(zbuf, xb_hbm.at[pl.ds(row0, bm)], zsem)
        has_rows = [pe_ref[e] > (pe_ref[e - 1] if e else 0) for e in range(N_EXPERTS)]
        for e in range(N_EXPERTS):
            @pl.when(has_rows[e])
            def _():
                zero_block(pe_ref[e] - bm).start()

        def tail_start(j, c):
            zero_block(j * bm).start()
            return c

        def tail_wait(j, c):
            zero_block(j * bm).wait()
            return c

        lax.fori_loop(nu_ref[0], nb, tail_start, 0)
        for e in range(N_EXPERTS):
            @pl.when(has_rows[e])
            def _():
                zero_block(pe_ref[e] - bm).wait()
        lax.fori_loop(nu_ref[0], nb, tail_wait, 0)

    def scatter(h2_ref):
        def body(t, c):
            for kk in range(TOP_K):
                pltpu.make_async_copy(h2_ref.at[t], xb_hbm.at[dest_ref[kk, t]], sem).start(priority=kk % 2)
            return c

        lax.fori_loop(0, bt, body, 0, unroll=4)
        for kk in range(TOP_K):
            pltpu.make_async_copy(h2_ref, xb_hbm.at[pl.ds(0, bt)], sem).wait()

    @pl.when(i < n_prompt_steps)
    def _():
        scatter(h2p_ref)

    @pl.when(i >= n_prompt_steps)
    def _():
        scatter(h2s_ref)


def _dispatch(pad_end, n_used, dest, h2_p, h2_s, n_rows):
    bt = TOK_BLOCK
    n_p, n_s = h2_p.shape[0] // bt, h2_s.shape[0] // bt
    nb = n_rows // MOE_BLOCK
    grid_spec = pltpu.PrefetchScalarGridSpec(
        num_scalar_prefetch=2,
        grid=(n_p + n_s,),
        in_specs=[pl.BlockSpec((TOP_K, bt), lambda i, pe, nu: (0, i), memory_space=pltpu.SMEM),
                  pl.BlockSpec((bt,) + ROW_SHAPE, lambda i, pe, nu: (jnp.minimum(i, n_p - 1), 0, 0)),
                  pl.BlockSpec((bt,) + ROW_SHAPE, lambda i, pe, nu: (jnp.maximum(i - n_p, 0), 0, 0))],
        out_specs=pl.BlockSpec(memory_space=pl.ANY),
        scratch_shapes=[pltpu.VMEM((MOE_BLOCK,) + ROW_SHAPE, F32), pltpu.SemaphoreType.DMA(()),
                        pltpu.SemaphoreType.DMA(())],
    )
    return pl.pallas_call(
        functools.partial(_dispatch_kernel, bt=bt, n_prompt_steps=n_p, nb=nb),
        grid_spec=grid_spec,
        out_shape=jax.ShapeDtypeStruct((n_rows,) + ROW_SHAPE, F32),
        compiler_params=_cparams(("arbitrary",)),
        name="dispatch",
    )(pad_end, n_used, dest, h2_p, h2_s)


def _experts_kernel(be_ref, nu_ref, nxt_ref, par_ref, xb_hbm, wu_hbm, bu_ref, wd_hbm, bd_ref, yb_hbm,
                    wu_bf, wd_bf, wu_f32, wd_f32, xbuf, ybuf, sems, wsems, *, bm):
    j = pl.program_id(0)
    nb = pl.num_programs(0)
    slot = j % 2
    nu = nu_ref[0]
    active = j < nu
    x_copies = lambda blk, s: [_rows_copy(xbuf.at[s], xb_hbm, blk * bm, bm, sems.at[0, s], False)]
    y_copies = lambda blk, s: [_rows_copy(ybuf.at[s], yb_hbm, blk * bm, bm, sems.at[1, s], True)]
    w_copies = lambda e, s: [pltpu.make_async_copy(wu_hbm.at[e], wu_f32.at[s], wsems.at[0, s]),
                             pltpu.make_async_copy(wd_hbm.at[e], wd_f32.at[s], wsems.at[1, s])]

    @pl.when(j == 0)
    def _():
        for cp in x_copies(0, 0) + w_copies(be_ref[0], par_ref[0]):
            cp.start()

    @pl.when(j + 1 < nu)
    def _():
        for cp in x_copies(j + 1, 1 - slot):
            cp.start()

    @pl.when(j >= 2)
    def _():
        for cp in y_copies(j - 2, slot):
            cp.wait()

    @pl.when(active & ((j == 0) | (be_ref[j] != be_ref[jnp.maximum(j - 1, 0)])))
    def _():
        half = par_ref[j]
        for cp in w_copies(be_ref[j], half):
            cp.wait()
        wu_bf[...] = wu_f32[half].astype(BF16)
        wd_bf[...] = wd_f32[half].astype(BF16)

        @pl.when(nxt_ref[j] >= 0)
        def _():
            for cp in w_copies(nxt_ref[j], 1 - half):
                cp.start()

    @pl.when(active)
    def _():
        for cp in x_copies(j, slot):
            cp.wait()
        u = _dot(xbuf[slot].astype(BF16), wu_bf[...]) + bu_ref[...]
        glu = jnp.minimum(u[:, :D_FF], SWIGLU_LIMIT)
        lin = jnp.clip(u[:, D_FF:], -SWIGLU_LIMIT, SWIGLU_LIMIT)
        act = glu * _sigmoid(SWIGLU_ALPHA * glu) * (lin + 1.0)
        ybuf[slot] = _dot(act.astype(BF16), wd_bf[...]) + bd_ref[...]

    @pl.when(jnp.logical_not(active))
    def _():
        ybuf[slot] = jnp.zeros((bm, D_MODEL), F32)

    for cp in y_copies(j, slot):
        cp.start()

    @pl.when(j == nb - 1)
    def _():
        for cp in y_copies(j - 1, 1 - slot) + y_copies(j, slot):
            cp.wait()


def _experts(blk_exp, n_used, blk_next, blk_half, xb, w_up, b_up, w_down, b_down):
    n_rows = xb.shape[0]
    bm = MOE_BLOCK
    nb = n_rows // bm
    anyspec = pl.BlockSpec(memory_space=pl.ANY)
    grid_spec = pltpu.PrefetchScalarGridSpec(
        num_scalar_prefetch=4,
        grid=(nb,),
        in_specs=[anyspec, anyspec,
                  pl.BlockSpec((None, 1, 2 * D_FF), lambda j, be, nu, nx, hf: (be[j], 0, 0)),
                  anyspec,
                  pl.BlockSpec((None, 1, D_MODEL), lambda j, be, nu, nx, hf: (be[j], 0, 0))],
        out_specs=anyspec,
        scratch_shapes=[pltpu.VMEM((D_MODEL, 2 * D_FF), BF16), pltpu.VMEM((D_FF, D_MODEL), BF16),
                        pltpu.VMEM((2, D_MODEL, 2 * D_FF), F32), pltpu.VMEM((2, D_FF, D_MODEL), F32),
                        pltpu.VMEM((2, bm, D_MODEL), F32), pltpu.VMEM((2, bm, D_MODEL), F32),
                        pltpu.SemaphoreType.DMA((2, 2)), pltpu.SemaphoreType.DMA((2, 2))],
    )
    return pl.pallas_call(
        functools.partial(_experts_kernel, bm=bm),
        grid_spec=grid_spec,
        out_shape=jax.ShapeDtypeStruct((n_rows,) + ROW_SHAPE, F32),
        compiler_params=_cparams(("arbitrary",)),
        name="experts",
    )(blk_exp, n_used, blk_next, blk_half, xb, w_up, b_up.reshape(N_EXPERTS, 1, 2 * D_FF), w_down,
      b_down.reshape(N_EXPERTS, 1, D_MODEL))


def _combine_kernel(dcur_ref, dnext_ref, x1_ref, tw_ref, g2_ref, lg_ref, lb_ref, yb_hbm, o_ref, buf, sems, *, bm):
    i = pl.program_id(0) * pl.num_programs(1) + pl.program_id(1)
    n = pl.num_programs(0) * pl.num_programs(1)
    slot = i % 2

    def issue(dref, s):
        def body(t, c):
            for kk in range(TOP_K):
                pltpu.make_async_copy(yb_hbm.at[dref[kk, t]], buf.at[s, kk, pl.ds(t, 1), :],
                                      sems.at[s]).start(priority=kk % 2)
            return c
        lax.fori_loop(0, bm, body, 0, unroll=4)

    @pl.when(i == 0)
    def _():
        issue(dcur_ref, 0)

    @pl.when(i + 1 < n)
    def _():
        issue(dnext_ref, 1 - slot)

    for kk in range(TOP_K):
        pltpu.make_async_copy(yb_hbm.at[pl.ds(0, bm), 0, :], buf.at[slot, kk], sems.at[slot]).wait()

    tw = tw_ref[...]
    ff = tw[:, 0:1] * buf[slot, 0]
    for kk in range(1, TOP_K):
        ff = ff + tw[:, kk:kk + 1] * buf[slot, kk]
    o_ref[...] = _layer_norm(DEEPNORM_ALPHA * x1_ref[...] + (1.0 + g2_ref[...]) * ff, lg_ref[...], lb_ref[...])


def _combine(dest, tw_rows, x1, g2, ln_g, ln_b, yb, bm):
    G, R, D = x1.shape
    nb = R // bm
    n_blk = G * nb
    row = lambda w: pl.BlockSpec((None, bm, w), lambda g, i: (g, i, 0))
    full = lambda a: pl.BlockSpec(a.shape, lambda g, i: (0,) * a.ndim)
    cur = pl.BlockSpec((TOP_K, bm), lambda g, i: (0, g * nb + i), memory_space=pltpu.SMEM)
    nxt = pl.BlockSpec((TOP_K, bm), lambda g, i: (0, jnp.minimum(g * nb + i + 1, n_blk - 1)),
                       memory_space=pltpu.SMEM)
    return pl.pallas_call(
        functools.partial(_combine_kernel, bm=bm),
        grid=(G, nb),
        in_specs=[cur, nxt, row(D), pl.BlockSpec((bm, TOP_K), lambda g, i: (g * nb + i, 0)),
                  _mod_spec(g2, bm), full(ln_g), full(ln_b), pl.BlockSpec(memory_space=pl.ANY)],
        out_specs=row(D),
        out_shape=jax.ShapeDtypeStruct((G, R, D), F32),
        scratch_shapes=[pltpu.VMEM((2, TOP_K, bm, D), F32), pltpu.SemaphoreType.DMA((2,))],
        compiler_params=_cparams(("arbitrary", "arbitrary")),
        name="combine",
    )(dest, dest, x1, tw_rows, g2, ln_g, ln_b, yb)


def kernel(x_prompt, x_sample, c_prompt, c_sample, cache_k, cache_v, state_hgrn, page_table, ln_in_g, ln_in_b, w_ada, b_ada, w_in, hg_lb, hg_norm_g, da_lq1, da_lk1, da_lq2, da_lk2, da_subln_g, w_out, ln1_g, ln1_b, w_router, b_router, w_up, b_up, w_down, b_down, ln2_g, ln2_b):
    assert w_in.shape[0] == 1, "single-layer trunk"
    B, T, D = x_prompt.shape
    NS = x_sample.shape[0]
    n_prompt = B * T
    n_tot = n_prompt + NS
    r1 = lambda a: a.reshape(1, -1)

    lb = r1(jax.nn.softmax(hg_lb.astype(F32), axis=0)[0])
    lam = (jnp.exp(jnp.sum(da_lq1[0].astype(F32) * da_lk1[0].astype(F32)))
           - jnp.exp(jnp.sum(da_lq2[0].astype(F32) * da_lk2[0].astype(F32))) + LAM_INIT)
    lam_row = jnp.full((1, HEAD_W), lam, F32)
    ln_g, ln_b = r1(ln_in_g), r1(ln_in_b)
    w_in_bf = w_in[0].astype(BF16)
    wqt = w_in_bf[:, HG_COLS:HG_COLS + GROUP_W].T
    w_out_bf = w_out[0].astype(BF16)
    wr_t = w_router[0].T
    wrh = wr_t.astype(BF16)
    wrl = (wr_t - wrh.astype(F32)).astype(BF16)
    br = b_router[0].reshape(N_EXPERTS, 1)
    norm_g, subln_g = r1(hg_norm_g[0]), r1(da_subln_g[0])

    n_c = B + NS
    c_all = jnp.concatenate([c_prompt, c_sample, jnp.zeros((-n_c % 8, D), F32)], axis=0)
    mod = _ada(c_all, w_ada[0], b_ada[0])
    mod_p = [mod[:B, j * D:(j + 1) * D].reshape(B, 1, D) for j in range(6)]
    mod_s = [mod[B:n_c, j * D:(j + 1) * D].reshape(1, NS, D) for j in range(6)]

    zhg_p, kf_p, vf_p, kb_p, qt_p, vt_p = _inproj(x_prompt, mod_p[1], mod_p[0], ln_g, ln_b, w_in_bf, INPROJ_BLOCK,
                                                  (wqt, ATTN_BLOCK))
    hg_p, s_p = _hgrn_prompt(zhg_p, lb, norm_g)
    da_p = _attn_prompt(qt_p, kb_p, vt_p, lam, subln_g)

    xs = x_sample.reshape(1, NS, D)
    zhg_s, kf_s, vf_s, q_s = _inproj(xs, mod_s[1], mod_s[0], ln_g, ln_b, w_in_bf, NS)
    hg_s, s_s = _hgrn_sample(zhg_s[0], state_hgrn[0], lb, norm_g)
    da_s = _attn_sample(q_s[0], kf_s.reshape(NS, GROUP_W), vf_s.reshape(NS, GROUP_W), cache_k, cache_v, page_table,
                        lam_row, subln_g)

    consts = (ln_g, ln_b, w_out_bf, r1(ln1_g[0]), r1(ln1_b[0]), wrh, wrl, br)
    counts0 = jnp.zeros((N_EXPERTS, 1), F32)
    x1_p, h2_p, idx_p, tw_p, rank_p, counts1 = _outproj(
        hg_p, da_p, x_prompt, (mod_p[2], mod_p[4], mod_p[3]), consts, counts0, ROW_BLOCK)
    x1_s, h2_s, idx_s, tw_s, rank_s, counts = _outproj(
        hg_s.reshape(1, NS, GROUP_W), da_s.reshape(1, NS, GROUP_W), xs, (mod_s[2], mod_s[4], mod_s[3]), consts,
        counts1, NS)

    cnt = counts[:, 0].astype(I32)
    padded = (cnt + MOE_BLOCK - 1) // MOE_BLOCK * MOE_BLOCK
    pad_end = jnp.cumsum(padded)
    pad_start = pad_end - padded
    e_ids = jnp.arange(N_EXPERTS, dtype=I32)

    def slot_of(idx, rank):
        return jnp.sum(jnp.where(idx[..., None] == e_ids, pad_start, 0), axis=-1) + rank

    dest_p = slot_of(idx_p, rank_p)
    dest_s = slot_of(idx_s, rank_s)
    n_blocks = -(-n_tot * TOP_K // MOE_BLOCK) + N_EXPERTS
    blk_row0 = jnp.arange(n_blocks, dtype=I32) * MOE_BLOCK
    blk_exp = jnp.minimum(jnp.sum((pad_end[None, :] <= blk_row0[:, None]).astype(I32), axis=1), N_EXPERTS - 1)
    n_used = (pad_end[-1:] // MOE_BLOCK).astype(I32)

    xb = _dispatch(pad_end.astype(I32), n_used, jnp.concatenate([dest_p, dest_s], axis=1), h2_p, h2_s,
                   n_blocks * MOE_BLOCK)
    has_rows = padded > 0
    later = (e_ids[None, :] > e_ids[:, None]) & has_rows[None, :]
    next_e = jnp.where(jnp.any(later, axis=1), jnp.min(jnp.where(later, e_ids[None, :], N_EXPERTS), axis=1), -1)
    half_e = (jnp.cumsum(has_rows.astype(I32)) - 1) % 2
    per_block = lambda tab: jnp.sum(jnp.where(blk_exp[:, None] == e_ids, tab, 0), axis=1).astype(I32)
    yb = _experts(blk_exp, n_used, per_block(next_e), per_block(half_e), xb, w_up[0], b_up[0], w_down[0], b_down[0])

    l2g, l2b = r1(ln2_g[0]), r1(ln2_b[0])
    y_p = _combine(dest_p, tw_p.T, x1_p, mod_p[5], l2g, l2b, yb, TOK_BLOCK)
    y_s = _combine(dest_s, tw_s.T, x1_s, mod_s[5], l2g, l2b, yb, TOK_BLOCK)

    smp = lambda a: a.reshape(1, NS, 1, HEADS, HEAD_W)
    return (y_p, y_s.reshape(NS, 1, D), kf_p[None], vf_p[None], s_p[None], smp(kf_s), smp(vf_s), s_s[None])
```

```python
import functools
import math

import numpy as np
import jax
import jax.numpy as jnp
from jax import lax
from jax.experimental import pallas as pl
from jax.experimental.pallas import tpu as pltpu

F32, BF16, I32, U32 = jnp.float32, jnp.bfloat16, jnp.int32, jnp.uint32

D_MODEL = 1024
HEADS = 4
HEAD_W = 128
DA_HEAD = 64
GROUP_W = HEADS * HEAD_W
HG_COLS = 4 * GROUP_W
IN_WIDTH = HG_COLS + 3 * GROUP_W
N_EXPERTS = 32
TOP_K = 4
D_FF = 1024
SWIGLU_ALPHA = 1.702
SWIGLU_LIMIT = 7.0
DEEPNORM_ALPHA = 2.0 ** 0.25
LN_EPS = 1e-5
RMS_EPS = 1e-6
LAM_INIT = 0.8 - 0.6 * math.exp(-0.3 * 0)
LOG2E = math.log2(math.e)
ALIBI_SLOPES = tuple(2.0 ** (-8.0 * (h + 1) / HEADS) for h in range(HEADS))
PAGE_SIZE = 128
EXP_CLAMP = 80.0
SKIP_LOG2 = 151.0

VMEM_LIMIT = 56 * 1024 * 1024
MOE_BLOCK = 256
HG_CHUNK = 256
ATTN_BLOCK = 1024
INPROJ_BLOCK = 512
ROW_BLOCK = 512
TOK_BLOCK = 128


def _cparams(sem):
    return pltpu.CompilerParams(dimension_semantics=sem, vmem_limit_bytes=VMEM_LIMIT)


def _sigmoid(x):
    return 1.0 / (1.0 + jnp.exp(-x))


def _layer_norm(x, g, b):
    mu = jnp.mean(x, -1, keepdims=True)
    xc = x - mu
    var = jnp.mean(xc * xc, -1, keepdims=True)
    return xc * lax.rsqrt(var + LN_EPS) * g + b


def _dot(a, b):
    return jnp.dot(a, b, preferred_element_type=F32)


def _dot_nt(a, b):
    return lax.dot_general(a, b, (((1,), (1,)), ((), ())), preferred_element_type=F32)


def _dot_tn(a, b):
    return lax.dot_general(a, b, (((0,), (0,)), ((), ())), preferred_element_type=F32)


ROW_SHAPE = (1, D_MODEL)


def _rows_copy(mat_ref, rows_hbm, row0, n, sem, to_hbm):
    rows = rows_hbm.at[pl.ds(row0, n), 0, :]
    return pltpu.make_async_copy(mat_ref, rows, sem) if to_hbm else pltpu.make_async_copy(rows, mat_ref, sem)


def _ada_kernel(c_ref, w_ref, b_ref, o_ref):
    c = c_ref[...]
    a = (c * _sigmoid(c)).astype(BF16)
    o_ref[...] = _dot(a, w_ref[...].astype(BF16)) + b_ref[...]


def _ada(c, w_ada, b_ada):
    rows, d = c.shape
    n = w_ada.shape[1]
    bn = 1536
    return pl.pallas_call(
        _ada_kernel,
        grid=(n // bn,),
        in_specs=[pl.BlockSpec((rows, d), lambda j: (0, 0)),
                  pl.BlockSpec((d, bn), lambda j: (0, j)),
                  pl.BlockSpec((1, bn), lambda j: (0, j))],
        out_specs=pl.BlockSpec((rows, bn), lambda j: (0, j)),
        out_shape=jax.ShapeDtypeStruct((rows, n), F32),
        compiler_params=_cparams(("arbitrary",)),
        name="ada",
    )(c, w_ada, b_ada.reshape(1, n))


def _inproj_common(x_ref, g_ref, b_ref, sc_ref, sh_ref, w_ref, zhg_ref, kf_ref, vf_ref):
    x0 = _layer_norm(x_ref[...], g_ref[...], b_ref[...])
    h = (x0 * (1.0 + sc_ref[...]) + sh_ref[...]).astype(BF16)
    zhg_ref[...] = _dot(h, w_ref[:, 0:HG_COLS])
    c0 = HG_COLS + GROUP_W
    k = _dot(h, w_ref[:, c0:c0 + GROUP_W])
    v = _dot(h, w_ref[:, c0 + GROUP_W:c0 + 2 * GROUP_W])
    for hd in range(HEADS):
        kf_ref[:, hd, :] = k[:, hd * HEAD_W:(hd + 1) * HEAD_W]
        vf_ref[:, hd, :] = v[:, hd * HEAD_W:(hd + 1) * HEAD_W]
    return h, k, v


def _inproj_prompt_kernel(x_ref, g_ref, b_ref, sc_ref, sh_ref, w_ref, wqt_ref, seg_ref,
                          zhg_ref, kf_ref, vf_ref, kb_ref, qt_ref, vt_ref, kn_ref):
    h, k, v = _inproj_common(x_ref, g_ref, b_ref, sc_ref, sh_ref, w_ref, zhg_ref, kf_ref, vf_ref)
    kb_ref[...] = k.astype(BF16)
    kn_ref[...] = jnp.max(_dot((k * k).astype(BF16), seg_ref[...]), axis=0, keepdims=True)
    qt_ref[...] = (_dot_nt(wqt_ref[...], h) * (DA_HEAD ** -0.5 * LOG2E)).astype(BF16)
    vt_ref[...] = v.T.astype(BF16)


def _inproj_sample_kernel(x_ref, g_ref, b_ref, sc_ref, sh_ref, w_ref, zhg_ref, kf_ref, vf_ref, q_ref):
    h, _, _ = _inproj_common(x_ref, g_ref, b_ref, sc_ref, sh_ref, w_ref, zhg_ref, kf_ref, vf_ref)
    q_ref[...] = (_dot(h, w_ref[:, HG_COLS:HG_COLS + GROUP_W]) * (DA_HEAD ** -0.5)).astype(BF16)


def _mod_spec(mod, bm):
    if mod.shape[1] == 1:
        return pl.BlockSpec((None, 1, mod.shape[2]), lambda g, i: (g, 0, 0))
    return pl.BlockSpec((None, bm, mod.shape[2]), lambda g, i: (g, i, 0))


def _inproj(x, sc, sh, ln_g, ln_b, w_in_bf, bm, transposed=None):
    G, R, D = x.shape
    nb = R // bm
    row = lambda w: pl.BlockSpec((None, bm, w), lambda g, i: (g, i, 0))
    full = lambda a: pl.BlockSpec(a.shape, lambda g, i: (0,) * a.ndim)
    heads = pl.BlockSpec((None, bm, HEADS, HEAD_W), lambda g, i: (g, i, 0, 0))
    heads_shape = jax.ShapeDtypeStruct((G, R, HEADS, HEAD_W), F32)
    sds = lambda w, dt: jax.ShapeDtypeStruct((G, R, w), dt)
    args = [x, ln_g, ln_b, sc, sh, w_in_bf]
    in_specs = [row(D), full(ln_g), full(ln_b), _mod_spec(sc, bm), _mod_spec(sh, bm), full(w_in_bf)]
    out_specs = [row(HG_COLS), heads, heads, row(GROUP_W)]
    out_shape = [sds(HG_COLS, F32), heads_shape, heads_shape, sds(GROUP_W, BF16)]
    body = _inproj_sample_kernel
    if transposed is not None:
        wqt, tblk = transposed
        per = tblk // bm
        tr = pl.BlockSpec((None, None, GROUP_W, bm), lambda g, i: (g, i // per, 0, i % per))
        tr_shape = jax.ShapeDtypeStruct((G, R // tblk, GROUP_W, tblk), BF16)
        seg = jnp.asarray(np.arange(GROUP_W)[:, None] // HEAD_W == np.arange(HEAD_W)[None, :], BF16)
        args += [wqt, seg]
        in_specs += [full(wqt), full(seg)]
        out_specs += [tr, tr, pl.BlockSpec((None, None, 1, HEAD_W), lambda g, i: (g, i, 0, 0))]
        out_shape += [tr_shape, tr_shape, jax.ShapeDtypeStruct((G, nb, 1, HEAD_W), F32)]
        body = _inproj_prompt_kernel
    return pl.pallas_call(
        body,
        grid=(G, nb),
        in_specs=in_specs,
        out_specs=out_specs,
        out_shape=out_shape,
        compiler_params=_cparams(("arbitrary", "arbitrary")),
        name="inproj",
    )(*args)


def _hgrn_gates(zq, zf, lb):
    q = zq * _sigmoid(zq)
    f = lb + (1.0 - lb) * _sigmoid(zf)
    k = (1.0 - lb) * _sigmoid(-zf)
    return q, jnp.log(f), k


def _hgrn_kernel(z_ref, lb_ref, ng_ref, lvl_ref, tri_ref, o_ref, sfin_ref, st_ref, *, C):
    t = pl.program_id(1)

    @pl.when(t == 0)
    def _():
        st_ref[...] = jnp.zeros_like(st_ref)

    lvl = lvl_ref[...]
    tri = tri_ref[...]
    n_levels = int(math.log2(C)) - 3
    ng = ng_ref[...]
    for h in range(HEADS):
        cs = slice(h * HEAD_W, (h + 1) * HEAD_W)
        zq = z_ref[:, h * HEAD_W:(h + 1) * HEAD_W]
        zf = z_ref[:, GROUP_W + h * HEAD_W:GROUP_W + (h + 1) * HEAD_W]
        v = z_ref[:, 2 * GROUP_W + h * HEAD_W:2 * GROUP_W + (h + 1) * HEAD_W].astype(BF16)
        zg = z_ref[:, 3 * GROUP_W + h * HEAD_W:3 * GROUP_W + (h + 1) * HEAD_W]
        q, g, k = _hgrn_gates(zq, zf, lb_ref[:, cs])
        g1 = g.astype(BF16)
        r1 = g - g1.astype(F32)
        g2 = r1.astype(BF16)
        g3 = (r1 - g2.astype(F32)).astype(BF16)
        b = _dot(tri, g1) + _dot(tri, g2) + _dot(tri, g3)

        b8 = b.reshape(C // 8, 8, HEAD_W)
        bmid = jnp.broadcast_to(b8[:, 3:4, :], b8.shape).reshape(C, HEAD_W)
        e = jnp.clip(b - bmid, -EXP_CLAMP, EXP_CLAMP)
        a = jnp.where(lvl == 0, _dot_nt((q * jnp.exp(e)).astype(BF16), (k * jnp.exp(-e)).astype(BF16)), 0.0)
        for li in range(1, n_levels + 1):
            m = 4 << li
            bb = b.reshape(C // (2 * m), 2 * m, HEAD_W)
            d = b - jnp.broadcast_to(bb[:, m - 1:m, :], bb.shape).reshape(C, HEAD_W)
            qs = (q * jnp.exp(jnp.minimum(d, 0.0))).astype(BF16)
            ks = (k * jnp.exp(jnp.minimum(-d, 0.0))).astype(BF16)
            a = jnp.where(lvl == li, _dot_nt(qs, ks), a)

        st = st_ref[h]
        o = _dot(a.astype(BF16), v) + _dot_nt((q * jnp.exp(b)).astype(BF16), st.astype(BF16))
        b_last = b[C - 1:C, :]
        kd = (k * jnp.exp(b_last - b)).astype(BF16)
        st_ref[h] = st * jnp.exp(b_last) + _dot_tn(v, kd)

        ms = jnp.mean(o * o, -1, keepdims=True)
        o_ref[:, cs] = (o * lax.rsqrt(ms + RMS_EPS) * ng * (zg * _sigmoid(zg))).astype(BF16)

    @pl.when(t == pl.num_programs(1) - 1)
    def _():
        for h in range(HEADS):
            sfin_ref[h] = st_ref[h].T


def _hgrn_level_table(C):
    t = np.arange(C)[:, None]
    s = np.arange(C)[None, :]
    x = t ^ s
    lvl = np.zeros((C, C), np.int32)
    m = 8
    while m < C:
        lvl += (x >= m).astype(np.int32)
        m *= 2
    return np.where(s <= t, lvl, -1).astype(np.int32)


def _hgrn_prompt(zhg, lb, norm_g):
    B, T, _ = zhg.shape
    C = HG_CHUNK
    lvl = jnp.asarray(_hgrn_level_table(C))
    tri = jnp.asarray(np.tril(np.ones((C, C), np.float32)), BF16)
    full = lambda a: pl.BlockSpec(a.shape, lambda b, t: (0,) * a.ndim)
    return pl.pallas_call(
        functools.partial(_hgrn_kernel, C=C),
        grid=(B, T // C),
        in_specs=[pl.BlockSpec((None, C, HG_COLS), lambda b, t: (b, t, 0)), full(lb), full(norm_g), full(lvl), full(tri)],
        out_specs=[pl.BlockSpec((None, C, GROUP_W), lambda b, t: (b, t, 0)),
                   pl.BlockSpec((None, HEADS, HEAD_W, HEAD_W), lambda b, t: (b, 0, 0, 0))],
        out_shape=[jax.ShapeDtypeStruct((B, T, GROUP_W), BF16),
                   jax.ShapeDtypeStruct((B, HEADS, HEAD_W, HEAD_W), F32)],
        scratch_shapes=[pltpu.VMEM((HEADS, HEAD_W, HEAD_W), F32)],
        compiler_params=_cparams(("arbitrary", "arbitrary")),
        name="hgrn_prompt",
    )(zhg, lb, norm_g, lvl, tri)


def _hgrn_step_kernel(z_ref, s_ref, lb_ref, ng_ref, o_ref, so_ref, *, G):
    ng = ng_ref[...]
    for h in range(HEADS):
        cs = slice(h * HEAD_W, (h + 1) * HEAD_W)
        zq = z_ref[:, h * HEAD_W:(h + 1) * HEAD_W]
        zf = z_ref[:, GROUP_W + h * HEAD_W:GROUP_W + (h + 1) * HEAD_W]
        v = z_ref[:, 2 * GROUP_W + h * HEAD_W:2 * GROUP_W + (h + 1) * HEAD_W]
        zg = z_ref[:, 3 * GROUP_W + h * HEAD_W:3 * GROUP_W + (h + 1) * HEAD_W]
        lb = lb_ref[:, cs]
        q = zq * _sigmoid(zq)
        f = lb + (1.0 - lb) * _sigmoid(zf)
        k = (1.0 - lb) * _sigmoid(-zf)
        qT, fT, kT = q.T, f.T, k.T
        rows = []
        for j in range(G):
            s_new = fT[:, j:j + 1] * s_ref[j, h] + kT[:, j:j + 1] * v[j:j + 1, :]
            so_ref[j, h] = s_new
            rows.append(jnp.sum(s_new * qT[:, j:j + 1], axis=0, keepdims=True))
        o = jnp.concatenate(rows, axis=0)
        ms = jnp.mean(o * o, -1, keepdims=True)
        o_ref[:, cs] = (o * lax.rsqrt(ms + RMS_EPS) * ng * (zg * _sigmoid(zg))).astype(BF16)


def _hgrn_sample(zhg, state, lb, norm_g):
    N = zhg.shape[0]
    G = 8
    full = lambda a: pl.BlockSpec(a.shape, lambda i: (0,) * a.ndim)
    st_spec = pl.BlockSpec((G, HEADS, HEAD_W, HEAD_W), lambda i: (i, 0, 0, 0))
    return pl.pallas_call(
        functools.partial(_hgrn_step_kernel, G=G),
        grid=(N // G,),
        in_specs=[pl.BlockSpec((G, HG_COLS), lambda i: (i, 0)), st_spec, full(lb), full(norm_g)],
        out_specs=[pl.BlockSpec((G, GROUP_W), lambda i: (i, 0)), st_spec],
        out_shape=[jax.ShapeDtypeStruct((N, GROUP_W), BF16), jax.ShapeDtypeStruct(state.shape, F32)],
        compiler_params=_cparams(("arbitrary",)),
        name="hgrn_sample",
    )(zhg, state, lb, norm_g)


def _attn_kernel(qt_ref, k_ref, vt_ref, kpm_ref, sl_ref, feat_ref, lam_ref, gcol_ref, o_ref, m_ref, l_ref, acc_ref,
                 *, blk):
    qi = pl.program_id(2)
    row = lax.broadcasted_iota(I32, (HEAD_W, blk), 0)
    qt = qt_ref[...].astype(F32)
    a_hi = sl_ref[0:1, :]
    a_lo = sl_ref[1:2, :]
    slope2 = sl_ref[2:3, :]
    def slope_rows(r0):
        in_rows = (row >= r0) & (row < r0 + 4)
        return jnp.where(in_rows, jnp.where((row - r0) % 2 == 0, a_hi, a_lo), 0.0)

    q_aug = [jnp.where(row < DA_HEAD, qt, slope_rows(DA_HEAD)).astype(BF16),
             jnp.where(row >= DA_HEAD, qt, slope_rows(0)).astype(BF16)]
    keep = [feat_ref[0] > 0, feat_ref[1] > 0]
    feats = [feat_ref[2], feat_ref[3]]
    m_ref[...] = jnp.full_like(m_ref, -jnp.inf)
    l_ref[...] = jnp.zeros_like(l_ref)
    acc_ref[...] = jnp.zeros_like(acc_ref)

    def block(kj, masked):
        k_start = pl.multiple_of(kj * blk, blk)
        kb = k_ref[pl.ds(k_start, blk), :]
        vt = vt_ref[kj]
        off = slope2 * jnp.full((1, blk), k_start - qi * blk, I32).astype(F32)
        if masked:
            ok = lax.broadcasted_iota(I32, (blk, 1), 0) <= lax.broadcasted_iota(I32, (1, blk), 1)
        for mi in range(2):
            st = _dot(jnp.where(keep[mi], kb, feats[mi]), q_aug[mi])
            if masked:
                st = jnp.where(ok, st, -jnp.inf)
            m_prev = m_ref[mi]
            m_new = jnp.maximum(m_prev, jnp.max(st, axis=0, keepdims=True) + off)
            p = jnp.exp2(st - (m_new - off))
            alpha = jnp.exp2(m_prev - m_new)
            l_ref[mi] = alpha * l_ref[mi] + jnp.sum(p, axis=0, keepdims=True)
            acc_ref[mi] = alpha * acc_ref[mi] + _dot(vt, p.astype(BF16))
            m_ref[mi] = m_new

    block(qi, True)

    nblk = kpm_ref.shape[1]
    jlane = lax.broadcasted_iota(I32, (1, nblk), 1)
    far = slope2[:, 0:1] * ((jlane - qi) * blk + (blk - 1)).astype(F32)
    need = jlane < 0
    for mi in range(2):
        qm = qt[mi * DA_HEAD:(mi + 1) * DA_HEAD, :]
        qn = jnp.sqrt(jnp.max(jnp.sum(qm * qm, axis=0, keepdims=True), axis=1, keepdims=True))
        m_lo = jnp.min(m_ref[mi], axis=1, keepdims=True)
        need = need | (1.02 * qn * kpm_ref[mi:mi + 1, :] + 1.0 + far - m_lo >= -SKIP_LOG2)
    n_visit = jnp.sum((need & (jlane < qi)).astype(I32))

    def body(kj, c):
        block(kj, False)
        return c

    lax.fori_loop(qi - n_visit, qi, body, 0)

    ot = acc_ref[0] * (1.0 / l_ref[0]) - lam_ref[...] * (acc_ref[1] * (1.0 / l_ref[1]))
    ms = jnp.mean(ot * ot, axis=0, keepdims=True)
    o_ref[...] = (ot * lax.rsqrt(ms + RMS_EPS) * gcol_ref[...]).T.astype(BF16)


def _attn_prompt(qt, k, vt, kn2, lam, subln_g):
    B, nblk, _, blk = qt.shape
    T = k.shape[1]
    bf = lambda x: np.asarray(x, np.float32).astype(BF16).astype(np.float64)
    a = np.asarray(ALIBI_SLOPES, np.float64) * LOG2E
    a_hi = bf(a)
    a_lo = bf(a - a_hi)
    sl = np.zeros((HEADS, 8, blk), np.float32)
    sl[:, 0, :], sl[:, 1, :], sl[:, 2, :] = a_hi[:, None], a_lo[:, None], a.astype(np.float32)[:, None]
    lane = np.arange(HEAD_W)[None, :]
    r = np.arange(blk)[:, None]
    assert blk <= 256 * 256

    def pos_feat(l0):
        return (np.where((lane == l0) | (lane == l0 + 1), r // 256 * 256, 0)
                + np.where((lane == l0 + 2) | (lane == l0 + 3), r % 256, 0)).astype(np.float32)

    feat = np.stack([np.broadcast_to(lane < DA_HEAD, (blk, HEAD_W)).astype(np.float32),
                     np.broadcast_to(lane >= DA_HEAD, (blk, HEAD_W)).astype(np.float32),
                     pos_feat(DA_HEAD), pos_feat(0)])
    feat = jnp.asarray(feat, BF16)
    lam_row = jnp.full((1, blk), lam, F32)
    gcol = jnp.broadcast_to((subln_g.reshape(HEAD_W, 1) * (1.0 - LAM_INIT)), (HEAD_W, blk))
    kn = jnp.sqrt(1.02 * jnp.max(kn2[:, :, 0, :HEADS].reshape(B, nblk, -1, HEADS), axis=2))
    upto = np.arange(nblk)[:, None] >= np.arange(nblk)[None, :]
    kpm = jnp.max(jnp.where(upto[None, :, :, None], kn[:, None, :, :], 0.0), axis=2)
    kpm = jnp.broadcast_to(kpm.transpose(0, 2, 1)[:, :, None, :], (B, HEADS, 2, nblk))
    full = lambda x: pl.BlockSpec(x.shape, lambda b, h, i: (0,) * x.ndim)
    return pl.pallas_call(
        functools.partial(_attn_kernel, blk=blk),
        grid=(B, HEADS, nblk),
        in_specs=[pl.BlockSpec((None, None, HEAD_W, blk), lambda b, h, i: (b, i, h, 0)),
                  pl.BlockSpec((None, T, HEAD_W), lambda b, h, i: (b, 0, h)),
                  pl.BlockSpec((None, nblk, HEAD_W, blk), lambda b, h, i: (b, 0, h, 0)),
                  pl.BlockSpec((None, None, 2, nblk), lambda b, h, i: (b, h, 0, 0)),
                  pl.BlockSpec((None, 8, blk), lambda b, h, i: (h, 0, 0)),
                  full(feat), full(lam_row), full(gcol)],
        out_specs=pl.BlockSpec((None, blk, HEAD_W), lambda b, h, i: (b, i, h)),
        out_shape=jax.ShapeDtypeStruct((B, T, GROUP_W), BF16),
        scratch_shapes=[pltpu.VMEM((2, 1, blk), F32), pltpu.VMEM((2, 1, blk), F32), pltpu.VMEM((2, HEAD_W, blk), F32)],
        compiler_params=_cparams(("arbitrary", "arbitrary", "arbitrary")),
        name="attn_prompt",
    )(qt, k, vt, kpm, jnp.asarray(sl), feat, lam_row, gcol)


def _attn_decode_kernel(pt_ref, q_ref, kn_ref, vn_ref, bias_ref, lam_ref, g_ref, ck_hbm, cv_hbm, o_ref,
                        kbuf, vbuf, sems, *, n_pages):
    i = pl.program_id(0)
    slot = i % 2

    def page_copies(seq, s):
        for pg in range(n_pages):
            page = pt_ref[seq, pg]
            for h in range(HEADS):
                dst = (s, h, pl.ds(pg * PAGE_SIZE, PAGE_SIZE))
                yield pltpu.make_async_copy(ck_hbm.at[0, page, :, h, :], kbuf.at[dst], sems.at[0, s])
                yield pltpu.make_async_copy(cv_hbm.at[0, page, :, h, :], vbuf.at[dst], sems.at[1, s])

    @pl.when(i == 0)
    def _():
        for cp in page_copies(0, 0):
            cp.start()

    @pl.when(i + 1 < pl.num_programs(0))
    def _():
        for cp in page_copies(i + 1, 1 - slot):
            cp.start()

    for cp in page_copies(i, slot):
        cp.wait()

    r8 = lax.broadcasted_iota(I32, (8, GROUP_W), 0)
    c8 = lax.broadcasted_iota(I32, (8, GROUP_W), 1)
    qmat = jnp.where(c8 // DA_HEAD == r8, jnp.broadcast_to(q_ref[...].astype(F32), (8, GROUP_W)), 0.0)
    qmat_bf = qmat.astype(BF16)
    s = bias_ref[...]
    for h in range(HEADS):
        s = s + _dot_nt(qmat_bf[:, h * HEAD_W:(h + 1) * HEAD_W], kbuf[slot, h].astype(BF16))
    kn = kn_ref[...].astype(BF16).astype(F32)
    s_self = jnp.sum(qmat_bf.astype(F32) * kn, axis=-1, keepdims=True)
    m = jnp.maximum(jnp.max(s, -1, keepdims=True), s_self)
    p = jnp.exp(s - m)
    p_self = jnp.exp(s_self - m)
    inv_l = 1.0 / (jnp.sum(p, -1, keepdims=True) + p_self)
    coef = jnp.where(lax.broadcasted_iota(I32, (8, 1), 0) % 2 == 0, 1.0, -lam_ref[:, 0:1]) * inv_l
    w = (p * coef).astype(BF16)
    accs = [_dot(w, vbuf[slot, h].astype(BF16)) for h in range(HEADS)]
    acc = jnp.concatenate(accs, axis=1) + (p_self * coef) * vn_ref[...]
    o = jnp.sum(jnp.where(c8 // HEAD_W == r8 // 2, acc, 0.0), axis=0, keepdims=True)
    outs = []
    for h in range(HEADS):
        oh = o[:, h * HEAD_W:(h + 1) * HEAD_W]
        ms = jnp.mean(oh * oh, -1, keepdims=True)
        outs.append(oh * lax.rsqrt(ms + RMS_EPS) * g_ref[...] * (1.0 - LAM_INIT))
    o_ref[...] = jnp.concatenate(outs, axis=1).astype(BF16)


def _attn_sample(q, k_new, v_new, cache_k, cache_v, page_table, lam_row, subln_g):
    N, n_pages = page_table.shape
    past = n_pages * PAGE_SIZE
    kpos = np.arange(past, dtype=np.float32)[None, :]
    slope_rows = np.repeat(np.asarray(ALIBI_SLOPES, np.float32), 2)[:, None]
    bias = jnp.asarray(-slope_rows * (past - kpos))
    row = pl.BlockSpec((None, 1, GROUP_W), lambda i, pt: (i, 0, 0))
    full = lambda a: pl.BlockSpec(a.shape, lambda i, pt: (0,) * a.ndim)
    anyspec = pl.BlockSpec(memory_space=pl.ANY)
    grid_spec = pltpu.PrefetchScalarGridSpec(
        num_scalar_prefetch=1,
        grid=(N,),
        in_specs=[row, row, row, full(bias), full(lam_row), full(subln_g), anyspec, anyspec],
        out_specs=row,
        scratch_shapes=[pltpu.VMEM((2, HEADS, past, HEAD_W), F32), pltpu.VMEM((2, HEADS, past, HEAD_W), F32),
                        pltpu.SemaphoreType.DMA((2, 2))],
    )
    r3 = lambda a: a.reshape(N, 1, GROUP_W)
    out = pl.pallas_call(
        functools.partial(_attn_decode_kernel, n_pages=n_pages),
        grid_spec=grid_spec,
        out_shape=jax.ShapeDtypeStruct((N, 1, GROUP_W), BF16),
        compiler_params=_cparams(("arbitrary",)),
        name="attn_sample",
    )(page_table, r3(q), r3(k_new), r3(v_new), bias, lam_row, subln_g, cache_k, cache_v)
    return out.reshape(N, GROUP_W)


def _outproj_kernel(hg_ref, da_ref, x_ref, lg_ref, lb_ref, g1_ref, sc2_ref, sh2_ref, w_ref, l1g_ref, l1b_ref,
                    wrh_ref, wrl_ref, br_ref, u_ref, cin_ref,
                    x1_ref, h2_hbm, idx_ref, tw_ref, rank_ref, cout_ref, run_ref, hbuf, hsem, *, n_steps):
    step = pl.program_id(0) * pl.num_programs(1) + pl.program_id(1)
    slot = step % 2
    bm = x_ref.shape[0]
    h2_copies = lambda st, s: [_rows_copy(hbuf.at[s], h2_hbm, st * bm, bm, hsem.at[s], True)]

    @pl.when(step == 0)
    def _():
        run_ref[...] = cin_ref[...]

    x0 = _layer_norm(x_ref[...], lg_ref[...], lb_ref[...])
    mix = _dot(hg_ref[...], w_ref[0:GROUP_W, :]) + _dot(da_ref[...], w_ref[GROUP_W:2 * GROUP_W, :])
    x1 = _layer_norm(DEEPNORM_ALPHA * x0 + (1.0 + g1_ref[...]) * mix, l1g_ref[...], l1b_ref[...])
    x1_ref[...] = x1
    h2 = x1 * (1.0 + sc2_ref[...]) + sh2_ref[...]

    @pl.when(step >= 2)
    def _():
        for cp in h2_copies(step - 2, slot):
            cp.wait()

    hbuf[slot] = h2
    for cp in h2_copies(step, slot):
        cp.start()

    @pl.when(step == n_steps - 1)
    def _():
        tail = h2_copies(step, slot)
        if n_steps > 1:
            tail = h2_copies(step - 1, 1 - slot) + tail
        for cp in tail:
            cp.wait()

    hi = h2.astype(BF16)
    lo = (h2 - hi.astype(F32)).astype(BF16)
    wrh = wrh_ref[...]
    logits = _dot_nt(wrh, hi) + _dot_nt(wrh, lo) + _dot_nt(wrl_ref[...], hi) + br_ref[...]

    n_e, bm = logits.shape
    rows = lax.broadcasted_iota(I32, (n_e, bm), 0).astype(F32)
    vals, sels = [], []
    work = logits
    for kk in range(TOP_K):
        mx = jnp.max(work, axis=0, keepdims=True)
        ix = jnp.min(jnp.where(work == mx, rows, float(n_e)), axis=0, keepdims=True)
        sel = rows == ix
        idx_ref[kk:kk + 1, :] = ix.astype(I32)
        vals.append(mx)
        sels.append(sel)
        work = jnp.where(sel, -jnp.inf, work)
    es = [jnp.exp(vv - vals[0]) for vv in vals]
    inv = 1.0 / (es[0] + es[1] + es[2] + es[3])
    for kk in range(TOP_K):
        tw_ref[kk:kk + 1, :] = es[kk] * inv

    base = run_ref[...]
    for kk in range(TOP_K):
        oh = jnp.where(sels[kk], 1.0, 0.0)
        before = base + _dot(oh.astype(BF16), u_ref[...])
        rank_ref[kk:kk + 1, :] = jnp.sum(jnp.where(sels[kk], before, 0.0), axis=0, keepdims=True).astype(I32)
        base = base + jnp.sum(oh, axis=1, keepdims=True)
    run_ref[...] = base
    cout_ref[...] = base


def _outproj(hg, da, x, mods, consts, counts_in, bm):
    G, R, D = x.shape
    g1, sc2, sh2 = mods
    ln_g, ln_b, w_out_bf, l1g, l1b, wrh, wrl, br = consts
    nb = R // bm
    n_tok = G * R
    u = jnp.asarray(np.triu(np.ones((bm, bm), np.float32), 1), BF16)
    row = lambda w: pl.BlockSpec((None, bm, w), lambda g, i: (g, i, 0))
    full = lambda a: pl.BlockSpec(a.shape, lambda g, i: (0,) * a.ndim)
    tok_lanes = pl.BlockSpec((TOP_K, bm), lambda g, i: (0, g * nb + i))
    return pl.pallas_call(
        functools.partial(_outproj_kernel, n_steps=G * nb),
        grid=(G, nb),
        in_specs=[row(GROUP_W), row(GROUP_W), row(D), full(ln_g), full(ln_b),
                  _mod_spec(g1, bm), _mod_spec(sc2, bm), _mod_spec(sh2, bm),
                  full(w_out_bf), full(l1g), full(l1b), full(wrh), full(wrl), full(br), full(u), full(counts_in)],
        out_specs=[row(D), pl.BlockSpec(memory_space=pl.ANY), tok_lanes, tok_lanes, tok_lanes, full(counts_in)],
        out_shape=[jax.ShapeDtypeStruct((G, R, D), F32),
                   jax.ShapeDtypeStruct((n_tok,) + ROW_SHAPE, F32),
                   jax.ShapeDtypeStruct((TOP_K, n_tok), I32),
                   jax.ShapeDtypeStruct((TOP_K, n_tok), F32),
                   jax.ShapeDtypeStruct((TOP_K, n_tok), I32),
                   jax.ShapeDtypeStruct(counts_in.shape, F32)],
        scratch_shapes=[pltpu.VMEM(counts_in.shape, F32), pltpu.VMEM((2, bm, D), F32), pltpu.SemaphoreType.DMA((2,))],
        compiler_params=_cparams(("arbitrary", "arbitrary")),
        name="outproj",
    )(hg, da, x, ln_g, ln_b, g1, sc2, sh2, w_out_bf, l1g, l1b, wrh, wrl, br, u, counts_in)


def _dispatch_kernel(pe_ref, nu_ref, dest_ref, h2p_ref, h2s_ref, xb_hbm, zbuf, zsem, sem, *, bt, n_prompt_steps, nb):
    i = pl.program_id(0)
    bm = MOE_BLOCK

    @pl.when(i == 0)
    def _():
        zbuf[...] = jnp.zeros_like(zbuf)
        zero_block = lambda row0: pltpu.make_async_copy(zbuf, xb_hbm.at[pl.ds(row0, bm)], zsem)
        has_rows = [pe_ref[e] > (pe_ref[e - 1] if e else 0) for e in range(N_EXPERTS)]
        for e in range(N_EXPERTS):
            @pl.when(has_rows[e])
            def _():
                zero_block(pe_ref[e] - bm).start()

        def tail_start(j, c):
            zero_block(j * bm).start()
            return c

        def tail_wait(j, c):
            zero_block(j * bm).wait()
            return c

        lax.fori_loop(nu_ref[0], nb, tail_start, 0)
        for e in range(N_EXPERTS):
            @pl.when(has_rows[e])
            def _():
                zero_block(pe_ref[e] - bm).wait()
        lax.fori_loop(nu_ref[0], nb, tail_wait, 0)

    def scatter(h2_ref):
        def body(t, c):
            for kk in range(TOP_K):
                pltpu.make_async_copy(h2_ref.at[t], xb_hbm.at[dest_ref[0, t * TOP_K + kk]], sem).start(priority=kk % 2)
            return c

        lax.fori_loop(0, bt, body, 0, unroll=4)
        for kk in range(TOP_K):
            pltpu.make_async_copy(h2_ref, xb_hbm.at[pl.ds(0, bt)], sem).wait()

    @pl.when(i < n_prompt_steps)
    def _():
        scatter(h2p_ref)

    @pl.when(i >= n_prompt_steps)
    def _():
        scatter(h2s_ref)


def _dispatch(pad_end, n_used, dest, h2_p, h2_s, n_rows):
    bt = TOK_BLOCK
    n_p, n_s = h2_p.shape[0] // bt, h2_s.shape[0] // bt
    nb = n_rows // MOE_BLOCK
    grid_spec = pltpu.PrefetchScalarGridSpec(
        num_scalar_prefetch=2,
        grid=(n_p + n_s,),
        in_specs=[pl.BlockSpec((None, 1, TOP_K * bt), lambda i, pe, nu: (i, 0, 0), memory_space=pltpu.SMEM),
                  pl.BlockSpec((bt,) + ROW_SHAPE, lambda i, pe, nu: (jnp.minimum(i, n_p - 1), 0, 0)),
                  pl.BlockSpec((bt,) + ROW_SHAPE, lambda i, pe, nu: (jnp.maximum(i - n_p, 0), 0, 0))],
        out_specs=pl.BlockSpec(memory_space=pl.ANY),
        scratch_shapes=[pltpu.VMEM((MOE_BLOCK,) + ROW_SHAPE, F32), pltpu.SemaphoreType.DMA(()),
                        pltpu.SemaphoreType.DMA(())],
    )
    return pl.pallas_call(
        functools.partial(_dispatch_kernel, bt=bt, n_prompt_steps=n_p, nb=nb),
        grid_spec=grid_spec,
        out_shape=jax.ShapeDtypeStruct((n_rows,) + ROW_SHAPE, F32),
        compiler_params=_cparams(("arbitrary",)),
        name="dispatch",
    )(pad_end, n_used, dest, h2_p, h2_s)


def _experts_kernel(be_ref, nu_ref, nxt_ref, par_ref, xb_hbm, wu_hbm, bu_ref, wd_hbm, bd_ref, yb_hbm,
                    wu_bf, wd_bf, wu_f32, wd_f32, xbuf, ybuf, sems, wsems, *, bm):
    j = pl.program_id(0)
    nb = pl.num_programs(0)
    slot = j % 2
    nu = nu_ref[0]
    active = j < nu
    x_copies = lambda blk, s: [_rows_copy(xbuf.at[s], xb_hbm, blk * bm, bm, sems.at[0, s], False)]
    y_copies = lambda blk, s: [_rows_copy(ybuf.at[s], yb_hbm, blk * bm, bm, sems.at[1, s], True)]
    w_copies = lambda e, s: [pltpu.make_async_copy(wu_hbm.at[e], wu_f32.at[s], wsems.at[0, s]),
                             pltpu.make_async_copy(wd_hbm.at[e], wd_f32.at[s], wsems.at[1, s])]

    @pl.when(j == 0)
    def _():
        for cp in x_copies(0, 0) + w_copies(be_ref[0], par_ref[0]):
            cp.start()

    @pl.when(j + 1 < nu)
    def _():
        for cp in x_copies(j + 1, 1 - slot):
            cp.start()

    @pl.when(j >= 2)
    def _():
        for cp in y_copies(j - 2, slot):
            cp.wait()

    @pl.when(active & ((j == 0) | (be_ref[j] != be_ref[jnp.maximum(j - 1, 0)])))
    def _():
        half = par_ref[j]
        for cp in w_copies(be_ref[j], half):
            cp.wait()
        wu_bf[...] = wu_f32[half].astype(BF16)
        wd_bf[...] = wd_f32[half].astype(BF16)

        @pl.when(nxt_ref[j] >= 0)
        def _():
            for cp in w_copies(nxt_ref[j], 1 - half):
                cp.start()

    @pl.when(active)
    def _():
        for cp in x_copies(j, slot):
            cp.wait()
        u = _dot(xbuf[slot].astype(BF16), wu_bf[...]) + bu_ref[...]
        glu = jnp.minimum(u[:, :D_FF], SWIGLU_LIMIT)
        lin = jnp.clip(u[:, D_FF:], -SWIGLU_LIMIT, SWIGLU_LIMIT)
        act = glu * _sigmoid(SWIGLU_ALPHA * glu) * (lin + 1.0)
        ybuf[slot] = _dot(act.astype(BF16), wd_bf[...]) + bd_ref[...]

    @pl.when(jnp.logical_not(active))
    def _():
        ybuf[slot] = jnp.zeros((bm, D_MODEL), F32)

    for cp in y_copies(j, slot):
        cp.start()

    @pl.when(j == nb - 1)
    def _():
        for cp in y_copies(j - 1, 1 - slot) + y_copies(j, slot):
            cp.wait()


def _experts(blk_exp, n_used, blk_next, blk_half, xb, w_up, b_up, w_down, b_down):
    n_rows = xb.shape[0]
    bm = MOE_BLOCK
    nb = n_rows // bm
    anyspec = pl.BlockSpec(memory_space=pl.ANY)
    grid_spec = pltpu.PrefetchScalarGridSpec(
        num_scalar_prefetch=4,
        grid=(nb,),
        in_specs=[anyspec, anyspec,
                  pl.BlockSpec((None, 1, 2 * D_FF), lambda j, be, nu, nx, hf: (be[j], 0, 0)),
                  anyspec,
                  pl.BlockSpec((None, 1, D_MODEL), lambda j, be, nu, nx, hf: (be[j], 0, 0))],
        out_specs=anyspec,
        scratch_shapes=[pltpu.VMEM((D_MODEL, 2 * D_FF), BF16), pltpu.VMEM((D_FF, D_MODEL), BF16),
                        pltpu.VMEM((2, D_MODEL, 2 * D_FF), F32), pltpu.VMEM((2, D_FF, D_MODEL), F32),
                        pltpu.VMEM((2, bm, D_MODEL), F32), pltpu.VMEM((2, bm, D_MODEL), F32),
                        pltpu.SemaphoreType.DMA((2, 2)), pltpu.SemaphoreType.DMA((2, 2))],
    )
    return pl.pallas_call(
        functools.partial(_experts_kernel, bm=bm),
        grid_spec=grid_spec,
        out_shape=jax.ShapeDtypeStruct((n_rows,) + ROW_SHAPE, F32),
        compiler_params=_cparams(("arbitrary",)),
        name="experts",
    )(blk_exp, n_used, blk_next, blk_half, xb, w_up, b_up.reshape(N_EXPERTS, 1, 2 * D_FF), w_down,
      b_down.reshape(N_EXPERTS, 1, D_MODEL))


def _combine_kernel(dcur_ref, dnext_ref, x1_ref, tw_ref, g2_ref, lg_ref, lb_ref, yb_hbm, o_ref, buf, sems, *, bm):
    i = pl.program_id(0) * pl.num_programs(1) + pl.program_id(1)
    n = pl.num_programs(0) * pl.num_programs(1)
    slot = i % 2

    def issue(dref, s):
        def body(a, c):
            for b in range(8):
                for kk in range(TOP_K):
                    pltpu.make_async_copy(yb_hbm.at[dref[0, (a * 8 + b) * TOP_K + kk]], buf.at[s, kk, a, pl.ds(b, 1), :],
                                          sems.at[s]).start(priority=kk % 2)
            return c
        lax.fori_loop(0, bm // 8, body, 0)

    @pl.when(i == 0)
    def _():
        issue(dcur_ref, 0)

    @pl.when(i + 1 < n)
    def _():
        issue(dnext_ref, 1 - slot)

    for kk in range(TOP_K):
        for a in range(bm // 8):
            pltpu.make_async_copy(yb_hbm.at[pl.ds(0, 8), 0, :], buf.at[slot, kk, a], sems.at[slot]).wait()

    tw = tw_ref[...]
    rows_of = lambda kk: buf[slot, kk].reshape(bm, D_MODEL)
    ff = tw[:, 0:1] * rows_of(0)
    for kk in range(1, TOP_K):
        ff = ff + tw[:, kk:kk + 1] * rows_of(kk)
    o_ref[...] = _layer_norm(DEEPNORM_ALPHA * x1_ref[...] + (1.0 + g2_ref[...]) * ff, lg_ref[...], lb_ref[...])


def _combine(dest, tw_rows, x1, g2, ln_g, ln_b, yb, bm):
    G, R, D = x1.shape
    nb = R // bm
    n_blk = G * nb
    row = lambda w: pl.BlockSpec((None, bm, w), lambda g, i: (g, i, 0))
    full = lambda a: pl.BlockSpec(a.shape, lambda g, i: (0,) * a.ndim)
    cur = pl.BlockSpec((None, 1, TOP_K * bm), lambda g, i: (g * nb + i, 0, 0), memory_space=pltpu.SMEM)
    nxt = pl.BlockSpec((None, 1, TOP_K * bm), lambda g, i: (jnp.minimum(g * nb + i + 1, n_blk - 1), 0, 0),
                       memory_space=pltpu.SMEM)
    return pl.pallas_call(
        functools.partial(_combine_kernel, bm=bm),
        grid=(G, nb),
        in_specs=[cur, nxt, row(D), pl.BlockSpec((bm, TOP_K), lambda g, i: (g * nb + i, 0)),
                  _mod_spec(g2, bm), full(ln_g), full(ln_b), pl.BlockSpec(memory_space=pl.ANY)],
        out_specs=row(D),
        out_shape=jax.ShapeDtypeStruct((G, R, D), F32),
        scratch_shapes=[pltpu.VMEM((2, TOP_K, bm // 8, 8, D), F32), pltpu.SemaphoreType.DMA((2,))],
        compiler_params=_cparams(("arbitrary", "arbitrary")),
        name="combine",
    )(dest, dest, x1, tw_rows, g2, ln_g, ln_b, yb)


def kernel(x_prompt, x_sample, c_prompt, c_sample, cache_k, cache_v, state_hgrn, page_table, ln_in_g, ln_in_b, w_ada, b_ada, w_in, hg_lb, hg_norm_g, da_lq1, da_lk1, da_lq2, da_lk2, da_subln_g, w_out, ln1_g, ln1_b, w_router, b_router, w_up, b_up, w_down, b_down, ln2_g, ln2_b):
    assert w_in.shape[0] == 1, "single-layer trunk"
    B, T, D = x_prompt.shape
    NS = x_sample.shape[0]
    n_prompt = B * T
    n_tot = n_prompt + NS
    r1 = lambda a: a.reshape(1, -1)

    lb = r1(jax.nn.softmax(hg_lb.astype(F32), axis=0)[0])
    lam = (jnp.exp(jnp.sum(da_lq1[0].astype(F32) * da_lk1[0].astype(F32)))
           - jnp.exp(jnp.sum(da_lq2[0].astype(F32) * da_lk2[0].astype(F32))) + LAM_INIT)
    lam_row = jnp.full((1, HEAD_W), lam, F32)
    ln_g, ln_b = r1(ln_in_g), r1(ln_in_b)
    w_in_bf = w_in[0].astype(BF16)
    wqt = w_in_bf[:, HG_COLS:HG_COLS + GROUP_W].T
    w_out_bf = w_out[0].astype(BF16)
    wr_t = w_router[0].T
    wrh = wr_t.astype(BF16)
    wrl = (wr_t - wrh.astype(F32)).astype(BF16)
    br = b_router[0].reshape(N_EXPERTS, 1)
    norm_g, subln_g = r1(hg_norm_g[0]), r1(da_subln_g[0])

    n_c = B + NS
    c_all = jnp.concatenate([c_prompt, c_sample, jnp.zeros((-n_c % 8, D), F32)], axis=0)
    mod = _ada(c_all, w_ada[0], b_ada[0])
    mod_p = [mod[:B, j * D:(j + 1) * D].reshape(B, 1, D) for j in range(6)]
    mod_s = [mod[B:n_c, j * D:(j + 1) * D].reshape(1, NS, D) for j in range(6)]

    zhg_p, kf_p, vf_p, kb_p, qt_p, vt_p, kn2_p = _inproj(x_prompt, mod_p[1], mod_p[0], ln_g, ln_b, w_in_bf, INPROJ_BLOCK,
                                                  (wqt, ATTN_BLOCK))
    hg_p, s_p = _hgrn_prompt(zhg_p, lb, norm_g)
    da_p = _attn_prompt(qt_p, kb_p, vt_p, kn2_p, lam, subln_g)

    xs = x_sample.reshape(1, NS, D)
    zhg_s, kf_s, vf_s, q_s = _inproj(xs, mod_s[1], mod_s[0], ln_g, ln_b, w_in_bf, NS)
    hg_s, s_s = _hgrn_sample(zhg_s[0], state_hgrn[0], lb, norm_g)
    da_s = _attn_sample(q_s[0], kf_s.reshape(NS, GROUP_W), vf_s.reshape(NS, GROUP_W), cache_k, cache_v, page_table,
                        lam_row, subln_g)

    consts = (ln_g, ln_b, w_out_bf, r1(ln1_g[0]), r1(ln1_b[0]), wrh, wrl, br)
    counts0 = jnp.zeros((N_EXPERTS, 1), F32)
    x1_p, h2_p, idx_p, tw_p, rank_p, counts1 = _outproj(
        hg_p, da_p, x_prompt, (mod_p[2], mod_p[4], mod_p[3]), consts, counts0, ROW_BLOCK)
    x1_s, h2_s, idx_s, tw_s, rank_s, counts = _outproj(
        hg_s.reshape(1, NS, GROUP_W), da_s.reshape(1, NS, GROUP_W), xs, (mod_s[2], mod_s[4], mod_s[3]), consts,
        counts1, NS)

    cnt = counts[:, 0].astype(I32)
    padded = (cnt + MOE_BLOCK - 1) // MOE_BLOCK * MOE_BLOCK
    pad_end = jnp.cumsum(padded)
    pad_start = pad_end - padded
    e_ids = jnp.arange(N_EXPERTS, dtype=I32)

    def slot_of(idx, rank):
        return jnp.sum(jnp.where(idx[..., None] == e_ids, pad_start, 0), axis=-1) + rank

    dest_p = slot_of(idx_p, rank_p)
    dest_s = slot_of(idx_s, rank_s)
    n_blocks = -(-n_tot * TOP_K // MOE_BLOCK) + N_EXPERTS
    blk_row0 = jnp.arange(n_blocks, dtype=I32) * MOE_BLOCK
    blk_exp = jnp.minimum(jnp.sum((pad_end[None, :] <= blk_row0[:, None]).astype(I32), axis=1), N_EXPERTS - 1)
    n_used = (pad_end[-1:] // MOE_BLOCK).astype(I32)

    by_block = lambda d: d.T.reshape(-1, 1, TOK_BLOCK * TOP_K)
    slots_p, slots_s = by_block(dest_p), by_block(dest_s)
    xb = _dispatch(pad_end.astype(I32), n_used, jnp.concatenate([slots_p, slots_s], axis=0), h2_p, h2_s,
                   n_blocks * MOE_BLOCK)
    has_rows = padded > 0
    later = (e_ids[None, :] > e_ids[:, None]) & has_rows[None, :]
    next_e = jnp.where(jnp.any(later, axis=1), jnp.min(jnp.where(later, e_ids[None, :], N_EXPERTS), axis=1), -1)
    half_e = (jnp.cumsum(has_rows.astype(I32)) - 1) % 2
    per_block = lambda tab: jnp.sum(jnp.where(blk_exp[:, None] == e_ids, tab, 0), axis=1).astype(I32)
    yb = _experts(blk_exp, n_used, per_block(next_e), per_block(half_e), xb, w_up[0], b_up[0], w_down[0], b_down[0])

    l2g, l2b = r1(ln2_g[0]), r1(ln2_b[0])
    y_p = _combine(slots_p, tw_p.T, x1_p, mod_p[5], l2g, l2b, yb, TOK_BLOCK)
    y_s = _combine(slots_s, tw_s.T, x1_s, mod_s[5], l2g, l2b, yb, TOK_BLOCK)

    smp = lambda a: a.reshape(1, NS, 1, HEADS, HEAD_W)
    return (y_p, y_s.reshape(NS, 1, D), kf_p[None], vf_p[None], s_p[None], smp(kf_s), smp(vf_s), s_s[None])
```

```python
import functools
import math

import numpy as np
import jax
import jax.numpy as jnp
from jax import lax
from jax.experimental import pallas as pl
from jax.experimental.pallas import tpu as pltpu

F32, BF16, I32 = jnp.float32, jnp.bfloat16, jnp.int32

D_MODEL = 1024
HEADS = 4
HEAD_W = 128
DA_HEAD = 64
GROUP_W = HEADS * HEAD_W
HG_COLS = 4 * GROUP_W
N_EXPERTS = 32
TOP_K = 4
D_FF = 1024
SWIGLU_ALPHA = 1.702
SWIGLU_LIMIT = 7.0
DEEPNORM_ALPHA = 2.0 ** 0.25
LN_EPS = 1e-5
RMS_EPS = 1e-6
LAM_INIT = 0.8 - 0.6 * math.exp(-0.3 * 0)
LOG2E = math.log2(math.e)
ALIBI_SLOPES = tuple(2.0 ** (-8.0 * (h + 1) / HEADS) for h in range(HEADS))
PAGE_SIZE = 128
EXP_CLAMP = 80.0
SKIP_LOG2 = 151.0

VMEM_LIMIT = 56 * 1024 * 1024
SUBLANES = 8
ADA_COLS = 1536
MOE_BLOCK = 256
HG_CHUNK = 256
ATTN_BLOCK = 1024
INPROJ_BLOCK = 512
ROW_BLOCK = 512
TOK_BLOCK = 128


def _cparams(sem):
    return pltpu.CompilerParams(dimension_semantics=sem, vmem_limit_bytes=VMEM_LIMIT)


def _sigmoid(x):
    return 1.0 / (1.0 + jnp.exp(-x))


def _layer_norm(x, g, b):
    mu = jnp.mean(x, -1, keepdims=True)
    xc = x - mu
    var = jnp.mean(xc * xc, -1, keepdims=True)
    return xc * lax.rsqrt(var + LN_EPS) * g + b


def _dot(a, b):
    return jnp.dot(a, b, preferred_element_type=F32)


def _dot_nt(a, b):
    return lax.dot_general(a, b, (((1,), (1,)), ((), ())), preferred_element_type=F32)


def _dot_tn(a, b):
    return lax.dot_general(a, b, (((0,), (0,)), ((), ())), preferred_element_type=F32)


ROW_SHAPE = (1, D_MODEL)


def _rows_copy(mat_ref, rows_hbm, row0, n, sem, to_hbm):
    rows = rows_hbm.at[pl.ds(row0, n), 0, :]
    return pltpu.make_async_copy(mat_ref, rows, sem) if to_hbm else pltpu.make_async_copy(rows, mat_ref, sem)


def _ada_kernel(c_ref, w_ref, b_ref, o_ref):
    c = c_ref[...]
    a = (c * _sigmoid(c)).astype(BF16)
    o_ref[...] = _dot(a, w_ref[...].astype(BF16)) + b_ref[...]


def _ada(c, w_ada, b_ada):
    rows, d = c.shape
    n = w_ada.shape[1]
    bn = ADA_COLS
    return pl.pallas_call(
        _ada_kernel,
        grid=(n // bn,),
        in_specs=[pl.BlockSpec((rows, d), lambda j: (0, 0)),
                  pl.BlockSpec((d, bn), lambda j: (0, j)),
                  pl.BlockSpec((1, bn), lambda j: (0, j))],
        out_specs=pl.BlockSpec((rows, bn), lambda j: (0, j)),
        out_shape=jax.ShapeDtypeStruct((rows, n), F32),
        compiler_params=_cparams(("arbitrary",)),
        name="ada",
    )(c, w_ada, b_ada.reshape(1, n))


def _inproj_common(x_ref, g_ref, b_ref, sc_ref, sh_ref, w_ref, zhg_ref, kf_ref, vf_ref):
    x0 = _layer_norm(x_ref[...], g_ref[...], b_ref[...])
    h = (x0 * (1.0 + sc_ref[...]) + sh_ref[...]).astype(BF16)
    zhg_ref[...] = _dot(h, w_ref[:, 0:HG_COLS])
    c0 = HG_COLS + GROUP_W
    k = _dot(h, w_ref[:, c0:c0 + GROUP_W])
    v = _dot(h, w_ref[:, c0 + GROUP_W:c0 + 2 * GROUP_W])
    for hd in range(HEADS):
        kf_ref[:, hd, :] = k[:, hd * HEAD_W:(hd + 1) * HEAD_W]
        vf_ref[:, hd, :] = v[:, hd * HEAD_W:(hd + 1) * HEAD_W]
    return h, k, v


def _inproj_prompt_kernel(x_ref, g_ref, b_ref, sc_ref, sh_ref, w_ref, wqt_ref, seg_ref,
                          zhg_ref, kf_ref, vf_ref, kb_ref, qt_ref, vt_ref, kn_ref):
    h, k, v = _inproj_common(x_ref, g_ref, b_ref, sc_ref, sh_ref, w_ref, zhg_ref, kf_ref, vf_ref)
    kb_ref[...] = k.astype(BF16)
    kn_ref[...] = jnp.max(_dot((k * k).astype(BF16), seg_ref[...]), axis=0, keepdims=True)
    qt_ref[...] = (_dot_nt(wqt_ref[...], h) * (DA_HEAD ** -0.5 * LOG2E)).astype(BF16)
    vt_ref[...] = v.T.astype(BF16)


def _inproj_sample_kernel(x_ref, g_ref, b_ref, sc_ref, sh_ref, w_ref, zhg_ref, kf_ref, vf_ref, q_ref):
    h, _, _ = _inproj_common(x_ref, g_ref, b_ref, sc_ref, sh_ref, w_ref, zhg_ref, kf_ref, vf_ref)
    q_ref[...] = (_dot(h, w_ref[:, HG_COLS:HG_COLS + GROUP_W]) * (DA_HEAD ** -0.5)).astype(BF16)


def _mod_spec(mod, bm):
    if mod.shape[1] == 1:
        return pl.BlockSpec((None, 1, mod.shape[2]), lambda g, i: (g, 0, 0))
    return pl.BlockSpec((None, bm, mod.shape[2]), lambda g, i: (g, i, 0))


def _inproj(x, sc, sh, ln_g, ln_b, w_in_bf, bm, transposed=None):
    G, R, D = x.shape
    nb = R // bm
    row = lambda w: pl.BlockSpec((None, bm, w), lambda g, i: (g, i, 0))
    full = lambda a: pl.BlockSpec(a.shape, lambda g, i: (0,) * a.ndim)
    heads = pl.BlockSpec((None, bm, HEADS, HEAD_W), lambda g, i: (g, i, 0, 0))
    heads_shape = jax.ShapeDtypeStruct((G, R, HEADS, HEAD_W), F32)
    sds = lambda w, dt: jax.ShapeDtypeStruct((G, R, w), dt)
    args = [x, ln_g, ln_b, sc, sh, w_in_bf]
    in_specs = [row(D), full(ln_g), full(ln_b), _mod_spec(sc, bm), _mod_spec(sh, bm), full(w_in_bf)]
    out_specs = [row(HG_COLS), heads, heads, row(GROUP_W)]
    out_shape = [sds(HG_COLS, F32), heads_shape, heads_shape, sds(GROUP_W, BF16)]
    body = _inproj_sample_kernel
    if transposed is not None:
        wqt, tblk = transposed
        per = tblk // bm
        tr = pl.BlockSpec((None, None, GROUP_W, bm), lambda g, i: (g, i // per, 0, i % per))
        tr_shape = jax.ShapeDtypeStruct((G, R // tblk, GROUP_W, tblk), BF16)
        seg = jnp.asarray(np.arange(GROUP_W)[:, None] // HEAD_W == np.arange(HEAD_W)[None, :], BF16)
        args += [wqt, seg]
        in_specs += [full(wqt), full(seg)]
        out_specs += [tr, tr, pl.BlockSpec((None, None, 1, HEAD_W), lambda g, i: (g, i, 0, 0))]
        out_shape += [tr_shape, tr_shape, jax.ShapeDtypeStruct((G, nb, 1, HEAD_W), F32)]
        body = _inproj_prompt_kernel
    return pl.pallas_call(
        body,
        grid=(G, nb),
        in_specs=in_specs,
        out_specs=out_specs,
        out_shape=out_shape,
        compiler_params=_cparams(("arbitrary", "arbitrary")),
        name="inproj",
    )(*args)


def _hgrn_gates(zq, zf, lb):
    q = zq * _sigmoid(zq)
    f = lb + (1.0 - lb) * _sigmoid(zf)
    k = (1.0 - lb) * _sigmoid(-zf)
    return q, jnp.log(f), k


def _hgrn_kernel(z_ref, lb_ref, ng_ref, lvl_ref, tri_ref, o_ref, sfin_ref, st_ref, *, C):
    t = pl.program_id(1)

    @pl.when(t == 0)
    def _():
        st_ref[...] = jnp.zeros_like(st_ref)

    lvl = lvl_ref[...]
    tri = tri_ref[...]
    n_levels = int(math.log2(C)) - 3
    ng = ng_ref[...]
    for h in range(HEADS):
        cs = slice(h * HEAD_W, (h + 1) * HEAD_W)
        zq = z_ref[:, h * HEAD_W:(h + 1) * HEAD_W]
        zf = z_ref[:, GROUP_W + h * HEAD_W:GROUP_W + (h + 1) * HEAD_W]
        v = z_ref[:, 2 * GROUP_W + h * HEAD_W:2 * GROUP_W + (h + 1) * HEAD_W].astype(BF16)
        zg = z_ref[:, 3 * GROUP_W + h * HEAD_W:3 * GROUP_W + (h + 1) * HEAD_W]
        q, g, k = _hgrn_gates(zq, zf, lb_ref[:, cs])
        g1 = g.astype(BF16)
        r1 = g - g1.astype(F32)
        g2 = r1.astype(BF16)
        g3 = (r1 - g2.astype(F32)).astype(BF16)
        b = _dot(tri, g1) + _dot(tri, g2) + _dot(tri, g3)

        b8 = b.reshape(C // SUBLANES, SUBLANES, HEAD_W)
        mid = SUBLANES // 2 - 1
        bmid = jnp.broadcast_to(b8[:, mid:mid + 1, :], b8.shape).reshape(C, HEAD_W)
        e = jnp.clip(b - bmid, -EXP_CLAMP, EXP_CLAMP)
        a = jnp.where(lvl == 0, _dot_nt((q * jnp.exp(e)).astype(BF16), (k * jnp.exp(-e)).astype(BF16)), 0.0)
        for li in range(1, n_levels + 1):
            m = 4 << li
            bb = b.reshape(C // (2 * m), 2 * m, HEAD_W)
            d = b - jnp.broadcast_to(bb[:, m - 1:m, :], bb.shape).reshape(C, HEAD_W)
            qs = (q * jnp.exp(jnp.minimum(d, 0.0))).astype(BF16)
            ks = (k * jnp.exp(jnp.minimum(-d, 0.0))).astype(BF16)
            a = jnp.where(lvl == li, _dot_nt(qs, ks), a)

        st = st_ref[h]
        o = _dot(a.astype(BF16), v) + _dot_nt((q * jnp.exp(b)).astype(BF16), st.astype(BF16))
        b_last = b[C - 1:C, :]
        kd = (k * jnp.exp(b_last - b)).astype(BF16)
        st_ref[h] = st * jnp.exp(b_last) + _dot_tn(v, kd)

        ms = jnp.mean(o * o, -1, keepdims=True)
        o_ref[:, cs] = (o * lax.rsqrt(ms + RMS_EPS) * ng * (zg * _sigmoid(zg))).astype(BF16)

    @pl.when(t == pl.num_programs(1) - 1)
    def _():
        for h in range(HEADS):
            sfin_ref[h] = st_ref[h].T


def _hgrn_level_table(C):
    t = np.arange(C)[:, None]
    s = np.arange(C)[None, :]
    x = t ^ s
    lvl = np.zeros((C, C), np.int32)
    m = 8
    while m < C:
        lvl += (x >= m).astype(np.int32)
        m *= 2
    return np.where(s <= t, lvl, -1).astype(np.int32)


def _hgrn_prompt(zhg, lb, norm_g):
    B, T, _ = zhg.shape
    C = HG_CHUNK
    lvl = jnp.asarray(_hgrn_level_table(C))
    tri = jnp.asarray(np.tril(np.ones((C, C), np.float32)), BF16)
    full = lambda a: pl.BlockSpec(a.shape, lambda b, t: (0,) * a.ndim)
    return pl.pallas_call(
        functools.partial(_hgrn_kernel, C=C),
        grid=(B, T // C),
        in_specs=[pl.BlockSpec((None, C, HG_COLS), lambda b, t: (b, t, 0)), full(lb), full(norm_g), full(lvl),
                  full(tri)],
        out_specs=[pl.BlockSpec((None, C, GROUP_W), lambda b, t: (b, t, 0)),
                   pl.BlockSpec((None, HEADS, HEAD_W, HEAD_W), lambda b, t: (b, 0, 0, 0))],
        out_shape=[jax.ShapeDtypeStruct((B, T, GROUP_W), BF16),
                   jax.ShapeDtypeStruct((B, HEADS, HEAD_W, HEAD_W), F32)],
        scratch_shapes=[pltpu.VMEM((HEADS, HEAD_W, HEAD_W), F32)],
        compiler_params=_cparams(("arbitrary", "arbitrary")),
        name="hgrn_prompt",
    )(zhg, lb, norm_g, lvl, tri)


def _hgrn_step_kernel(z_ref, s_ref, lb_ref, ng_ref, o_ref, so_ref, *, G):
    ng = ng_ref[...]
    for h in range(HEADS):
        cs = slice(h * HEAD_W, (h + 1) * HEAD_W)
        zq = z_ref[:, h * HEAD_W:(h + 1) * HEAD_W]
        zf = z_ref[:, GROUP_W + h * HEAD_W:GROUP_W + (h + 1) * HEAD_W]
        v = z_ref[:, 2 * GROUP_W + h * HEAD_W:2 * GROUP_W + (h + 1) * HEAD_W]
        zg = z_ref[:, 3 * GROUP_W + h * HEAD_W:3 * GROUP_W + (h + 1) * HEAD_W]
        lb = lb_ref[:, cs]
        q = zq * _sigmoid(zq)
        f = lb + (1.0 - lb) * _sigmoid(zf)
        k = (1.0 - lb) * _sigmoid(-zf)
        qT, fT, kT = q.T, f.T, k.T
        rows = []
        for j in range(G):
            s_new = fT[:, j:j + 1] * s_ref[j, h] + kT[:, j:j + 1] * v[j:j + 1, :]
            so_ref[j, h] = s_new
            rows.append(jnp.sum(s_new * qT[:, j:j + 1], axis=0, keepdims=True))
        o = jnp.concatenate(rows, axis=0)
        ms = jnp.mean(o * o, -1, keepdims=True)
        o_ref[:, cs] = (o * lax.rsqrt(ms + RMS_EPS) * ng * (zg * _sigmoid(zg))).astype(BF16)


def _hgrn_sample(zhg, state, lb, norm_g):
    N = zhg.shape[0]
    G = 8
    full = lambda a: pl.BlockSpec(a.shape, lambda i: (0,) * a.ndim)
    st_spec = pl.BlockSpec((G, HEADS, HEAD_W, HEAD_W), lambda i: (i, 0, 0, 0))
    return pl.pallas_call(
        functools.partial(_hgrn_step_kernel, G=G),
        grid=(N // G,),
        in_specs=[pl.BlockSpec((G, HG_COLS), lambda i: (i, 0)), st_spec, full(lb), full(norm_g)],
        out_specs=[pl.BlockSpec((G, GROUP_W), lambda i: (i, 0)), st_spec],
        out_shape=[jax.ShapeDtypeStruct((N, GROUP_W), BF16), jax.ShapeDtypeStruct(state.shape, F32)],
        compiler_params=_cparams(("arbitrary",)),
        name="hgrn_sample",
    )(zhg, state, lb, norm_g)


def _attn_kernel(qt_ref, k_ref, vt_ref, kpm_ref, sl_ref, feat_ref, lam_ref, gcol_ref, o_ref, m_ref, l_ref, acc_ref,
                 *, blk):
    qi = pl.program_id(2)
    row = lax.broadcasted_iota(I32, (HEAD_W, blk), 0)
    qt = qt_ref[...].astype(F32)
    a_hi = sl_ref[0:1, :]
    a_lo = sl_ref[1:2, :]
    slope2 = sl_ref[2:3, :]
    def slope_rows(r0):
        in_rows = (row >= r0) & (row < r0 + 4)
        return jnp.where(in_rows, jnp.where((row - r0) % 2 == 0, a_hi, a_lo), 0.0)

    q_aug = [jnp.where(row < DA_HEAD, qt, slope_rows(DA_HEAD)).astype(BF16),
             jnp.where(row >= DA_HEAD, qt, slope_rows(0)).astype(BF16)]
    keep = [feat_ref[0] > 0, feat_ref[1] > 0]
    feats = [feat_ref[2], feat_ref[3]]
    m_ref[...] = jnp.full_like(m_ref, -jnp.inf)
    l_ref[...] = jnp.zeros_like(l_ref)
    acc_ref[...] = jnp.zeros_like(acc_ref)

    def block(kj, masked):
        k_start = pl.multiple_of(kj * blk, blk)
        kb = k_ref[pl.ds(k_start, blk), :]
        vt = vt_ref[kj]
        off = slope2 * jnp.full((1, blk), k_start - qi * blk, I32).astype(F32)
        if masked:
            ok = lax.broadcasted_iota(I32, (blk, 1), 0) <= lax.broadcasted_iota(I32, (1, blk), 1)
        for mi in range(2):
            st = _dot(jnp.where(keep[mi], kb, feats[mi]), q_aug[mi])
            if masked:
                st = jnp.where(ok, st, -jnp.inf)
            m_prev = m_ref[mi]
            m_new = jnp.maximum(m_prev, jnp.max(st, axis=0, keepdims=True) + off)
            p = jnp.exp2(st - (m_new - off))
            alpha = jnp.exp2(m_prev - m_new)
            l_ref[mi] = alpha * l_ref[mi] + jnp.sum(p, axis=0, keepdims=True)
            acc_ref[mi] = alpha * acc_ref[mi] + _dot(vt, p.astype(BF16))
            m_ref[mi] = m_new

    block(qi, True)

    nblk = kpm_ref.shape[1]
    jlane = lax.broadcasted_iota(I32, (1, nblk), 1)
    far = slope2[:, 0:1] * ((jlane - qi) * blk + (blk - 1)).astype(F32)
    need = jlane < 0
    for mi in range(2):
        qm = qt[mi * DA_HEAD:(mi + 1) * DA_HEAD, :]
        qn = jnp.sqrt(jnp.max(jnp.sum(qm * qm, axis=0, keepdims=True), axis=1, keepdims=True))
        m_lo = jnp.min(m_ref[mi], axis=1, keepdims=True)
        need = need | (1.02 * qn * kpm_ref[mi:mi + 1, :] + 1.0 + far - m_lo >= -SKIP_LOG2)
    n_visit = jnp.sum((need & (jlane < qi)).astype(I32))

    def body(kj, c):
        block(kj, False)
        return c

    lax.fori_loop(qi - n_visit, qi, body, 0)

    ot = acc_ref[0] * (1.0 / l_ref[0]) - lam_ref[...] * (acc_ref[1] * (1.0 / l_ref[1]))
    ms = jnp.mean(ot * ot, axis=0, keepdims=True)
    o_ref[...] = (ot * lax.rsqrt(ms + RMS_EPS) * gcol_ref[...]).T.astype(BF16)


def _attn_prompt(qt, k, vt, kn2, lam, subln_g):
    B, nblk, _, blk = qt.shape
    T = k.shape[1]
    bf = lambda x: np.asarray(x, np.float32).astype(BF16).astype(np.float64)
    a = np.asarray(ALIBI_SLOPES, np.float64) * LOG2E
    a_hi = bf(a)
    a_lo = bf(a - a_hi)
    sl = np.zeros((HEADS, 8, blk), np.float32)
    sl[:, 0, :], sl[:, 1, :], sl[:, 2, :] = a_hi[:, None], a_lo[:, None], a.astype(np.float32)[:, None]
    lane = np.arange(HEAD_W)[None, :]
    r = np.arange(blk)[:, None]
    assert blk <= 256 * 256

    def pos_feat(l0):
        return (np.where((lane == l0) | (lane == l0 + 1), r // 256 * 256, 0)
                + np.where((lane == l0 + 2) | (lane == l0 + 3), r % 256, 0)).astype(np.float32)

    feat = np.stack([np.broadcast_to(lane < DA_HEAD, (blk, HEAD_W)).astype(np.float32),
                     np.broadcast_to(lane >= DA_HEAD, (blk, HEAD_W)).astype(np.float32),
                     pos_feat(DA_HEAD), pos_feat(0)])
    feat = jnp.asarray(feat, BF16)
    lam_row = jnp.full((1, blk), lam, F32)
    gcol = jnp.broadcast_to((subln_g.reshape(HEAD_W, 1) * (1.0 - LAM_INIT)), (HEAD_W, blk))
    kn = jnp.sqrt(1.02 * jnp.max(kn2[:, :, 0, :HEADS].reshape(B, nblk, -1, HEADS), axis=2))
    upto = np.arange(nblk)[:, None] >= np.arange(nblk)[None, :]
    kpm = jnp.max(jnp.where(upto[None, :, :, None], kn[:, None, :, :], 0.0), axis=2)
    kpm = jnp.broadcast_to(kpm.transpose(0, 2, 1)[:, :, None, :], (B, HEADS, 2, nblk))
    full = lambda x: pl.BlockSpec(x.shape, lambda b, h, i: (0,) * x.ndim)
    return pl.pallas_call(
        functools.partial(_attn_kernel, blk=blk),
        grid=(B, HEADS, nblk),
        in_specs=[pl.BlockSpec((None, None, HEAD_W, blk), lambda b, h, i: (b, i, h, 0)),
                  pl.BlockSpec((None, T, HEAD_W), lambda b, h, i: (b, 0, h)),
                  pl.BlockSpec((None, nblk, HEAD_W, blk), lambda b, h, i: (b, 0, h, 0)),
                  pl.BlockSpec((None, None, 2, nblk), lambda b, h, i: (b, h, 0, 0)),
                  pl.BlockSpec((None, 8, blk), lambda b, h, i: (h, 0, 0)),
                  full(feat), full(lam_row), full(gcol)],
        out_specs=pl.BlockSpec((None, blk, HEAD_W), lambda b, h, i: (b, i, h)),
        out_shape=jax.ShapeDtypeStruct((B, T, GROUP_W), BF16),
        scratch_shapes=[pltpu.VMEM((2, 1, blk), F32), pltpu.VMEM((2, 1, blk), F32), pltpu.VMEM((2, HEAD_W, blk), F32)],
        compiler_params=_cparams(("arbitrary", "arbitrary", "arbitrary")),
        name="attn_prompt",
    )(qt, k, vt, kpm, jnp.asarray(sl), feat, lam_row, gcol)


def _attn_decode_kernel(pt_ref, q_ref, kn_ref, vn_ref, bias_ref, lam_ref, g_ref, ck_hbm, cv_hbm, o_ref,
                        kbuf, vbuf, sems, *, n_pages):
    i = pl.program_id(0)
    slot = i % 2

    def page_copies(seq, s):
        for pg in range(n_pages):
            page = pt_ref[seq, pg]
            for h in range(HEADS):
                dst = (s, h, pl.ds(pg * PAGE_SIZE, PAGE_SIZE))
                yield pltpu.make_async_copy(ck_hbm.at[0, page, :, h, :], kbuf.at[dst], sems.at[0, s])
                yield pltpu.make_async_copy(cv_hbm.at[0, page, :, h, :], vbuf.at[dst], sems.at[1, s])

    @pl.when(i == 0)
    def _():
        for cp in page_copies(0, 0):
            cp.start()

    @pl.when(i + 1 < pl.num_programs(0))
    def _():
        for cp in page_copies(i + 1, 1 - slot):
            cp.start()

    for cp in page_copies(i, slot):
        cp.wait()

    r8 = lax.broadcasted_iota(I32, (8, GROUP_W), 0)
    c8 = lax.broadcasted_iota(I32, (8, GROUP_W), 1)
    qmat = jnp.where(c8 // DA_HEAD == r8, jnp.broadcast_to(q_ref[...].astype(F32), (8, GROUP_W)), 0.0)
    qmat_bf = qmat.astype(BF16)
    s = bias_ref[...]
    for h in range(HEADS):
        s = s + _dot_nt(qmat_bf[:, h * HEAD_W:(h + 1) * HEAD_W], kbuf[slot, h].astype(BF16))
    kn = kn_ref[...].astype(BF16).astype(F32)
    s_self = jnp.sum(qmat_bf.astype(F32) * kn, axis=-1, keepdims=True)
    m = jnp.maximum(jnp.max(s, -1, keepdims=True), s_self)
    p = jnp.exp(s - m)
    p_self = jnp.exp(s_self - m)
    inv_l = 1.0 / (jnp.sum(p, -1, keepdims=True) + p_self)
    coef = jnp.where(lax.broadcasted_iota(I32, (8, 1), 0) % 2 == 0, 1.0, -lam_ref[:, 0:1]) * inv_l
    w = (p * coef).astype(BF16)
    accs = [_dot(w, vbuf[slot, h].astype(BF16)) for h in range(HEADS)]
    acc = jnp.concatenate(accs, axis=1) + (p_self * coef) * vn_ref[...]
    o = jnp.sum(jnp.where(c8 // HEAD_W == r8 // 2, acc, 0.0), axis=0, keepdims=True)
    outs = []
    for h in range(HEADS):
        oh = o[:, h * HEAD_W:(h + 1) * HEAD_W]
        ms = jnp.mean(oh * oh, -1, keepdims=True)
        outs.append(oh * lax.rsqrt(ms + RMS_EPS) * g_ref[...] * (1.0 - LAM_INIT))
    o_ref[...] = jnp.concatenate(outs, axis=1).astype(BF16)


def _attn_sample(q, k_new, v_new, cache_k, cache_v, page_table, lam_row, subln_g):
    N, n_pages = page_table.shape
    past = n_pages * PAGE_SIZE
    kpos = np.arange(past, dtype=np.float32)[None, :]
    slope_rows = np.repeat(np.asarray(ALIBI_SLOPES, np.float32), 2)[:, None]
    bias = jnp.asarray(-slope_rows * (past - kpos))
    row = pl.BlockSpec((None, 1, GROUP_W), lambda i, pt: (i, 0, 0))
    full = lambda a: pl.BlockSpec(a.shape, lambda i, pt: (0,) * a.ndim)
    anyspec = pl.BlockSpec(memory_space=pl.ANY)
    grid_spec = pltpu.PrefetchScalarGridSpec(
        num_scalar_prefetch=1,
        grid=(N,),
        in_specs=[row, row, row, full(bias), full(lam_row), full(subln_g), anyspec, anyspec],
        out_specs=row,
        scratch_shapes=[pltpu.VMEM((2, HEADS, past, HEAD_W), F32), pltpu.VMEM((2, HEADS, past, HEAD_W), F32),
                        pltpu.SemaphoreType.DMA((2, 2))],
    )
    r3 = lambda a: a.reshape(N, 1, GROUP_W)
    out = pl.pallas_call(
        functools.partial(_attn_decode_kernel, n_pages=n_pages),
        grid_spec=grid_spec,
        out_shape=jax.ShapeDtypeStruct((N, 1, GROUP_W), BF16),
        compiler_params=_cparams(("arbitrary",)),
        name="attn_sample",
    )(page_table, r3(q), r3(k_new), r3(v_new), bias, lam_row, subln_g, cache_k, cache_v)
    return out.reshape(N, GROUP_W)


def _outproj_kernel(hg_ref, da_ref, x_ref, lg_ref, lb_ref, g1_ref, sc2_ref, sh2_ref, w_ref, l1g_ref, l1b_ref,
                    wrh_ref, wrl_ref, br_ref, u_ref, cin_ref,
                    x1_ref, h2_hbm, idx_ref, tw_ref, rank_ref, cout_ref, run_ref, hbuf, hsem, *, n_steps):
    step = pl.program_id(0) * pl.num_programs(1) + pl.program_id(1)
    slot = step % 2
    bm = x_ref.shape[0]
    h2_copies = lambda st, s: [_rows_copy(hbuf.at[s], h2_hbm, st * bm, bm, hsem.at[s], True)]

    @pl.when(step == 0)
    def _():
        run_ref[...] = cin_ref[...]

    x0 = _layer_norm(x_ref[...], lg_ref[...], lb_ref[...])
    mix = _dot(hg_ref[...], w_ref[0:GROUP_W, :]) + _dot(da_ref[...], w_ref[GROUP_W:2 * GROUP_W, :])
    x1 = _layer_norm(DEEPNORM_ALPHA * x0 + (1.0 + g1_ref[...]) * mix, l1g_ref[...], l1b_ref[...])
    x1_ref[...] = x1
    h2 = x1 * (1.0 + sc2_ref[...]) + sh2_ref[...]

    @pl.when(step >= 2)
    def _():
        for cp in h2_copies(step - 2, slot):
            cp.wait()

    hbuf[slot] = h2
    for cp in h2_copies(step, slot):
        cp.start()

    @pl.when(step == n_steps - 1)
    def _():
        tail = h2_copies(step, slot)
        if n_steps > 1:
            tail = h2_copies(step - 1, 1 - slot) + tail
        for cp in tail:
            cp.wait()

    hi = h2.astype(BF16)
    lo = (h2 - hi.astype(F32)).astype(BF16)
    wrh = wrh_ref[...]
    logits = _dot_nt(wrh, hi) + _dot_nt(wrh, lo) + _dot_nt(wrl_ref[...], hi) + br_ref[...]

    n_e, bm = logits.shape
    rows = lax.broadcasted_iota(I32, (n_e, bm), 0).astype(F32)
    vals, sels = [], []
    work = logits
    for kk in range(TOP_K):
        mx = jnp.max(work, axis=0, keepdims=True)
        ix = jnp.min(jnp.where(work == mx, rows, float(n_e)), axis=0, keepdims=True)
        sel = rows == ix
        idx_ref[kk:kk + 1, :] = ix.astype(I32)
        vals.append(mx)
        sels.append(sel)
        work = jnp.where(sel, -jnp.inf, work)
    es = [jnp.exp(vv - vals[0]) for vv in vals]
    inv = 1.0 / (es[0] + es[1] + es[2] + es[3])
    for kk in range(TOP_K):
        tw_ref[kk:kk + 1, :] = es[kk] * inv

    base = run_ref[...]
    for kk in range(TOP_K):
        oh = jnp.where(sels[kk], 1.0, 0.0)
        before = base + _dot(oh.astype(BF16), u_ref[...])
        rank_ref[kk:kk + 1, :] = jnp.sum(jnp.where(sels[kk], before, 0.0), axis=0, keepdims=True).astype(I32)
        base = base + jnp.sum(oh, axis=1, keepdims=True)
    run_ref[...] = base
    cout_ref[...] = base


def _outproj(hg, da, x, mods, consts, counts_in, bm):
    G, R, D = x.shape
    g1, sc2, sh2 = mods
    ln_g, ln_b, w_out_bf, l1g, l1b, wrh, wrl, br = consts
    nb = R // bm
    n_tok = G * R
    u = jnp.asarray(np.triu(np.ones((bm, bm), np.float32), 1), BF16)
    row = lambda w: pl.BlockSpec((None, bm, w), lambda g, i: (g, i, 0))
    full = lambda a: pl.BlockSpec(a.shape, lambda g, i: (0,) * a.ndim)
    tok_lanes = pl.BlockSpec((TOP_K, bm), lambda g, i: (0, g * nb + i))
    return pl.pallas_call(
        functools.partial(_outproj_kernel, n_steps=G * nb),
        grid=(G, nb),
        in_specs=[row(GROUP_W), row(GROUP_W), row(D), full(ln_g), full(ln_b),
                  _mod_spec(g1, bm), _mod_spec(sc2, bm), _mod_spec(sh2, bm),
                  full(w_out_bf), full(l1g), full(l1b), full(wrh), full(wrl), full(br), full(u), full(counts_in)],
        out_specs=[row(D), pl.BlockSpec(memory_space=pl.ANY), tok_lanes, tok_lanes, tok_lanes, full(counts_in)],
        out_shape=[jax.ShapeDtypeStruct((G, R, D), F32),
                   jax.ShapeDtypeStruct((n_tok,) + ROW_SHAPE, F32),
                   jax.ShapeDtypeStruct((TOP_K, n_tok), I32),
                   jax.ShapeDtypeStruct((TOP_K, n_tok), F32),
                   jax.ShapeDtypeStruct((TOP_K, n_tok), I32),
                   jax.ShapeDtypeStruct(counts_in.shape, F32)],
        scratch_shapes=[pltpu.VMEM(counts_in.shape, F32), pltpu.VMEM((2, bm, D), F32), pltpu.SemaphoreType.DMA((2,))],
        compiler_params=_cparams(("arbitrary", "arbitrary")),
        name="outproj",
    )(hg, da, x, ln_g, ln_b, g1, sc2, sh2, w_out_bf, l1g, l1b, wrh, wrl, br, u, counts_in)


def _dispatch_kernel(pe_ref, nu_ref, dest_ref, h2p_ref, h2s_ref, xb_hbm, zbuf, zsem, sem, *, bt, n_prompt_steps, nb):
    i = pl.program_id(0)
    bm = MOE_BLOCK

    @pl.when(i == 0)
    def _():
        zbuf[...] = jnp.zeros_like(zbuf)
        zero_block = lambda row0: pltpu.make_async_copy(zbuf, xb_hbm.at[pl.ds(row0, bm)], zsem)
        has_rows = [pe_ref[e] > (pe_ref[e - 1] if e else 0) for e in range(N_EXPERTS)]
        for e in range(N_EXPERTS):
            @pl.when(has_rows[e])
            def _():
                zero_block(pe_ref[e] - bm).start()

        def tail_start(j, c):
            zero_block(j * bm).start()
            return c

        def tail_wait(j, c):
            zero_block(j * bm).wait()
            return c

        lax.fori_loop(nu_ref[0], nb, tail_start, 0)
        for e in range(N_EXPERTS):
            @pl.when(has_rows[e])
            def _():
                zero_block(pe_ref[e] - bm).wait()
        lax.fori_loop(nu_ref[0], nb, tail_wait, 0)

    def scatter(h2_ref):
        def body(t, c):
            for kk in range(TOP_K):
                pltpu.make_async_copy(h2_ref.at[t], xb_hbm.at[dest_ref[0, t * TOP_K + kk]], sem).start(priority=kk % 2)
            return c

        lax.fori_loop(0, bt, body, 0, unroll=4)
        for kk in range(TOP_K):
            pltpu.make_async_copy(h2_ref, xb_hbm.at[pl.ds(0, bt)], sem).wait()

    @pl.when(i < n_prompt_steps)
    def _():
        scatter(h2p_ref)

    @pl.when(i >= n_prompt_steps)
    def _():
        scatter(h2s_ref)


def _dispatch(pad_end, n_used, dest, h2_p, h2_s, n_rows):
    bt = TOK_BLOCK
    n_p, n_s = h2_p.shape[0] // bt, h2_s.shape[0] // bt
    nb = n_rows // MOE_BLOCK
    grid_spec = pltpu.PrefetchScalarGridSpec(
        num_scalar_prefetch=2,
        grid=(n_p + n_s,),
        in_specs=[pl.BlockSpec((None, 1, TOP_K * bt), lambda i, pe, nu: (i, 0, 0), memory_space=pltpu.SMEM),
                  pl.BlockSpec((bt,) + ROW_SHAPE, lambda i, pe, nu: (jnp.minimum(i, n_p - 1), 0, 0)),
                  pl.BlockSpec((bt,) + ROW_SHAPE, lambda i, pe, nu: (jnp.maximum(i - n_p, 0), 0, 0))],
        out_specs=pl.BlockSpec(memory_space=pl.ANY),
        scratch_shapes=[pltpu.VMEM((MOE_BLOCK,) + ROW_SHAPE, F32), pltpu.SemaphoreType.DMA(()),
                        pltpu.SemaphoreType.DMA(())],
    )
    return pl.pallas_call(
        functools.partial(_dispatch_kernel, bt=bt, n_prompt_steps=n_p, nb=nb),
        grid_spec=grid_spec,
        out_shape=jax.ShapeDtypeStruct((n_rows,) + ROW_SHAPE, F32),
        compiler_params=_cparams(("arbitrary",)),
        name="dispatch",
    )(pad_end, n_used, dest, h2_p, h2_s)


def _experts_kernel(be_ref, nu_ref, nxt_ref, par_ref, xb_hbm, wu_hbm, bu_ref, wd_hbm, bd_ref, yb_hbm,
                    wu_bf, wd_bf, wu_f32, wd_f32, xbuf, ybuf, sems, wsems, *, bm):
    j = pl.program_id(0)
    nb = pl.num_programs(0)
    slot = j % 2
    nu = nu_ref[0]
    active = j < nu
    x_copies = lambda blk, s: [_rows_copy(xbuf.at[s], xb_hbm, blk * bm, bm, sems.at[0, s], False)]
    y_copies = lambda blk, s: [_rows_copy(ybuf.at[s], yb_hbm, blk * bm, bm, sems.at[1, s], True)]
    w_copies = lambda e, s: [pltpu.make_async_copy(wu_hbm.at[e], wu_f32.at[s], wsems.at[0, s]),
                             pltpu.make_async_copy(wd_hbm.at[e], wd_f32.at[s], wsems.at[1, s])]

    @pl.when(j == 0)
    def _():
        for cp in x_copies(0, 0) + w_copies(be_ref[0], par_ref[0]):
            cp.start()

    @pl.when(j + 1 < nu)
    def _():
        for cp in x_copies(j + 1, 1 - slot):
            cp.start()

    @pl.when(j >= 2)
    def _():
        for cp in y_copies(j - 2, slot):
            cp.wait()

    @pl.when(active & ((j == 0) | (be_ref[j] != be_ref[jnp.maximum(j - 1, 0)])))
    def _():
        half = par_ref[j]
        for cp in w_copies(be_ref[j], half):
            cp.wait()
        wu_bf[...] = wu_f32[half].astype(BF16)
        wd_bf[...] = wd_f32[half].astype(BF16)

        @pl.when(nxt_ref[j] >= 0)
        def _():
            for cp in w_copies(nxt_ref[j], 1 - half):
                cp.start()

    @pl.when(active)
    def _():
        for cp in x_copies(j, slot):
            cp.wait()
        u = _dot(xbuf[slot].astype(BF16), wu_bf[...]) + bu_ref[...]
        glu = jnp.minimum(u[:, :D_FF], SWIGLU_LIMIT)
        lin = jnp.clip(u[:, D_FF:], -SWIGLU_LIMIT, SWIGLU_LIMIT)
        act = glu * _sigmoid(SWIGLU_ALPHA * glu) * (lin + 1.0)
        ybuf[slot] = _dot(act.astype(BF16), wd_bf[...]) + bd_ref[...]

    @pl.when(jnp.logical_not(active))
    def _():
        ybuf[slot] = jnp.zeros((bm, D_MODEL), F32)

    for cp in y_copies(j, slot):
        cp.start()

    @pl.when(j == nb - 1)
    def _():
        for cp in y_copies(j - 1, 1 - slot) + y_copies(j, slot):
            cp.wait()


def _experts(blk_exp, n_used, blk_next, blk_half, xb, w_up, b_up, w_down, b_down):
    n_rows = xb.shape[0]
    bm = MOE_BLOCK
    nb = n_rows // bm
    anyspec = pl.BlockSpec(memory_space=pl.ANY)
    grid_spec = pltpu.PrefetchScalarGridSpec(
        num_scalar_prefetch=4,
        grid=(nb,),
        in_specs=[anyspec, anyspec,
                  pl.BlockSpec((None, 1, 2 * D_FF), lambda j, be, nu, nx, hf: (be[j], 0, 0)),
                  anyspec,
                  pl.BlockSpec((None, 1, D_MODEL), lambda j, be, nu, nx, hf: (be[j], 0, 0))],
        out_specs=anyspec,
        scratch_shapes=[pltpu.VMEM((D_MODEL, 2 * D_FF), BF16), pltpu.VMEM((D_FF, D_MODEL), BF16),
                        pltpu.VMEM((2, D_MODEL, 2 * D_FF), F32), pltpu.VMEM((2, D_FF, D_MODEL), F32),
                        pltpu.VMEM((2, bm, D_MODEL), F32), pltpu.VMEM((2, bm, D_MODEL), F32),
                        pltpu.SemaphoreType.DMA((2, 2)), pltpu.SemaphoreType.DMA((2, 2))],
    )
    return pl.pallas_call(
        functools.partial(_experts_kernel, bm=bm),
        grid_spec=grid_spec,
        out_shape=jax.ShapeDtypeStruct((n_rows,) + ROW_SHAPE, F32),
        compiler_params=_cparams(("arbitrary",)),
        name="experts",
    )(blk_exp, n_used, blk_next, blk_half, xb, w_up, b_up.reshape(N_EXPERTS, 1, 2 * D_FF), w_down,
      b_down.reshape(N_EXPERTS, 1, D_MODEL))


def _combine_kernel(dcur_ref, dnext_ref, x1_ref, tw_ref, g2_ref, lg_ref, lb_ref, yb_hbm, o_ref, buf, sems, *, bm):
    i = pl.program_id(0) * pl.num_programs(1) + pl.program_id(1)
    n = pl.num_programs(0) * pl.num_programs(1)
    slot = i % 2
    n_groups = bm // SUBLANES

    def issue(dref, s):
        def body(a, c):
            for b in range(SUBLANES):
                for kk in range(TOP_K):
                    src = yb_hbm.at[dref[0, (a * SUBLANES + b) * TOP_K + kk]]
                    pltpu.make_async_copy(src, buf.at[s, kk, a, pl.ds(b, 1), :], sems.at[s]).start(priority=kk % 2)
            return c
        lax.fori_loop(0, n_groups, body, 0)

    @pl.when(i == 0)
    def _():
        issue(dcur_ref, 0)

    @pl.when(i + 1 < n)
    def _():
        issue(dnext_ref, 1 - slot)

    for kk in range(TOP_K):
        for a in range(n_groups):
            pltpu.make_async_copy(yb_hbm.at[pl.ds(0, SUBLANES), 0, :], buf.at[slot, kk, a], sems.at[slot]).wait()

    tw = tw_ref[...]
    rows_of = lambda kk: buf[slot, kk].reshape(bm, D_MODEL)
    ff = tw[:, 0:1] * rows_of(0)
    for kk in range(1, TOP_K):
        ff = ff + tw[:, kk:kk + 1] * rows_of(kk)
    o_ref[...] = _layer_norm(DEEPNORM_ALPHA * x1_ref[...] + (1.0 + g2_ref[...]) * ff, lg_ref[...], lb_ref[...])


def _combine(dest, tw_rows, x1, g2, ln_g, ln_b, yb, bm):
    G, R, D = x1.shape
    nb = R // bm
    n_blk = G * nb
    row = lambda w: pl.BlockSpec((None, bm, w), lambda g, i: (g, i, 0))
    full = lambda a: pl.BlockSpec(a.shape, lambda g, i: (0,) * a.ndim)
    cur = pl.BlockSpec((None, 1, TOP_K * bm), lambda g, i: (g * nb + i, 0, 0), memory_space=pltpu.SMEM)
    nxt = pl.BlockSpec((None, 1, TOP_K * bm), lambda g, i: (jnp.minimum(g * nb + i + 1, n_blk - 1), 0, 0),
                       memory_space=pltpu.SMEM)
    return pl.pallas_call(
        functools.partial(_combine_kernel, bm=bm),
        grid=(G, nb),
        in_specs=[cur, nxt, row(D), pl.BlockSpec((bm, TOP_K), lambda g, i: (g * nb + i, 0)),
                  _mod_spec(g2, bm), full(ln_g), full(ln_b), pl.BlockSpec(memory_space=pl.ANY)],
        out_specs=row(D),
        out_shape=jax.ShapeDtypeStruct((G, R, D), F32),
        scratch_shapes=[pltpu.VMEM((2, TOP_K, bm // SUBLANES, SUBLANES, D), F32), pltpu.SemaphoreType.DMA((2,))],
        compiler_params=_cparams(("arbitrary", "arbitrary")),
        name="combine",
    )(dest, dest, x1, tw_rows, g2, ln_g, ln_b, yb)


def kernel(x_prompt, x_sample, c_prompt, c_sample, cache_k, cache_v, state_hgrn, page_table, ln_in_g, ln_in_b, w_ada, b_ada, w_in, hg_lb, hg_norm_g, da_lq1, da_lk1, da_lq2, da_lk2, da_subln_g, w_out, ln1_g, ln1_b, w_router, b_router, w_up, b_up, w_down, b_down, ln2_g, ln2_b):
    assert w_in.shape[0] == 1, "single-layer trunk"
    B, T, D = x_prompt.shape
    NS = x_sample.shape[0]
    n_prompt = B * T
    n_tot = n_prompt + NS
    r1 = lambda a: a.reshape(1, -1)

    lb = r1(jax.nn.softmax(hg_lb.astype(F32), axis=0)[0])
    lam = (jnp.exp(jnp.sum(da_lq1[0].astype(F32) * da_lk1[0].astype(F32)))
           - jnp.exp(jnp.sum(da_lq2[0].astype(F32) * da_lk2[0].astype(F32))) + LAM_INIT)
    lam_row = jnp.full((1, HEAD_W), lam, F32)
    ln_g, ln_b = r1(ln_in_g), r1(ln_in_b)
    w_in_bf = w_in[0].astype(BF16)
    wqt = w_in_bf[:, HG_COLS:HG_COLS + GROUP_W].T
    w_out_bf = w_out[0].astype(BF16)
    wr_t = w_router[0].T
    wrh = wr_t.astype(BF16)
    wrl = (wr_t - wrh.astype(F32)).astype(BF16)
    br = b_router[0].reshape(N_EXPERTS, 1)
    norm_g, subln_g = r1(hg_norm_g[0]), r1(da_subln_g[0])

    n_c = B + NS
    c_all = jnp.concatenate([c_prompt, c_sample, jnp.zeros((-n_c % 8, D), F32)], axis=0)
    mod = _ada(c_all, w_ada[0], b_ada[0])
    mod_p = [mod[:B, j * D:(j + 1) * D].reshape(B, 1, D) for j in range(6)]
    mod_s = [mod[B:n_c, j * D:(j + 1) * D].reshape(1, NS, D) for j in range(6)]

    zhg_p, kf_p, vf_p, kb_p, qt_p, vt_p, kn2_p = _inproj(
        x_prompt, mod_p[1], mod_p[0], ln_g, ln_b, w_in_bf, INPROJ_BLOCK, (wqt, ATTN_BLOCK))
    hg_p, s_p = _hgrn_prompt(zhg_p, lb, norm_g)
    da_p = _attn_prompt(qt_p, kb_p, vt_p, kn2_p, lam, subln_g)

    xs = x_sample.reshape(1, NS, D)
    zhg_s, kf_s, vf_s, q_s = _inproj(xs, mod_s[1], mod_s[0], ln_g, ln_b, w_in_bf, NS)
    hg_s, s_s = _hgrn_sample(zhg_s[0], state_hgrn[0], lb, norm_g)
    da_s = _attn_sample(q_s[0], kf_s.reshape(NS, GROUP_W), vf_s.reshape(NS, GROUP_W), cache_k, cache_v, page_table,
                        lam_row, subln_g)

    consts = (ln_g, ln_b, w_out_bf, r1(ln1_g[0]), r1(ln1_b[0]), wrh, wrl, br)
    counts0 = jnp.zeros((N_EXPERTS, 1), F32)
    x1_p, h2_p, idx_p, tw_p, rank_p, counts1 = _outproj(
        hg_p, da_p, x_prompt, (mod_p[2], mod_p[4], mod_p[3]), consts, counts0, ROW_BLOCK)
    x1_s, h2_s, idx_s, tw_s, rank_s, counts = _outproj(
        hg_s.reshape(1, NS, GROUP_W), da_s.reshape(1, NS, GROUP_W), xs, (mod_s[2], mod_s[4], mod_s[3]), consts,
        counts1, NS)

    cnt = counts[:, 0].astype(I32)
    padded = (cnt + MOE_BLOCK - 1) // MOE_BLOCK * MOE_BLOCK
    pad_end = jnp.cumsum(padded)
    pad_start = pad_end - padded
    e_ids = jnp.arange(N_EXPERTS, dtype=I32)

    def slot_of(idx, rank):
        return jnp.sum(jnp.where(idx[..., None] == e_ids, pad_start, 0), axis=-1) + rank

    dest_p = slot_of(idx_p, rank_p)
    dest_s = slot_of(idx_s, rank_s)
    n_blocks = -(-n_tot * TOP_K // MOE_BLOCK) + N_EXPERTS
    blk_row0 = jnp.arange(n_blocks, dtype=I32) * MOE_BLOCK
    blk_exp = jnp.minimum(jnp.sum((pad_end[None, :] <= blk_row0[:, None]).astype(I32), axis=1), N_EXPERTS - 1)
    n_used = (pad_end[-1:] // MOE_BLOCK).astype(I32)

    by_block = lambda d: d.T.reshape(-1, 1, TOK_BLOCK * TOP_K)
    slots_p, slots_s = by_block(dest_p), by_block(dest_s)
    xb = _dispatch(pad_end.astype(I32), n_used, jnp.concatenate([slots_p, slots_s], axis=0), h2_p, h2_s,
                   n_blocks * MOE_BLOCK)
    has_rows = padded > 0
    later = (e_ids[None, :] > e_ids[:, None]) & has_rows[None, :]
    next_e = jnp.where(jnp.any(later, axis=1), jnp.min(jnp.where(later, e_ids[None, :], N_EXPERTS), axis=1), -1)
    half_e = (jnp.cumsum(has_rows.astype(I32)) - 1) % 2
    per_block = lambda tab: jnp.sum(jnp.where(blk_exp[:, None] == e_ids, tab, 0), axis=1).astype(I32)
    yb = _experts(blk_exp, n_used, per_block(next_e), per_block(half_e), xb, w_up[0], b_up[0], w_down[0], b_down[0])

    l2g, l2b = r1(ln2_g[0]), r1(ln2_b[0])
    y_p = _combine(slots_p, tw_p.T, x1_p, mod_p[5], l2g, l2b, yb, TOK_BLOCK)
    y_s = _combine(slots_s, tw_s.T, x1_s, mod_s[5], l2g, l2b, yb, TOK_BLOCK)

    smp = lambda a: a.reshape(1, NS, 1, HEADS, HEAD_W)
    return (y_p, y_s.reshape(NS, 1, D), kf_p[None], vf_p[None], s_p[None], smp(kf_s), smp(vf_s), s_s[None])
```

```python
import functools
import math

import numpy as np
import jax
import jax.numpy as jnp
from jax import lax
from jax.experimental import pallas as pl
from jax.experimental.pallas import tpu as pltpu

F32, BF16, I32 = jnp.float32, jnp.bfloat16, jnp.int32

D_MODEL = 1024
HEADS = 4
HEAD_W = 128
DA_HEAD = 64
GROUP_W = HEADS * HEAD_W
HG_COLS = 4 * GROUP_W
N_EXPERTS = 32
TOP_K = 4
D_FF = 1024
SWIGLU_ALPHA = 1.702
SWIGLU_LIMIT = 7.0
DEEPNORM_ALPHA = 2.0 ** 0.25
LN_EPS = 1e-5
RMS_EPS = 1e-6
LAM_INIT = 0.8 - 0.6 * math.exp(-0.3 * 0)
LOG2E = math.log2(math.e)
ALIBI_SLOPES = tuple(2.0 ** (-8.0 * (h + 1) / HEADS) for h in range(HEADS))
PAGE_SIZE = 128
EXP_CLAMP = 80.0
SKIP_LOG2 = 151.0

VMEM_LIMIT = 56 * 1024 * 1024
SUBLANES = 8
ADA_COLS = 1536
MOE_BLOCK = 256
HG_CHUNK = 256
ATTN_BLOCK = 1024
INPROJ_BLOCK = 512
ROW_BLOCK = 512
TOK_BLOCK = 128


def _cparams(sem):
    return pltpu.CompilerParams(dimension_semantics=sem, vmem_limit_bytes=VMEM_LIMIT)


def _sigmoid(x):
    return 1.0 / (1.0 + jnp.exp(-x))


def _layer_norm(x, g, b):
    mu = jnp.mean(x, -1, keepdims=True)
    xc = x - mu
    var = jnp.mean(xc * xc, -1, keepdims=True)
    return xc * lax.rsqrt(var + LN_EPS) * g + b


def _dot(a, b):
    return jnp.dot(a, b, preferred_element_type=F32)


def _dot_nt(a, b):
    return lax.dot_general(a, b, (((1,), (1,)), ((), ())), preferred_element_type=F32)


def _dot_tn(a, b):
    return lax.dot_general(a, b, (((0,), (0,)), ((), ())), preferred_element_type=F32)


ROW_SHAPE = (1, D_MODEL)


def _rows_copy(mat_ref, rows_hbm, row0, n, sem, to_hbm):
    rows = rows_hbm.at[pl.ds(row0, n), 0, :]
    return pltpu.make_async_copy(mat_ref, rows, sem) if to_hbm else pltpu.make_async_copy(rows, mat_ref, sem)


def _ada_kernel(c_ref, w_ref, b_ref, o_ref):
    c = c_ref[...]
    a = (c * _sigmoid(c)).astype(BF16)
    o_ref[...] = _dot(a, w_ref[...].astype(BF16)) + b_ref[...]


def _ada(c, w_ada, b_ada):
    rows, d = c.shape
    n = w_ada.shape[1]
    bn = ADA_COLS
    return pl.pallas_call(
        _ada_kernel,
        grid=(n // bn,),
        in_specs=[pl.BlockSpec((rows, d), lambda j: (0, 0)),
                  pl.BlockSpec((d, bn), lambda j: (0, j)),
                  pl.BlockSpec((1, bn), lambda j: (0, j))],
        out_specs=pl.BlockSpec((rows, bn), lambda j: (0, j)),
        out_shape=jax.ShapeDtypeStruct((rows, n), F32),
        compiler_params=_cparams(("arbitrary",)),
        name="ada",
    )(c, w_ada, b_ada.reshape(1, n))


def _inproj_common(x_ref, g_ref, b_ref, sc_ref, sh_ref, w_ref, zhg_ref, kf_ref, vf_ref):
    x0 = _layer_norm(x_ref[...], g_ref[...], b_ref[...])
    h = (x0 * (1.0 + sc_ref[...]) + sh_ref[...]).astype(BF16)
    zhg_ref[...] = _dot(h, w_ref[:, 0:HG_COLS])
    c0 = HG_COLS + GROUP_W
    k = _dot(h, w_ref[:, c0:c0 + GROUP_W])
    v = _dot(h, w_ref[:, c0 + GROUP_W:c0 + 2 * GROUP_W])
    for hd in range(HEADS):
        kf_ref[:, hd, :] = k[:, hd * HEAD_W:(hd + 1) * HEAD_W]
        vf_ref[:, hd, :] = v[:, hd * HEAD_W:(hd + 1) * HEAD_W]
    return h, k, v


def _inproj_prompt_kernel(x_ref, g_ref, b_ref, sc_ref, sh_ref, w_ref, wqt_ref, seg_ref,
                          zhg_ref, kf_ref, vf_ref, kb_ref, qt_ref, vt_ref, kn_ref):
    h, k, v = _inproj_common(x_ref, g_ref, b_ref, sc_ref, sh_ref, w_ref, zhg_ref, kf_ref, vf_ref)
    kb_ref[...] = k.astype(BF16)
    kn_ref[...] = jnp.max(_dot((k * k).astype(BF16), seg_ref[...]), axis=0, keepdims=True)
    qt_ref[...] = (_dot_nt(wqt_ref[...], h) * (DA_HEAD ** -0.5 * LOG2E)).astype(BF16)
    vt_ref[...] = v.T.astype(BF16)


def _inproj_sample_kernel(x_ref, g_ref, b_ref, sc_ref, sh_ref, w_ref, zhg_ref, kf_ref, vf_ref, q_ref):
    h, _, _ = _inproj_common(x_ref, g_ref, b_ref, sc_ref, sh_ref, w_ref, zhg_ref, kf_ref, vf_ref)
    q_ref[...] = (_dot(h, w_ref[:, HG_COLS:HG_COLS + GROUP_W]) * (DA_HEAD ** -0.5)).astype(BF16)


def _mod_spec(mod, bm):
    if mod.shape[1] == 1:
        return pl.BlockSpec((None, 1, mod.shape[2]), lambda g, i: (g, 0, 0))
    return pl.BlockSpec((None, bm, mod.shape[2]), lambda g, i: (g, i, 0))


def _inproj(x, sc, sh, ln_g, ln_b, w_in_bf, bm, transposed=None):
    G, R, D = x.shape
    nb = R // bm
    row = lambda w: pl.BlockSpec((None, bm, w), lambda g, i: (g, i, 0))
    full = lambda a: pl.BlockSpec(a.shape, lambda g, i: (0,) * a.ndim)
    heads = pl.BlockSpec((None, bm, HEADS, HEAD_W), lambda g, i: (g, i, 0, 0))
    heads_shape = jax.ShapeDtypeStruct((G, R, HEADS, HEAD_W), F32)
    sds = lambda w, dt: jax.ShapeDtypeStruct((G, R, w), dt)
    args = [x, ln_g, ln_b, sc, sh, w_in_bf]
    in_specs = [row(D), full(ln_g), full(ln_b), _mod_spec(sc, bm), _mod_spec(sh, bm), full(w_in_bf)]
    out_specs = [row(HG_COLS), heads, heads, row(GROUP_W)]
    out_shape = [sds(HG_COLS, F32), heads_shape, heads_shape, sds(GROUP_W, BF16)]
    body = _inproj_sample_kernel
    if transposed is not None:
        wqt, tblk = transposed
        per = tblk // bm
        tr = pl.BlockSpec((None, None, GROUP_W, bm), lambda g, i: (g, i // per, 0, i % per))
        tr_shape = jax.ShapeDtypeStruct((G, R // tblk, GROUP_W, tblk), BF16)
        seg = jnp.asarray(np.arange(GROUP_W)[:, None] // HEAD_W == np.arange(HEAD_W)[None, :], BF16)
        args += [wqt, seg]
        in_specs += [full(wqt), full(seg)]
        out_specs += [tr, tr, pl.BlockSpec((None, None, 1, HEAD_W), lambda g, i: (g, i, 0, 0))]
        out_shape += [tr_shape, tr_shape, jax.ShapeDtypeStruct((G, nb, 1, HEAD_W), F32)]
        body = _inproj_prompt_kernel
    return pl.pallas_call(
        body,
        grid=(G, nb),
        in_specs=in_specs,
        out_specs=out_specs,
        out_shape=out_shape,
        compiler_params=_cparams(("arbitrary", "arbitrary")),
        name="inproj",
    )(*args)


def _hgrn_gates(zq, zf, lb):
    q = zq * _sigmoid(zq)
    f = lb + (1.0 - lb) * _sigmoid(zf)
    k = (1.0 - lb) * _sigmoid(-zf)
    return q, jnp.log(f), k


def _hgrn_kernel(z_ref, lb_ref, ng_ref, lvl_ref, tri_ref, o_ref, sfin_ref, st_ref, *, C):
    t = pl.program_id(1)

    @pl.when(t == 0)
    def _():
        st_ref[...] = jnp.zeros_like(st_ref)

    lvl = lvl_ref[...]
    tri = tri_ref[...]
    n_levels = int(math.log2(C)) - 3
    ng = ng_ref[...]
    for h in range(HEADS):
        cs = slice(h * HEAD_W, (h + 1) * HEAD_W)
        zq = z_ref[:, h * HEAD_W:(h + 1) * HEAD_W]
        zf = z_ref[:, GROUP_W + h * HEAD_W:GROUP_W + (h + 1) * HEAD_W]
        v = z_ref[:, 2 * GROUP_W + h * HEAD_W:2 * GROUP_W + (h + 1) * HEAD_W].astype(BF16)
        zg = z_ref[:, 3 * GROUP_W + h * HEAD_W:3 * GROUP_W + (h + 1) * HEAD_W]
        q, g, k = _hgrn_gates(zq, zf, lb_ref[:, cs])
        g1 = g.astype(BF16)
        r1 = g - g1.astype(F32)
        g2 = r1.astype(BF16)
        g3 = (r1 - g2.astype(F32)).astype(BF16)
        b = _dot(tri, g1) + _dot(tri, g2) + _dot(tri, g3)

        b8 = b.reshape(C // SUBLANES, SUBLANES, HEAD_W)
        mid = SUBLANES // 2 - 1
        bmid = jnp.broadcast_to(b8[:, mid:mid + 1, :], b8.shape).reshape(C, HEAD_W)
        e = jnp.clip(b - bmid, -EXP_CLAMP, EXP_CLAMP)
        a = jnp.where(lvl == 0, _dot_nt((q * jnp.exp(e)).astype(BF16), (k * jnp.exp(-e)).astype(BF16)), 0.0)
        for li in range(1, n_levels + 1):
            m = 4 << li
            bb = b.reshape(C // (2 * m), 2 * m, HEAD_W)
            d = b - jnp.broadcast_to(bb[:, m - 1:m, :], bb.shape).reshape(C, HEAD_W)
            qs = (q * jnp.exp(jnp.minimum(d, 0.0))).astype(BF16)
            ks = (k * jnp.exp(jnp.minimum(-d, 0.0))).astype(BF16)
            a = jnp.where(lvl == li, _dot_nt(qs, ks), a)

        st = st_ref[h]
        o = _dot(a.astype(BF16), v) + _dot_nt((q * jnp.exp(b)).astype(BF16), st.astype(BF16))
        b_last = b[C - 1:C, :]
        kd = (k * jnp.exp(b_last - b)).astype(BF16)
        st_ref[h] = st * jnp.exp(b_last) + _dot_tn(v, kd)

        ms = jnp.mean(o * o, -1, keepdims=True)
        o_ref[:, cs] = (o * lax.rsqrt(ms + RMS_EPS) * ng * (zg * _sigmoid(zg))).astype(BF16)

    @pl.when(t == pl.num_programs(1) - 1)
    def _():
        for h in range(HEADS):
            sfin_ref[h] = st_ref[h].T


def _hgrn_level_table(C):
    t = np.arange(C)[:, None]
    s = np.arange(C)[None, :]
    x = t ^ s
    lvl = np.zeros((C, C), np.int32)
    m = 8
    while m < C:
        lvl += (x >= m).astype(np.int32)
        m *= 2
    return np.where(s <= t, lvl, -1).astype(np.int32)


def _hgrn_prompt(zhg, lb, norm_g):
    B, T, _ = zhg.shape
    C = HG_CHUNK
    lvl = jnp.asarray(_hgrn_level_table(C))
    tri = jnp.asarray(np.tril(np.ones((C, C), np.float32)), BF16)
    full = lambda a: pl.BlockSpec(a.shape, lambda b, t: (0,) * a.ndim)
    return pl.pallas_call(
        functools.partial(_hgrn_kernel, C=C),
        grid=(B, T // C),
        in_specs=[pl.BlockSpec((None, C, HG_COLS), lambda b, t: (b, t, 0)), full(lb), full(norm_g), full(lvl),
                  full(tri)],
        out_specs=[pl.BlockSpec((None, C, GROUP_W), lambda b, t: (b, t, 0)),
                   pl.BlockSpec((None, HEADS, HEAD_W, HEAD_W), lambda b, t: (b, 0, 0, 0))],
        out_shape=[jax.ShapeDtypeStruct((B, T, GROUP_W), BF16),
                   jax.ShapeDtypeStruct((B, HEADS, HEAD_W, HEAD_W), F32)],
        scratch_shapes=[pltpu.VMEM((HEADS, HEAD_W, HEAD_W), F32)],
        compiler_params=_cparams(("arbitrary", "arbitrary")),
        name="hgrn_prompt",
    )(zhg, lb, norm_g, lvl, tri)


def _hgrn_step_kernel(z_ref, s_ref, lb_ref, ng_ref, o_ref, so_ref, *, G):
    ng = ng_ref[...]
    for h in range(HEADS):
        cs = slice(h * HEAD_W, (h + 1) * HEAD_W)
        zq = z_ref[:, h * HEAD_W:(h + 1) * HEAD_W]
        zf = z_ref[:, GROUP_W + h * HEAD_W:GROUP_W + (h + 1) * HEAD_W]
        v = z_ref[:, 2 * GROUP_W + h * HEAD_W:2 * GROUP_W + (h + 1) * HEAD_W]
        zg = z_ref[:, 3 * GROUP_W + h * HEAD_W:3 * GROUP_W + (h + 1) * HEAD_W]
        lb = lb_ref[:, cs]
        q = zq * _sigmoid(zq)
        f = lb + (1.0 - lb) * _sigmoid(zf)
        k = (1.0 - lb) * _sigmoid(-zf)
        qT, fT, kT = q.T, f.T, k.T
        rows = []
        for j in range(G):
            s_new = fT[:, j:j + 1] * s_ref[j, h] + kT[:, j:j + 1] * v[j:j + 1, :]
            so_ref[j, h] = s_new
            rows.append(jnp.sum(s_new * qT[:, j:j + 1], axis=0, keepdims=True))
        o = jnp.concatenate(rows, axis=0)
        ms = jnp.mean(o * o, -1, keepdims=True)
        o_ref[:, cs] = (o * lax.rsqrt(ms + RMS_EPS) * ng * (zg * _sigmoid(zg))).astype(BF16)


def _hgrn_sample(zhg, state, lb, norm_g):
    N = zhg.shape[0]
    G = 8
    full = lambda a: pl.BlockSpec(a.shape, lambda i: (0,) * a.ndim)
    st_spec = pl.BlockSpec((G, HEADS, HEAD_W, HEAD_W), lambda i: (i, 0, 0, 0))
    return pl.pallas_call(
        functools.partial(_hgrn_step_kernel, G=G),
        grid=(N // G,),
        in_specs=[pl.BlockSpec((G, HG_COLS), lambda i: (i, 0)), st_spec, full(lb), full(norm_g)],
        out_specs=[pl.BlockSpec((G, GROUP_W), lambda i: (i, 0)), st_spec],
        out_shape=[jax.ShapeDtypeStruct((N, GROUP_W), BF16), jax.ShapeDtypeStruct(state.shape, F32)],
        compiler_params=_cparams(("arbitrary",)),
        name="hgrn_sample",
    )(zhg, state, lb, norm_g)


def _attn_kernel(qt_ref, k_ref, vt_ref, kpm_ref, sl_ref, feat_ref, lam_ref, gcol_ref, o_ref, m_ref, l_ref, acc_ref,
                 *, blk):
    qi = pl.program_id(2)
    row = lax.broadcasted_iota(I32, (HEAD_W, blk), 0)
    qt = qt_ref[...].astype(F32)
    a_hi = sl_ref[0:1, :]
    a_lo = sl_ref[1:2, :]
    slope2 = sl_ref[2:3, :]
    def slope_rows(r0):
        in_rows = (row >= r0) & (row < r0 + 4)
        return jnp.where(in_rows, jnp.where((row - r0) % 2 == 0, a_hi, a_lo), 0.0)

    q_aug = [jnp.where(row < DA_HEAD, qt, slope_rows(DA_HEAD)).astype(BF16),
             jnp.where(row >= DA_HEAD, qt, slope_rows(0)).astype(BF16)]
    keep = [feat_ref[0] > 0, feat_ref[1] > 0]
    feats = [feat_ref[2], feat_ref[3]]
    m_ref[...] = jnp.full_like(m_ref, -jnp.inf)
    l_ref[...] = jnp.zeros_like(l_ref)
    acc_ref[...] = jnp.zeros_like(acc_ref)

    def block(kj, masked):
        k_start = pl.multiple_of(kj * blk, blk)
        kb = k_ref[pl.ds(k_start, blk), :]
        vt = vt_ref[kj]
        off = slope2 * jnp.full((1, blk), k_start - qi * blk, I32).astype(F32)
        if masked:
            ok = lax.broadcasted_iota(I32, (blk, 1), 0) <= lax.broadcasted_iota(I32, (1, blk), 1)
        for mi in range(2):
            st = _dot(jnp.where(keep[mi], kb, feats[mi]), q_aug[mi])
            if masked:
                st = jnp.where(ok, st, -jnp.inf)
            m_prev = m_ref[mi]
            m_new = jnp.maximum(m_prev, jnp.max(st, axis=0, keepdims=True) + off)
            p = jnp.exp2(st - (m_new - off))
            alpha = jnp.exp2(m_prev - m_new)
            l_ref[mi] = alpha * l_ref[mi] + jnp.sum(p, axis=0, keepdims=True)
            acc_ref[mi] = alpha * acc_ref[mi] + _dot(vt, p.astype(BF16))
            m_ref[mi] = m_new

    block(qi, True)

    nblk = kpm_ref.shape[1]
    jlane = lax.broadcasted_iota(I32, (1, nblk), 1)
    far = slope2[:, 0:1] * ((jlane - qi) * blk + (blk - 1)).astype(F32)
    need = jlane < 0
    for mi in range(2):
        qm = qt[mi * DA_HEAD:(mi + 1) * DA_HEAD, :]
        qn = jnp.sqrt(jnp.max(jnp.sum(qm * qm, axis=0, keepdims=True), axis=1, keepdims=True))
        m_lo = jnp.min(m_ref[mi], axis=1, keepdims=True)
        need = need | (1.02 * qn * kpm_ref[mi:mi + 1, :] + 1.0 + far - m_lo >= -SKIP_LOG2)
    n_visit = jnp.sum((need & (jlane < qi)).astype(I32))

    def body(kj, c):
        block(kj, False)
        return c

    lax.fori_loop(qi - n_visit, qi, body, 0)

    ot = acc_ref[0] * (1.0 / l_ref[0]) - lam_ref[...] * (acc_ref[1] * (1.0 / l_ref[1]))
    ms = jnp.mean(ot * ot, axis=0, keepdims=True)
    o_ref[...] = (ot * lax.rsqrt(ms + RMS_EPS) * gcol_ref[...]).T.astype(BF16)


def _attn_prompt(qt, k, vt, kn2, lam, subln_g):
    B, nblk, _, blk = qt.shape
    T = k.shape[1]
    bf = lambda x: np.asarray(x, np.float32).astype(BF16).astype(np.float64)
    a = np.asarray(ALIBI_SLOPES, np.float64) * LOG2E
    a_hi = bf(a)
    a_lo = bf(a - a_hi)
    sl = np.zeros((HEADS, 8, blk), np.float32)
    sl[:, 0, :], sl[:, 1, :], sl[:, 2, :] = a_hi[:, None], a_lo[:, None], a.astype(np.float32)[:, None]
    lane = np.arange(HEAD_W)[None, :]
    r = np.arange(blk)[:, None]
    assert blk <= 256 * 256

    def pos_feat(l0):
        return (np.where((lane == l0) | (lane == l0 + 1), r // 256 * 256, 0)
                + np.where((lane == l0 + 2) | (lane == l0 + 3), r % 256, 0)).astype(np.float32)

    feat = np.stack([np.broadcast_to(lane < DA_HEAD, (blk, HEAD_W)).astype(np.float32),
                     np.broadcast_to(lane >= DA_HEAD, (blk, HEAD_W)).astype(np.float32),
                     pos_feat(DA_HEAD), pos_feat(0)])
    feat = jnp.asarray(feat, BF16)
    lam_row = jnp.full((1, blk), lam, F32)
    gcol = jnp.broadcast_to((subln_g.reshape(HEAD_W, 1) * (1.0 - LAM_INIT)), (HEAD_W, blk))
    kn = jnp.sqrt(1.02 * jnp.max(kn2[:, :, 0, :HEADS].reshape(B, nblk, -1, HEADS), axis=2))
    upto = np.arange(nblk)[:, None] >= np.arange(nblk)[None, :]
    kpm = jnp.max(jnp.where(upto[None, :, :, None], kn[:, None, :, :], 0.0), axis=2)
    kpm = jnp.broadcast_to(kpm.transpose(0, 2, 1)[:, :, None, :], (B, HEADS, 2, nblk))
    full = lambda x: pl.BlockSpec(x.shape, lambda b, h, i: (0,) * x.ndim)
    return pl.pallas_call(
        functools.partial(_attn_kernel, blk=blk),
        grid=(B, HEADS, nblk),
        in_specs=[pl.BlockSpec((None, None, HEAD_W, blk), lambda b, h, i: (b, i, h, 0)),
                  pl.BlockSpec((None, T, HEAD_W), lambda b, h, i: (b, 0, h)),
                  pl.BlockSpec((None, nblk, HEAD_W, blk), lambda b, h, i: (b, 0, h, 0)),
                  pl.BlockSpec((None, None, 2, nblk), lambda b, h, i: (b, h, 0, 0)),
                  pl.BlockSpec((None, 8, blk), lambda b, h, i: (h, 0, 0)),
                  full(feat), full(lam_row), full(gcol)],
        out_specs=pl.BlockSpec((None, blk, HEAD_W), lambda b, h, i: (b, i, h)),
        out_shape=jax.ShapeDtypeStruct((B, T, GROUP_W), BF16),
        scratch_shapes=[pltpu.VMEM((2, 1, blk), F32), pltpu.VMEM((2, 1, blk), F32), pltpu.VMEM((2, HEAD_W, blk), F32)],
        compiler_params=_cparams(("arbitrary", "arbitrary", "arbitrary")),
        name="attn_prompt",
    )(qt, k, vt, kpm, jnp.asarray(sl), feat, lam_row, gcol)


def _attn_decode_kernel(pt_ref, q_ref, kn_ref, vn_ref, bias_ref, lam_ref, g_ref, ck_hbm, cv_hbm, o_ref,
                        kbuf, vbuf, sems, *, n_pages):
    i = pl.program_id(0)
    slot = i % 2

    def page_copies(seq, s):
        for pg in range(n_pages):
            page = pt_ref[seq, pg]
            for h in range(HEADS):
                dst = (s, h, pl.ds(pg * PAGE_SIZE, PAGE_SIZE))
                yield pltpu.make_async_copy(ck_hbm.at[0, page, :, h, :], kbuf.at[dst], sems.at[0, s])
                yield pltpu.make_async_copy(cv_hbm.at[0, page, :, h, :], vbuf.at[dst], sems.at[1, s])

    @pl.when(i == 0)
    def _():
        for cp in page_copies(0, 0):
            cp.start()

    @pl.when(i + 1 < pl.num_programs(0))
    def _():
        for cp in page_copies(i + 1, 1 - slot):
            cp.start()

    for cp in page_copies(i, slot):
        cp.wait()

    r8 = lax.broadcasted_iota(I32, (8, GROUP_W), 0)
    c8 = lax.broadcasted_iota(I32, (8, GROUP_W), 1)
    qmat = jnp.where(c8 // DA_HEAD == r8, jnp.broadcast_to(q_ref[...].astype(F32), (8, GROUP_W)), 0.0)
    qmat_bf = qmat.astype(BF16)
    s = bias_ref[...]
    for h in range(HEADS):
        s = s + _dot_nt(qmat_bf[:, h * HEAD_W:(h + 1) * HEAD_W], kbuf[slot, h].astype(BF16))
    kn = kn_ref[...].astype(BF16).astype(F32)
    s_self = jnp.sum(qmat_bf.astype(F32) * kn, axis=-1, keepdims=True)
    m = jnp.maximum(jnp.max(s, -1, keepdims=True), s_self)
    p = jnp.exp(s - m)
    p_self = jnp.exp(s_self - m)
    inv_l = 1.0 / (jnp.sum(p, -1, keepdims=True) + p_self)
    coef = jnp.where(lax.broadcasted_iota(I32, (8, 1), 0) % 2 == 0, 1.0, -lam_ref[:, 0:1]) * inv_l
    w = (p * coef).astype(BF16)
    accs = [_dot(w, vbuf[slot, h].astype(BF16)) for h in range(HEADS)]
    acc = jnp.concatenate(accs, axis=1) + (p_self * coef) * vn_ref[...]
    o = jnp.sum(jnp.where(c8 // HEAD_W == r8 // 2, acc, 0.0), axis=0, keepdims=True)
    outs = []
    for h in range(HEADS):
        oh = o[:, h * HEAD_W:(h + 1) * HEAD_W]
        ms = jnp.mean(oh * oh, -1, keepdims=True)
        outs.append(oh * lax.rsqrt(ms + RMS_EPS) * g_ref[...] * (1.0 - LAM_INIT))
    o_ref[...] = jnp.concatenate(outs, axis=1).astype(BF16)


def _attn_sample(q, k_new, v_new, cache_k, cache_v, page_table, lam_row, subln_g):
    N, n_pages = page_table.shape
    past = n_pages * PAGE_SIZE
    kpos = np.arange(past, dtype=np.float32)[None, :]
    slope_rows = np.repeat(np.asarray(ALIBI_SLOPES, np.float32), 2)[:, None]
    bias = jnp.asarray(-slope_rows * (past - kpos))
    row = pl.BlockSpec((None, 1, GROUP_W), lambda i, pt: (i, 0, 0))
    full = lambda a: pl.BlockSpec(a.shape, lambda i, pt: (0,) * a.ndim)
    anyspec = pl.BlockSpec(memory_space=pl.ANY)
    grid_spec = pltpu.PrefetchScalarGridSpec(
        num_scalar_prefetch=1,
        grid=(N,),
        in_specs=[row, row, row, full(bias), full(lam_row), full(subln_g), anyspec, anyspec],
        out_specs=row,
        scratch_shapes=[pltpu.VMEM((2, HEADS, past, HEAD_W), F32), pltpu.VMEM((2, HEADS, past, HEAD_W), F32),
                        pltpu.SemaphoreType.DMA((2, 2))],
    )
    r3 = lambda a: a.reshape(N, 1, GROUP_W)
    out = pl.pallas_call(
        functools.partial(_attn_decode_kernel, n_pages=n_pages),
        grid_spec=grid_spec,
        out_shape=jax.ShapeDtypeStruct((N, 1, GROUP_W), BF16),
        compiler_params=_cparams(("arbitrary",)),
        name="attn_sample",
    )(page_table, r3(q), r3(k_new), r3(v_new), bias, lam_row, subln_g, cache_k, cache_v)
    return out.reshape(N, GROUP_W)


def _outproj_kernel(hg_ref, da_ref, x_ref, lg_ref, lb_ref, g1_ref, sc2_ref, sh2_ref, w_ref, l1g_ref, l1b_ref,
                    wrh_ref, wrl_ref, br_ref, u_ref, cin_ref,
                    x1_ref, h2_hbm, idx_ref, tw_ref, rank_ref, cout_ref, run_ref, hbuf, hsem, *, n_steps):
    step = pl.program_id(0) * pl.num_programs(1) + pl.program_id(1)
    slot = step % 2
    bm = x_ref.shape[0]
    h2_copies = lambda st, s: [_rows_copy(hbuf.at[s], h2_hbm, st * bm, bm, hsem.at[s], True)]

    @pl.when(step == 0)
    def _():
        run_ref[...] = cin_ref[...]

    x0 = _layer_norm(x_ref[...], lg_ref[...], lb_ref[...])
    mix = _dot(hg_ref[...], w_ref[0:GROUP_W, :]) + _dot(da_ref[...], w_ref[GROUP_W:2 * GROUP_W, :])
    x1 = _layer_norm(DEEPNORM_ALPHA * x0 + (1.0 + g1_ref[...]) * mix, l1g_ref[...], l1b_ref[...])
    x1_ref[...] = x1
    h2 = x1 * (1.0 + sc2_ref[...]) + sh2_ref[...]

    @pl.when(step >= 2)
    def _():
        for cp in h2_copies(step - 2, slot):
            cp.wait()

    hbuf[slot] = h2
    for cp in h2_copies(step, slot):
        cp.start()

    @pl.when(step == n_steps - 1)
    def _():
        tail = h2_copies(step, slot)
        if n_steps > 1:
            tail = h2_copies(step - 1, 1 - slot) + tail
        for cp in tail:
            cp.wait()

    hi = h2.astype(BF16)
    lo = (h2 - hi.astype(F32)).astype(BF16)
    wrh = wrh_ref[...]
    logits = _dot_nt(wrh, hi) + _dot_nt(wrh, lo) + _dot_nt(wrl_ref[...], hi) + br_ref[...]

    n_e, bm = logits.shape
    rows = lax.broadcasted_iota(I32, (n_e, bm), 0).astype(F32)
    vals, sels = [], []
    work = logits
    for kk in range(TOP_K):
        mx = jnp.max(work, axis=0, keepdims=True)
        ix = jnp.min(jnp.where(work == mx, rows, float(n_e)), axis=0, keepdims=True)
        sel = rows == ix
        idx_ref[kk:kk + 1, :] = ix.astype(I32)
        vals.append(mx)
        sels.append(sel)
        work = jnp.where(sel, -jnp.inf, work)
    es = [jnp.exp(vv - vals[0]) for vv in vals]
    inv = 1.0 / (es[0] + es[1] + es[2] + es[3])
    for kk in range(TOP_K):
        tw_ref[kk:kk + 1, :] = es[kk] * inv

    base = run_ref[...]
    for kk in range(TOP_K):
        oh = jnp.where(sels[kk], 1.0, 0.0)
        before = base + _dot(oh.astype(BF16), u_ref[...])
        rank_ref[kk:kk + 1, :] = jnp.sum(jnp.where(sels[kk], before, 0.0), axis=0, keepdims=True).astype(I32)
        base = base + jnp.sum(oh, axis=1, keepdims=True)
    run_ref[...] = base
    cout_ref[...] = base


def _outproj(hg, da, x, mods, consts, counts_in, bm):
    G, R, D = x.shape
    g1, sc2, sh2 = mods
    ln_g, ln_b, w_out_bf, l1g, l1b, wrh, wrl, br = consts
    nb = R // bm
    n_tok = G * R
    u = jnp.asarray(np.triu(np.ones((bm, bm), np.float32), 1), BF16)
    row = lambda w: pl.BlockSpec((None, bm, w), lambda g, i: (g, i, 0))
    full = lambda a: pl.BlockSpec(a.shape, lambda g, i: (0,) * a.ndim)
    tok_lanes = pl.BlockSpec((TOP_K, bm), lambda g, i: (0, g * nb + i))
    return pl.pallas_call(
        functools.partial(_outproj_kernel, n_steps=G * nb),
        grid=(G, nb),
        in_specs=[row(GROUP_W), row(GROUP_W), row(D), full(ln_g), full(ln_b),
                  _mod_spec(g1, bm), _mod_spec(sc2, bm), _mod_spec(sh2, bm),
                  full(w_out_bf), full(l1g), full(l1b), full(wrh), full(wrl), full(br), full(u), full(counts_in)],
        out_specs=[row(D), pl.BlockSpec(memory_space=pl.ANY), tok_lanes, tok_lanes, tok_lanes, full(counts_in)],
        out_shape=[jax.ShapeDtypeStruct((G, R, D), F32),
                   jax.ShapeDtypeStruct((n_tok,) + ROW_SHAPE, F32),
                   jax.ShapeDtypeStruct((TOP_K, n_tok), I32),
                   jax.ShapeDtypeStruct((TOP_K, n_tok), F32),
                   jax.ShapeDtypeStruct((TOP_K, n_tok), I32),
                   jax.ShapeDtypeStruct(counts_in.shape, F32)],
        scratch_shapes=[pltpu.VMEM(counts_in.shape, F32), pltpu.VMEM((2, bm, D), F32), pltpu.SemaphoreType.DMA((2,))],
        compiler_params=_cparams(("arbitrary", "arbitrary")),
        name="outproj",
    )(hg, da, x, ln_g, ln_b, g1, sc2, sh2, w_out_bf, l1g, l1b, wrh, wrl, br, u, counts_in)


def _dispatch_kernel(pe_ref, nu_ref, dest_ref, h2p_ref, h2s_ref, xb_hbm, zbuf, zsem, sem, *, bt, n_prompt_steps, nb):
    i = pl.program_id(0)
    bm = MOE_BLOCK

    @pl.when(i == 0)
    def _():
        zbuf[...] = jnp.zeros_like(zbuf)
        zero_block = lambda row0: pltpu.make_async_copy(zbuf, xb_hbm.at[pl.ds(row0, bm)], zsem)
        has_rows = [pe_ref[e] > (pe_ref[e - 1] if e else 0) for e in range(N_EXPERTS)]
        for e in range(N_EXPERTS):
            @pl.when(has_rows[e])
            def _():
                zero_block(pe_ref[e] - bm).start()

        def tail_start(j, c):
            zero_block(j * bm).start()
            return c

        def tail_wait(j, c):
            zero_block(j * bm).wait()
            return c

        lax.fori_loop(nu_ref[0], nb, tail_start, 0)
        for e in range(N_EXPERTS):
            @pl.when(has_rows[e])
            def _():
                zero_block(pe_ref[e] - bm).wait()
        lax.fori_loop(nu_ref[0], nb, tail_wait, 0)

    def scatter(h2_ref):
        def body(t, c):
            for kk in range(TOP_K):
                pltpu.make_async_copy(h2_ref.at[t], xb_hbm.at[dest_ref[0, t * TOP_K + kk]], sem).start(priority=kk % 2)
            return c

        lax.fori_loop(0, bt, body, 0, unroll=4)
        for kk in range(TOP_K):
            pltpu.make_async_copy(h2_ref, xb_hbm.at[pl.ds(0, bt)], sem).wait()

    @pl.when(i < n_prompt_steps)
    def _():
        scatter(h2p_ref)

    @pl.when(i >= n_prompt_steps)
    def _():
        scatter(h2s_ref)


def _dispatch(pad_end, n_used, dest, h2_p, h2_s, n_rows):
    bt = TOK_BLOCK
    n_p, n_s = h2_p.shape[0] // bt, h2_s.shape[0] // bt
    nb = n_rows // MOE_BLOCK
    grid_spec = pltpu.PrefetchScalarGridSpec(
        num_scalar_prefetch=2,
        grid=(n_p + n_s,),
        in_specs=[pl.BlockSpec((None, 1, TOP_K * bt), lambda i, pe, nu: (i, 0, 0), memory_space=pltpu.SMEM),
                  pl.BlockSpec((bt,) + ROW_SHAPE, lambda i, pe, nu: (jnp.minimum(i, n_p - 1), 0, 0)),
                  pl.BlockSpec((bt,) + ROW_SHAPE, lambda i, pe, nu: (jnp.maximum(i - n_p, 0), 0, 0))],
        out_specs=pl.BlockSpec(memory_space=pl.ANY),
        scratch_shapes=[pltpu.VMEM((MOE_BLOCK,) + ROW_SHAPE, F32), pltpu.SemaphoreType.DMA(()),
                        pltpu.SemaphoreType.DMA(())],
    )
    return pl.pallas_call(
        functools.partial(_dispatch_kernel, bt=bt, n_prompt_steps=n_p, nb=nb),
        grid_spec=grid_spec,
        out_shape=jax.ShapeDtypeStruct((n_rows,) + ROW_SHAPE, F32),
        compiler_params=_cparams(("arbitrary",)),
        name="dispatch",
    )(pad_end, n_used, dest, h2_p, h2_s)


def _experts_kernel(be_ref, nu_ref, nxt_ref, par_ref, xb_hbm, wu_hbm, bu_ref, wd_hbm, bd_ref, yb_hbm,
                    wu_bf, wd_bf, wu_f32, wd_f32, xbuf, ybuf, sems, wsems, *, bm):
    j = pl.program_id(0)
    nb = pl.num_programs(0)
    slot = j % 2
    nu = nu_ref[0]
    active = j < nu
    x_copies = lambda blk, s: [_rows_copy(xbuf.at[s], xb_hbm, blk * bm, bm, sems.at[0, s], False)]
    y_copies = lambda blk, s: [_rows_copy(ybuf.at[s], yb_hbm, blk * bm, bm, sems.at[1, s], True)]
    w_copies = lambda e, s: [pltpu.make_async_copy(wu_hbm.at[e], wu_f32.at[s], wsems.at[0, s]),
                             pltpu.make_async_copy(wd_hbm.at[e], wd_f32.at[s], wsems.at[1, s])]

    @pl.when(j == 0)
    def _():
        for cp in x_copies(0, 0) + w_copies(be_ref[0], par_ref[0]):
            cp.start()

    @pl.when(j + 1 < nu)
    def _():
        for cp in x_copies(j + 1, 1 - slot):
            cp.start()

    @pl.when(j >= 2)
    def _():
        for cp in y_copies(j - 2, slot):
            cp.wait()

    @pl.when(active & ((j == 0) | (be_ref[j] != be_ref[jnp.maximum(j - 1, 0)])))
    def _():
        half = par_ref[j]
        for cp in w_copies(be_ref[j], half):
            cp.wait()
        wu_bf[...] = wu_f32[half].astype(BF16)
        wd_bf[...] = wd_f32[half].astype(BF16)

        @pl.when(nxt_ref[j] >= 0)
        def _():
            for cp in w_copies(nxt_ref[j], 1 - half):
                cp.start()

    @pl.when(active)
    def _():
        for cp in x_copies(j, slot):
            cp.wait()
        x = xbuf[slot].astype(BF16)
        y = bd_ref[...]
        hw = D_FF // 2
        for c0 in range(0, D_FF, hw):
            glu = _dot(x, wu_bf[:, c0:c0 + hw]) + bu_ref[:, c0:c0 + hw]
            lin = _dot(x, wu_bf[:, D_FF + c0:D_FF + c0 + hw]) + bu_ref[:, D_FF + c0:D_FF + c0 + hw]
            glu = jnp.minimum(glu, SWIGLU_LIMIT)
            lin = jnp.clip(lin, -SWIGLU_LIMIT, SWIGLU_LIMIT)
            act = glu * _sigmoid(SWIGLU_ALPHA * glu) * (lin + 1.0)
            y = y + _dot(act.astype(BF16), wd_bf[c0:c0 + hw, :])
        ybuf[slot] = y

    @pl.when(jnp.logical_not(active))
    def _():
        ybuf[slot] = jnp.zeros((bm, D_MODEL), F32)

    for cp in y_copies(j, slot):
        cp.start()

    @pl.when(j == nb - 1)
    def _():
        for cp in y_copies(j - 1, 1 - slot) + y_copies(j, slot):
            cp.wait()


def _experts(blk_exp, n_used, blk_next, blk_half, xb, w_up, b_up, w_down, b_down):
    n_rows = xb.shape[0]
    bm = MOE_BLOCK
    nb = n_rows // bm
    anyspec = pl.BlockSpec(memory_space=pl.ANY)
    grid_spec = pltpu.PrefetchScalarGridSpec(
        num_scalar_prefetch=4,
        grid=(nb,),
        in_specs=[anyspec, anyspec,
                  pl.BlockSpec((None, 1, 2 * D_FF), lambda j, be, nu, nx, hf: (be[j], 0, 0)),
                  anyspec,
                  pl.BlockSpec((None, 1, D_MODEL), lambda j, be, nu, nx, hf: (be[j], 0, 0))],
        out_specs=anyspec,
        scratch_shapes=[pltpu.VMEM((D_MODEL, 2 * D_FF), BF16), pltpu.VMEM((D_FF, D_MODEL), BF16),
                        pltpu.VMEM((2, D_MODEL, 2 * D_FF), F32), pltpu.VMEM((2, D_FF, D_MODEL), F32),
                        pltpu.VMEM((2, bm, D_MODEL), F32), pltpu.VMEM((2, bm, D_MODEL), F32),
                        pltpu.SemaphoreType.DMA((2, 2)), pltpu.SemaphoreType.DMA((2, 2))],
    )
    return pl.pallas_call(
        functools.partial(_experts_kernel, bm=bm),
        grid_spec=grid_spec,
        out_shape=jax.ShapeDtypeStruct((n_rows,) + ROW_SHAPE, F32),
        compiler_params=_cparams(("arbitrary",)),
        name="experts",
    )(blk_exp, n_used, blk_next, blk_half, xb, w_up, b_up.reshape(N_EXPERTS, 1, 2 * D_FF), w_down,
      b_down.reshape(N_EXPERTS, 1, D_MODEL))


def _combine_kernel(dcur_ref, dnext_ref, x1_ref, tw_ref, g2_ref, lg_ref, lb_ref, yb_hbm, o_ref, buf, sems, *, bm):
    i = pl.program_id(0) * pl.num_programs(1) + pl.program_id(1)
    n = pl.num_programs(0) * pl.num_programs(1)
    slot = i % 2
    n_groups = bm // SUBLANES

    def issue(dref, s):
        def body(a, c):
            for b in range(SUBLANES):
                for kk in range(TOP_K):
                    src = yb_hbm.at[dref[0, (a * SUBLANES + b) * TOP_K + kk]]
                    pltpu.make_async_copy(src, buf.at[s, kk, a, pl.ds(b, 1), :], sems.at[s]).start(priority=kk % 2)
            return c
        lax.fori_loop(0, n_groups, body, 0)

    @pl.when(i == 0)
    def _():
        issue(dcur_ref, 0)

    @pl.when(i + 1 < n)
    def _():
        issue(dnext_ref, 1 - slot)

    for kk in range(TOP_K):
        for a in range(n_groups):
            pltpu.make_async_copy(yb_hbm.at[pl.ds(0, SUBLANES), 0, :], buf.at[slot, kk, a], sems.at[slot]).wait()

    tw = tw_ref[...]
    rows_of = lambda kk: buf[slot, kk].reshape(bm, D_MODEL)
    ff = tw[:, 0:1] * rows_of(0)
    for kk in range(1, TOP_K):
        ff = ff + tw[:, kk:kk + 1] * rows_of(kk)
    o_ref[...] = _layer_norm(DEEPNORM_ALPHA * x1_ref[...] + (1.0 + g2_ref[...]) * ff, lg_ref[...], lb_ref[...])


def _combine(dest, tw_rows, x1, g2, ln_g, ln_b, yb, bm):
    G, R, D = x1.shape
    nb = R // bm
    n_blk = G * nb
    row = lambda w: pl.BlockSpec((None, bm, w), lambda g, i: (g, i, 0))
    full = lambda a: pl.BlockSpec(a.shape, lambda g, i: (0,) * a.ndim)
    cur = pl.BlockSpec((None, 1, TOP_K * bm), lambda g, i: (g * nb + i, 0, 0), memory_space=pltpu.SMEM)
    nxt = pl.BlockSpec((None, 1, TOP_K * bm), lambda g, i: (jnp.minimum(g * nb + i + 1, n_blk - 1), 0, 0),
                       memory_space=pltpu.SMEM)
    return pl.pallas_call(
        functools.partial(_combine_kernel, bm=bm),
        grid=(G, nb),
        in_specs=[cur, nxt, row(D), pl.BlockSpec((bm, TOP_K), lambda g, i: (g * nb + i, 0)),
                  _mod_spec(g2, bm), full(ln_g), full(ln_b), pl.BlockSpec(memory_space=pl.ANY)],
        out_specs=row(D),
        out_shape=jax.ShapeDtypeStruct((G, R, D), F32),
        scratch_shapes=[pltpu.VMEM((2, TOP_K, bm // SUBLANES, SUBLANES, D), F32), pltpu.SemaphoreType.DMA((2,))],
        compiler_params=_cparams(("arbitrary", "arbitrary")),
        name="combine",
    )(dest, dest, x1, tw_rows, g2, ln_g, ln_b, yb)


def kernel(x_prompt, x_sample, c_prompt, c_sample, cache_k, cache_v, state_hgrn, page_table, ln_in_g, ln_in_b, w_ada, b_ada, w_in, hg_lb, hg_norm_g, da_lq1, da_lk1, da_lq2, da_lk2, da_subln_g, w_out, ln1_g, ln1_b, w_router, b_router, w_up, b_up, w_down, b_down, ln2_g, ln2_b):
    assert w_in.shape[0] == 1, "single-layer trunk"
    B, T, D = x_prompt.shape
    NS = x_sample.shape[0]
    n_prompt = B * T
    n_tot = n_prompt + NS
    r1 = lambda a: a.reshape(1, -1)

    lb = r1(jax.nn.softmax(hg_lb.astype(F32), axis=0)[0])
    lam = (jnp.exp(jnp.sum(da_lq1[0].astype(F32) * da_lk1[0].astype(F32)))
           - jnp.exp(jnp.sum(da_lq2[0].astype(F32) * da_lk2[0].astype(F32))) + LAM_INIT)
    lam_row = jnp.full((1, HEAD_W), lam, F32)
    ln_g, ln_b = r1(ln_in_g), r1(ln_in_b)
    w_in_bf = w_in[0].astype(BF16)
    wqt = w_in_bf[:, HG_COLS:HG_COLS + GROUP_W].T
    w_out_bf = w_out[0].astype(BF16)
    wr_t = w_router[0].T
    wrh = wr_t.astype(BF16)
    wrl = (wr_t - wrh.astype(F32)).astype(BF16)
    br = b_router[0].reshape(N_EXPERTS, 1)
    norm_g, subln_g = r1(hg_norm_g[0]), r1(da_subln_g[0])

    n_c = B + NS
    c_all = jnp.concatenate([c_prompt, c_sample, jnp.zeros((-n_c % 8, D), F32)], axis=0)
    mod = _ada(c_all, w_ada[0], b_ada[0])
    mod_p = [mod[:B, j * D:(j + 1) * D].reshape(B, 1, D) for j in range(6)]
    mod_s = [mod[B:n_c, j * D:(j + 1) * D].reshape(1, NS, D) for j in range(6)]

    zhg_p, kf_p, vf_p, kb_p, qt_p, vt_p, kn2_p = _inproj(
        x_prompt, mod_p[1], mod_p[0], ln_g, ln_b, w_in_bf, INPROJ_BLOCK, (wqt, ATTN_BLOCK))
    hg_p, s_p = _hgrn_prompt(zhg_p, lb, norm_g)
    da_p = _attn_prompt(qt_p, kb_p, vt_p, kn2_p, lam, subln_g)

    xs = x_sample.reshape(1, NS, D)
    zhg_s, kf_s, vf_s, q_s = _inproj(xs, mod_s[1], mod_s[0], ln_g, ln_b, w_in_bf, NS)
    hg_s, s_s = _hgrn_sample(zhg_s[0], state_hgrn[0], lb, norm_g)
    da_s = _attn_sample(q_s[0], kf_s.reshape(NS, GROUP_W), vf_s.reshape(NS, GROUP_W), cache_k, cache_v, page_table,
                        lam_row, subln_g)

    consts = (ln_g, ln_b, w_out_bf, r1(ln1_g[0]), r1(ln1_b[0]), wrh, wrl, br)
    counts0 = jnp.zeros((N_EXPERTS, 1), F32)
    x1_p, h2_p, idx_p, tw_p, rank_p, counts1 = _outproj(
        hg_p, da_p, x_prompt, (mod_p[2], mod_p[4], mod_p[3]), consts, counts0, ROW_BLOCK)
    x1_s, h2_s, idx_s, tw_s, rank_s, counts = _outproj(
        hg_s.reshape(1, NS, GROUP_W), da_s.reshape(1, NS, GROUP_W), xs, (mod_s[2], mod_s[4], mod_s[3]), consts,
        counts1, NS)

    cnt = counts[:, 0].astype(I32)
    padded = (cnt + MOE_BLOCK - 1) // MOE_BLOCK * MOE_BLOCK
    pad_end = jnp.cumsum(padded)
    pad_start = pad_end - padded
    e_ids = jnp.arange(N_EXPERTS, dtype=I32)

    def slot_of(idx, rank):
        return jnp.sum(jnp.where(idx[..., None] == e_ids, pad_start, 0), axis=-1) + rank

    dest_p = slot_of(idx_p, rank_p)
    dest_s = slot_of(idx_s, rank_s)
    n_blocks = -(-n_tot * TOP_K // MOE_BLOCK) + N_EXPERTS
    blk_row0 = jnp.arange(n_blocks, dtype=I32) * MOE_BLOCK
    blk_exp = jnp.minimum(jnp.sum((pad_end[None, :] <= blk_row0[:, None]).astype(I32), axis=1), N_EXPERTS - 1)
    n_used = (pad_end[-1:] // MOE_BLOCK).astype(I32)

    by_block = lambda d: d.T.reshape(-1, 1, TOK_BLOCK * TOP_K)
    slots_p, slots_s = by_block(dest_p), by_block(dest_s)
    xb = _dispatch(pad_end.astype(I32), n_used, jnp.concatenate([slots_p, slots_s], axis=0), h2_p, h2_s,
                   n_blocks * MOE_BLOCK)
    has_rows = padded > 0
    later = (e_ids[None, :] > e_ids[:, None]) & has_rows[None, :]
    next_e = jnp.where(jnp.any(later, axis=1), jnp.min(jnp.where(later, e_ids[None, :], N_EXPERTS), axis=1), -1)
    half_e = (jnp.cumsum(has_rows.astype(I32)) - 1) % 2
    per_block = lambda tab: jnp.sum(jnp.where(blk_exp[:, None] == e_ids, tab, 0), axis=1).astype(I32)
    yb = _experts(blk_exp, n_used, per_block(next_e), per_block(half_e), xb, w_up[0], b_up[0], w_down[0], b_down[0])

    l2g, l2b = r1(ln2_g[0]), r1(ln2_b[0])
    y_p = _combine(slots_p, tw_p.T, x1_p, mod_p[5], l2g, l2b, yb, TOK_BLOCK)
    y_s = _combine(slots_s, tw_s.T, x1_s, mod_s[5], l2g, l2b, yb, TOK_BLOCK)

    smp = lambda a: a.reshape(1, NS, 1, HEADS, HEAD_W)
    return (y_p, y_s.reshape(NS, 1, D), kf_p[None], vf_p[None], s_p[None], smp(kf_s), smp(vf_s), s_s[None])
```
